```python
import math
import jax
import jax.numpy as jnp
from jax import lax
import numpy as np

D_MODEL = 1024
BATCH = 1
SEQ = 16384
DEPTH = 2

N_EVEN = (DEPTH + 1) // 2
N_ODD = DEPTH // 2
D_FF = 2816
NORM_EPS = 1e-6
MIX_W = D_MODEL // 2

RET_HEADS = 4
RET_DK = MIX_W // RET_HEADS
RET_CHUNK = 128
ROPE_BASE = 10000.0
HEAD_NORM_EPS = 1e-6

SSM_HEAD_DIM = 64
SSM_HEADS = MIX_W // SSM_HEAD_DIM
SSM_GROUPS = 2
SSM_STATE = 64
SSM_CONV = 4
SSM_CHUNK = 128
SSM_XBC = MIX_W + 2 * SSM_GROUPS * SSM_STATE
SSM_NORM_EPS = 1e-5
DT_MIN = 1e-3
DT_MAX = 1e-1

LRU_BLOCKS = 4
LRU_BLOCK = MIX_W // LRU_BLOCKS
LRU_CONV = 4
LRU_C = 8.0

RWKV_HEAD_DIM = 64
RWKV_HEADS = MIX_W // RWKV_HEAD_DIM
DECAY_LORA = 64
ICL_LORA = 64
GATE_LORA = 160
RWKV_COLS = 3 * MIX_W + DECAY_LORA + ICL_LORA + GATE_LORA
RWKV_LN_EPS = 64e-5

AB_IN = 4 * MIX_W + MIX_W + SSM_XBC + SSM_HEADS
AB_SPLITS = (MIX_W, 2 * MIX_W, 3 * MIX_W, 4 * MIX_W, 5 * MIX_W, 5 * MIX_W + SSM_XBC)
CD_IN = 2 * MIX_W + RWKV_COLS
CD_SPLITS = (MIX_W, 2 * MIX_W)
RWKV_SPLITS = (MIX_W, 2 * MIX_W, 3 * MIX_W, 3 * MIX_W + DECAY_LORA, 3 * MIX_W + DECAY_LORA + ICL_LORA)

kernel_name = 'hybrid_macaron_retnet_mamba2_rglru_rwkv7'

F32 = jnp.float32


def rms_norm(x, g, eps=NORM_EPS):
    xf = x.astype(F32)
    y = xf * lax.rsqrt(jnp.mean(xf * xf, axis=-1, keepdims=True) + eps)
    return (y * g.astype(F32)).astype(x.dtype)


def swiglu_ffn(h, wg, wu, wd):
    return (jax.nn.silu(h @ wg) * (h @ wu)) @ wd


def causal_dwconv(x, w, b):
    k_width, ch = w.shape
    y = lax.conv_general_dilated(x, w[:, None, :].astype(x.dtype), window_strides=(1,),
                                 padding=[(k_width - 1, 0)],
                                 dimension_numbers=('NWC', 'WIO', 'NWC'),
                                 feature_group_count=ch)
    return y + b


def token_shift(p, mu):
    prev = jnp.pad(p, ((0, 0), (1, 0), (0, 0)))[:, :-1]
    return p + (prev - p) * mu


def rotary(x):
    t_len, d = x.shape[1], x.shape[-1]
    inv_freq = ROPE_BASE ** (-jnp.arange(0, d, 2, dtype=F32) / d)
    ang = jnp.arange(t_len, dtype=F32)[:, None] * inv_freq[None, :]
    cos = jnp.cos(ang)[None, :, None, :]
    sin = jnp.sin(ang)[None, :, None, :]
    x1, x2 = jnp.split(x.astype(F32), 2, axis=-1)
    return jnp.concatenate([x1 * cos - x2 * sin, x2 * cos + x1 * sin], axis=-1)


def retention_chunked(q, k, v):
    bsz, t_len, nh, dk = q.shape
    dv = v.shape[-1]
    c = RET_CHUNK
    nc = t_len // c
    log_g = jnp.log1p(-(2.0 ** (-5.0 - jnp.arange(nh, dtype=F32))))
    q = q.reshape(bsz, nc, c, nh, dk)
    k = k.reshape(bsz, nc, c, nh, dk)
    v = v.reshape(bsz, nc, c, nh, dv)
    pos = jnp.arange(c, dtype=F32)
    rel = pos[:, None] - pos[None, :]
    decay_in = jnp.where(rel >= 0, jnp.exp(jnp.maximum(rel, 0.0)[None] * log_g[:, None, None]), 0.0)
    scores = jnp.einsum('bnihd,bnjhd->bnhij', q, k) * decay_in
    o = jnp.einsum('bnhij,bnjhe->bnihe', scores, v)
    k_to_end = jnp.exp((c - 1 - pos)[:, None] * log_g[None, :])
    kv = jnp.einsum('bnjhd,jh,bnjhe->nbhde', k, k_to_end, v)
    chunk_decay = jnp.exp(c * log_g)[None, :, None, None]

    def step(s, kv_n):
        return s * chunk_decay + kv_n, s

    _, s_prev = lax.scan(step, jnp.zeros((bsz, nh, dk, dv), F32), kv)
    q_from_start = jnp.exp((pos + 1.0)[:, None] * log_g[None, :])
    o = o + jnp.einsum('bnihd,ih,nbhde->bnihe', q, q_from_start, s_prev)
    return o.reshape(bsz, t_len, nh, dv)


def ssd_chunked(xdt, a_dt, bm, cm):
    bsz, t_len, nh, hp = xdt.shape
    ng, ns = bm.shape[-2:]
    hg = nh // ng
    l = SSM_CHUNK
    nc = t_len // l
    x = xdt.reshape(bsz, nc, l, ng, hg, hp)
    bm = bm.reshape(bsz, nc, l, ng, ns)
    cm = cm.reshape(bsz, nc, l, ng, ns)
    a = a_dt.reshape(bsz, nc, l, ng, hg).transpose(0, 3, 4, 1, 2)
    a_cs = jnp.cumsum(a, axis=-1)
    causal = jnp.tril(jnp.ones((l, l), dtype=bool))
    seg = a_cs[..., :, None] - a_cs[..., None, :]
    decay_in = jnp.exp(jnp.where(causal, seg, -jnp.inf))
    cb = jnp.einsum('bclgn,bcsgn->bgcls', cm, bm)
    y_diag = jnp.einsum('bgcls,bghcls,bcsghp->bclghp', cb, decay_in, x)
    decay_to_end = jnp.exp(a_cs[..., -1:] - a_cs)
    chunk_states = jnp.einsum('bclgn,bghcl,bclghp->cbghpn', bm, decay_to_end, x)
    chunk_decay = jnp.exp(a_cs[..., -1]).transpose(3, 0, 1, 2)

    def step(s, inp):
        st, dec = inp
        return s * dec[..., None, None] + st, s

    s0 = jnp.zeros(chunk_states.shape[1:], F32)
    _, s_prev = lax.scan(step, s0, (chunk_states, chunk_decay))
    y_off = jnp.einsum('bclgn,cbghpn,bghcl->bclghp', cm, s_prev, jnp.exp(a_cs))
    return (y_diag + y_off).reshape(bsz, t_len, nh, hp)


def mamba2_mix(z, xbc, dt, conv_w, conv_b, dt_bias, a_log, d_skip, norm_g):
    bsz, t_len, _ = z.shape
    xbc = jax.nn.silu(causal_dwconv(xbc, conv_w, conv_b))
    xs, bm, cm = jnp.split(xbc, (MIX_W, MIX_W + SSM_GROUPS * SSM_STATE), axis=-1)
    dt = jax.nn.softplus(dt + dt_bias.astype(F32))
    a = -jnp.exp(a_log.astype(F32))
    xs = xs.reshape(bsz, t_len, SSM_HEADS, SSM_HEAD_DIM)
    y = ssd_chunked(xs * dt[..., None], dt * a,
                    bm.reshape(bsz, t_len, SSM_GROUPS, SSM_STATE),
                    cm.reshape(bsz, t_len, SSM_GROUPS, SSM_STATE))
    y = y + d_skip.astype(F32)[:, None] * xs
    y = (y.reshape(bsz, t_len, MIX_W) * jax.nn.silu(z)).reshape(bsz, t_len, SSM_GROUPS, MIX_W // SSM_GROUPS)
    y = y * lax.rsqrt(jnp.mean(y * y, axis=-1, keepdims=True) + SSM_NORM_EPS)
    return y.reshape(bsz, t_len, MIX_W) * norm_g.astype(F32)


def mix_ab(h, w_in, conv_w, conv_b, dt_bias, a_log, d_skip, norm_g, w_out):
    bsz, t_len, _ = h.shape
    p = (h @ w_in).astype(F32)
    q, k, v, g, z, xbc, dt = jnp.split(p, AB_SPLITS, axis=-1)
    q = rotary(q.reshape(bsz, t_len, RET_HEADS, RET_DK))
    k = rotary(k.reshape(bsz, t_len, RET_HEADS, RET_DK)) * (RET_DK ** -0.5)
    o = retention_chunked(q, k, v.reshape(bsz, t_len, RET_HEADS, RET_DK))
    o = o * lax.rsqrt(jnp.mean(o * o, axis=-1, keepdims=True) + HEAD_NORM_EPS)
    y_ret = jax.nn.silu(g) * o.reshape(bsz, t_len, MIX_W)
    y_ssm = mamba2_mix(z, xbc, dt, conv_w, conv_b, dt_bias, a_log, d_skip, norm_g)
    y = jnp.concatenate([y_ret, y_ssm], axis=-1).astype(h.dtype)
    return y @ w_out


def rglru_mix(xb, gb, conv_w, conv_b, wa, ba, wx, bx, lam):
    bsz, t_len, _ = xb.shape
    xb = causal_dwconv(xb, conv_w, conv_b)
    xh = xb.reshape(bsz, t_len, LRU_BLOCKS, LRU_BLOCK)
    r = jax.nn.sigmoid(jnp.einsum('btnd,nde->btne', xh, wa) + ba).reshape(bsz, t_len, MIX_W)
    i = jax.nn.sigmoid(jnp.einsum('btnd,nde->btne', xh, wx) + bx).reshape(bsz, t_len, MIX_W)
    log_a = -LRU_C * r * jax.nn.softplus(-lam.astype(F32))
    a = jnp.exp(log_a)
    u = jnp.sqrt(-jnp.expm1(2.0 * log_a)) * (i * xb)

    def combine(left, right):
        a_l, h_l = left
        a_r, h_r = right
        return a_l * a_r, a_r * h_l + h_r

    _, hs = lax.associative_scan(combine, (a, u), axis=1)
    return hs * jax.nn.gelu(gb)


def rwkv7_recurrence(r, w, k, v, kk, b):
    bsz, _, nh, nd = r.shape

    def step(s, inp):
        r_t, w_t, k_t, v_t, kk_t, b_t = inp
        s_kk = jnp.einsum('bhvk,bhk->bhv', s, kk_t)
        s = s * w_t[:, :, None, :] - s_kk[..., None] * b_t[:, :, None, :] + v_t[..., None] * k_t[:, :, None, :]
        return s, jnp.einsum('bhvk,bhk->bhv', s, r_t)

    seq = (r.transpose(1, 0, 2, 3), w.transpose(1, 0, 2, 3), k.transpose(1, 0, 2, 3),
           v.transpose(1, 0, 2, 3), kk.transpose(1, 0, 2, 3), b.transpose(1, 0, 2, 3))
    _, y = lax.scan(step, jnp.zeros((bsz, nh, nd, nd), F32), seq)
    return y.transpose(1, 0, 2, 3)


def rwkv7_mix(p, mu, w0, w2, a0, a2, g2, k_k, k_a, r_k, ln_g, ln_b):
    bsz, t_len, _ = p.shape
    p = token_shift(p, mu)
    r, k, v, wl, al, gl = jnp.split(p, RWKV_SPLITS, axis=-1)
    w = -jax.nn.softplus(-(w0 + jnp.tanh(wl) @ w2)) - 0.5
    decay = jnp.exp(-jnp.exp(w))
    a = jax.nn.sigmoid(a0 + al @ a2)
    g = jax.nn.sigmoid(gl) @ g2

    def heads(t):
        return t.reshape(bsz, t_len, RWKV_HEADS, RWKV_HEAD_DIM)

    kk = heads(k * k_k)
    kk = kk / jnp.maximum(jnp.sqrt(jnp.sum(kk * kk, axis=-1, keepdims=True)), 1e-12)
    k = k * (1.0 + (a - 1.0) * k_a)
    rh, kh, vh = heads(r), heads(k), heads(v)
    y = rwkv7_recurrence(rh, heads(decay), kh, vh, kk, kk * heads(a))
    mean = jnp.mean(y, axis=-1, keepdims=True)
    var = jnp.mean(jnp.square(y - mean), axis=-1, keepdims=True)
    y = ((y - mean) * lax.rsqrt(var + RWKV_LN_EPS)).reshape(bsz, t_len, MIX_W) * ln_g + ln_b
    bonus = jnp.sum(rh * kh * r_k, axis=-1, keepdims=True) * vh
    y = y + bonus.reshape(bsz, t_len, MIX_W)
    return y * g


def mix_cd(h, w_in, lru_conv_w, lru_conv_b, lru_wa, lru_ba, lru_wx, lru_bx, lru_lambda,
           mu, w0, w2, a0, a2, g2, k_k, k_a, r_k, ln_g, ln_b, w_out):
    p = (h @ w_in).astype(F32)
    xb, gb, pr = jnp.split(p, CD_SPLITS, axis=-1)
    y_lru = rglru_mix(xb, gb, lru_conv_w, lru_conv_b, lru_wa, lru_ba, lru_wx, lru_bx, lru_lambda)
    y_rwkv = rwkv7_mix(pr, mu, w0, w2, a0, a2, g2, k_k, k_a, r_k, ln_g, ln_b)
    y = jnp.concatenate([y_lru, y_rwkv], axis=-1).astype(h.dtype)
    return y @ w_out


def setup_inputs(seed: int = 0) -> dict:
    key = jax.random.key(seed)
    keys = iter(jax.random.split(key, 64))

    def normal(shape, scale):
        return scale * jax.random.normal(next(keys), shape, F32)

    def uniform(shape, lo, hi):
        return jax.random.uniform(next(keys), shape, F32, lo, hi)

    def gain(shape):
        return 1.0 + 0.02 * jax.random.normal(next(keys), shape, F32)

    d = D_MODEL
    x = normal((BATCH, SEQ, d), 1.0)
    dt0 = jnp.exp(uniform((N_EVEN, SSM_HEADS), math.log(DT_MIN), math.log(DT_MAX)))
    lru_a = uniform((N_ODD, MIX_W), 0.9, 0.999) ** (1.0 / LRU_C)
    ratio = jnp.arange(MIX_W, dtype=F32) / (MIX_W - 1)
    return {
        'x': x,
        'ffn1_norm': gain((DEPTH, d)),
        'ffn1_wg': normal((DEPTH, d, D_FF), d ** -0.5),
        'ffn1_wu': normal((DEPTH, d, D_FF), d ** -0.5),
        'ffn1_wd': normal((DEPTH, D_FF, d), D_FF ** -0.5),
        'mix_norm': gain((DEPTH, d)),
        'ffn2_norm': gain((DEPTH, d)),
        'ffn2_wg': normal((DEPTH, d, D_FF), d ** -0.5),
        'ffn2_wu': normal((DEPTH, d, D_FF), d ** -0.5),
        'ffn2_wd': normal((DEPTH, D_FF, d), D_FF ** -0.5),
        'ab_w_in': normal((N_EVEN, d, AB_IN), d ** -0.5),
        'ab_w_out': normal((N_EVEN, 2 * MIX_W, d), (2 * MIX_W) ** -0.5),
        'ssm_conv_w': normal((N_EVEN, SSM_CONV, SSM_XBC), SSM_CONV ** -0.5),
        'ssm_conv_b': normal((N_EVEN, SSM_XBC), 0.01),
        'ssm_dt_bias': dt0 + jnp.log(-jnp.expm1(-dt0)),
        'ssm_a_log': jnp.log(uniform((N_EVEN, SSM_HEADS), 1.0, 16.0)),
        'ssm_d': gain((N_EVEN, SSM_HEADS)),
        'ssm_norm': gain((N_EVEN, MIX_W)),
        'cd_w_in': normal((N_ODD, d, CD_IN), d ** -0.5),
        'cd_w_out': normal((N_ODD, 2 * MIX_W, d), (2 * MIX_W) ** -0.5),
        'lru_conv_w': normal((N_ODD, LRU_CONV, MIX_W), LRU_CONV ** -0.5),
        'lru_conv_b': normal((N_ODD, MIX_W), 0.01),
        'lru_wa': normal((N_ODD, LRU_BLOCKS, LRU_BLOCK, LRU_BLOCK), LRU_BLOCK ** -0.5),
        'lru_ba': normal((N_ODD, LRU_BLOCKS, LRU_BLOCK), 0.01),
        'lru_wx': normal((N_ODD, LRU_BLOCKS, LRU_BLOCK, LRU_BLOCK), LRU_BLOCK ** -0.5),
        'lru_bx': normal((N_ODD, LRU_BLOCKS, LRU_BLOCK), 0.01),
        'lru_lambda': jnp.log(lru_a) - jnp.log1p(-lru_a),
        'rwkv_mu': uniform((N_ODD, RWKV_COLS), 0.0, 1.0),
        'rwkv_w0': -6.5 + 5.0 * ratio ** 0.85 + normal((N_ODD, MIX_W), 0.1),
        'rwkv_w2': normal((N_ODD, DECAY_LORA, MIX_W), 0.1),
        'rwkv_a0': normal((N_ODD, MIX_W), 0.1),
        'rwkv_a2': normal((N_ODD, ICL_LORA, MIX_W), 0.1),
        'rwkv_g2': normal((N_ODD, GATE_LORA, MIX_W), GATE_LORA ** -0.5),
        'rwkv_kk': 0.85 + normal((N_ODD, MIX_W), 0.05),
        'rwkv_ka': 1.0 + normal((N_ODD, MIX_W), 0.05),
        'rwkv_rk': normal((N_ODD, RWKV_HEADS, RWKV_HEAD_DIM), 0.1),
        'rwkv_ln_g': gain((N_ODD, MIX_W)),
        'rwkv_ln_b': normal((N_ODD, MIX_W), 0.01),
        'final_norm': gain((d,)),
    }


def reference(x, ffn1_norm, ffn1_wg, ffn1_wu, ffn1_wd, mix_norm, ffn2_norm, ffn2_wg, ffn2_wu, ffn2_wd,
              ab_w_in, ab_w_out, ssm_conv_w, ssm_conv_b, ssm_dt_bias, ssm_a_log, ssm_d, ssm_norm,
              cd_w_in, cd_w_out, lru_conv_w, lru_conv_b, lru_wa, lru_ba, lru_wx, lru_bx, lru_lambda,
              rwkv_mu, rwkv_w0, rwkv_w2, rwkv_a0, rwkv_a2, rwkv_g2, rwkv_kk, rwkv_ka, rwkv_rk,
              rwkv_ln_g, rwkv_ln_b, final_norm):
    for layer in range(DEPTH):
        j = layer // 2
        x = x + 0.5 * swiglu_ffn(rms_norm(x, ffn1_norm[layer]), ffn1_wg[layer], ffn1_wu[layer],
                                 ffn1_wd[layer]).astype(x.dtype)
        h = rms_norm(x, mix_norm[layer])
        if layer % 2 == 0:
            y = mix_ab(h, ab_w_in[j], ssm_conv_w[j], ssm_conv_b[j], ssm_dt_bias[j], ssm_a_log[j],
                       ssm_d[j], ssm_norm[j], ab_w_out[j])
        else:
            y = mix_cd(h, cd_w_in[j], lru_conv_w[j], lru_conv_b[j], lru_wa[j], lru_ba[j], lru_wx[j],
                       lru_bx[j], lru_lambda[j], rwkv_mu[j], rwkv_w0[j], rwkv_w2[j], rwkv_a0[j],
                       rwkv_a2[j], rwkv_g2[j], rwkv_kk[j], rwkv_ka[j], rwkv_rk[j], rwkv_ln_g[j],
                       rwkv_ln_b[j], cd_w_out[j])
        x = x + y.astype(x.dtype)
        x = x + 0.5 * swiglu_ffn(rms_norm(x, ffn2_norm[layer]), ffn2_wg[layer], ffn2_wu[layer],
                                 ffn2_wd[layer]).astype(x.dtype)
    return rms_norm(x, final_norm)
```

```python
import functools
import math

import jax
import jax.numpy as jnp
from jax import lax
from jax.experimental import pallas as pl
from jax.experimental.pallas import tpu as pltpu

F32 = jnp.float32
BF16 = jnp.bfloat16

D_MODEL = 1024
D_FF = 2816
NORM_EPS = 1e-6
MIX_W = 512

RET_HEADS = 4
RET_DK = 128
RET_CHUNK = 128
ROPE_BASE = 10000.0
HEAD_NORM_EPS = 1e-6

SSM_HEADS = 8
SSM_HEAD_DIM = 64
SSM_GROUPS = 2
SSM_STATE = 64
SSM_CONV = 4
SSM_CHUNK = 128
SSM_XBC = 768
SSM_NORM_EPS = 1e-5

LRU_BLOCKS = 4
LRU_BLOCK = 128
LRU_CONV = 4
LRU_C = 8.0

RWKV_HEADS = 8
RWKV_HEAD_DIM = 64
DECAY_LORA = 64
ICL_LORA = 64
GATE_LORA = 160
RWKV_LN_EPS = 64e-5
RWKV_CHUNK = 64

LANES = 128
SUBLANES = 8
VMEM_LIMIT = 56 * 1024 * 1024

FFN_TM = 1024
FFN_TF = 256
MIX_TM = 256

AB_Q, AB_K, AB_V, AB_G, AB_Z, AB_XBC, AB_DT = 0, 512, 1024, 1536, 2048, 2560, 3328
AB_COLS = 3456
CD_XB, CD_GB, CD_PR = 0, 512, 1024
CD_PR_COLS = 2048
CD_COLS = CD_PR + CD_PR_COLS


def _bdot(a, b):
    return jnp.dot(a.astype(BF16), b.astype(BF16), preferred_element_type=F32)


def _bdot_nt(a, b):
    return lax.dot_general(a.astype(BF16), b.astype(BF16), (((1,), (1,)), ((), ())),
                           preferred_element_type=F32)


def _bdot_tn(a, b):
    return lax.dot_general(a.astype(BF16), b.astype(BF16), (((0,), (0,)), ((), ())),
                           preferred_element_type=F32)


def _split3(x):
    hi = x.astype(BF16)
    r1 = x - hi.astype(F32)
    mid = r1.astype(BF16)
    lo = (r1 - mid.astype(F32)).astype(BF16)
    return hi, mid, lo


def _sel_dot_lhs(x, sel):
    hi, mid, lo = _split3(x)
    d = lambda p: jnp.dot(p, sel, preferred_element_type=F32)
    return d(hi) + (d(mid) + d(lo))


def _sel_dot_rhs(sel, x):
    hi, mid, lo = _split3(x)
    d = lambda p: jnp.dot(sel, p, preferred_element_type=F32)
    return d(hi) + (d(mid) + d(lo))


def _dot_hi(a, b):
    a_hi = a.astype(BF16)
    a_lo = (a - a_hi.astype(F32)).astype(BF16)
    b_hi = b.astype(BF16)
    b_lo = (b - b_hi.astype(F32)).astype(BF16)
    d = lambda p, q: jnp.dot(p, q, preferred_element_type=F32)
    return d(a_hi, b_hi) + (d(a_hi, b_lo) + d(a_lo, b_hi))


def _rms_norm(x, g, eps):
    return x * lax.rsqrt(jnp.mean(x * x, axis=-1, keepdims=True) + eps) * g


def _silu(x):
    return x * jax.nn.sigmoid(x)


def _softplus(x):
    return jnp.maximum(x, 0.0) + jnp.log1p(jnp.exp(-jnp.abs(x)))


def _gelu_tanh(x):
    c = math.sqrt(2.0 / math.pi)
    return 0.5 * x * (1.0 + jnp.tanh(c * (x + 0.044715 * (x * x * x))))


def _causal_conv(ext_ref, x, w_ref, b_ref, tm, width):
    ext_ref[pl.ds(SUBLANES, tm), :] = x
    acc = b_ref[...] + w_ref[width - 1:width, :] * x
    for k in range(width - 1):
        acc = acc + w_ref[k:k + 1, :] * ext_ref[pl.ds(SUBLANES - (width - 1) + k, tm), :]
    ext_ref[pl.ds(0, SUBLANES), :] = ext_ref[pl.ds(tm, SUBLANES), :]
    return acc


def _ffn_body(x_ref, g_ref, wg_ref, wu_ref, wd_ref, fin_ref, o_ref, h_sc, acc_sc, *, final_norm):
    j = pl.program_id(1)

    @pl.when(j == 0)
    def _():
        h_sc[...] = _rms_norm(x_ref[...], g_ref[...], NORM_EPS).astype(BF16)
        acc_sc[...] = jnp.zeros_like(acc_sc)

    h = h_sc[...]
    gate = jnp.dot(h, wg_ref[...], preferred_element_type=F32)
    up = jnp.dot(h, wu_ref[...], preferred_element_type=F32)
    act = (_silu(gate) * up).astype(BF16)
    acc_sc[...] += jnp.dot(act, wd_ref[...], preferred_element_type=F32)

    @pl.when(j == pl.num_programs(1) - 1)
    def _():
        y = x_ref[...] + 0.5 * acc_sc[...]
        if final_norm:
            y = _rms_norm(y, fin_ref[...], NORM_EPS)
        o_ref[...] = y


def _ffn(x, norm_g, wg, wu, wd, fin_g, *, final_norm):
    t_len, d = x.shape
    tm = min(FFN_TM, t_len)
    tf = FFN_TF
    grid = (t_len // tm, D_FF // tf)
    return pl.pallas_call(
        functools.partial(_ffn_body, final_norm=final_norm),
        grid=grid,
        in_specs=[
            pl.BlockSpec((tm, d), lambda i, j: (i, 0)),
            pl.BlockSpec((1, d), lambda i, j: (0, 0)),
            pl.BlockSpec((d, tf), lambda i, j: (0, j)),
            pl.BlockSpec((d, tf), lambda i, j: (0, j)),
            pl.BlockSpec((tf, d), lambda i, j: (j, 0)),
            pl.BlockSpec((1, d), lambda i, j: (0, 0)),
        ],
        out_specs=pl.BlockSpec((tm, d), lambda i, j: (i, 0)),
        out_shape=jax.ShapeDtypeStruct((t_len, d), F32),
        scratch_shapes=[pltpu.VMEM((tm, d), BF16), pltpu.VMEM((tm, d), F32)],
        compiler_params=pltpu.CompilerParams(
            dimension_semantics=("arbitrary", "arbitrary"), vmem_limit_bytes=VMEM_LIMIT),
        name="ffn_final" if final_norm else "ffn",
    )(x, norm_g.reshape(1, d), wg, wu, wd, fin_g.reshape(1, d))


def _ab_body(x_ref, cos_ref, sin_ref, g_ref, win_ref, rdec_ref, rqfs_ref, rkte_ref, rcd_ref,
             cw_ref, cb_ref, dtb_ref, alog_ref, dexp_ref, sng_ref, exp_ref, tri_ref, wout_ref,
             o_ref, sret_sc, sssm_sc, ext_sc, *, tm):
    @pl.when(pl.program_id(0) == 0)
    def _():
        sret_sc[...] = jnp.zeros_like(sret_sc)
        sssm_sc[...] = jnp.zeros_like(sssm_sc)
        ext_sc[pl.ds(0, SUBLANES), :] = jnp.zeros((SUBLANES, SSM_XBC), F32)

    x = x_ref[...]
    h = _rms_norm(x, g_ref[...], NORM_EPS).astype(BF16)

    def proj(lo, width):
        return jnp.dot(h, win_ref[:, lo:lo + width], preferred_element_type=F32)

    cos2 = cos_ref[...]
    sin2 = sin_ref[...]

    def rotary(t):
        return t * cos2 + pltpu.roll(t, RET_DK // 2, axis=1) * sin2

    q_all = proj(AB_Q, MIX_W)
    k_all = proj(AB_K, MIX_W)
    v_all = proj(AB_V, MIX_W)
    gate_ret = _silu(proj(AB_G, MIX_W))
    scale = RET_DK ** -0.5
    c_len = RET_CHUNK
    ret_rows = []
    for c in range(tm // c_len):
        rs = slice(c * c_len, (c + 1) * c_len)
        heads = []
        for hh in range(RET_HEADS):
            ls = slice(hh * RET_DK, (hh + 1) * RET_DK)
            qh = q_all[rs, ls] * cos2[rs] + pltpu.roll(q_all[rs, ls], RET_DK // 2, axis=1) * sin2[rs]
            kh = (k_all[rs, ls] * cos2[rs] + pltpu.roll(k_all[rs, ls], RET_DK // 2, axis=1) * sin2[rs]) * scale
            vh = v_all[rs, ls]
            s_prev = sret_sc[hh]
            scores = _bdot_nt(qh, kh) * rdec_ref[hh]
            o = _bdot(scores, vh) + _bdot(qh * rqfs_ref[:, ls], s_prev)
            sret_sc[hh] = s_prev * rcd_ref[:, ls] + _bdot_tn(kh * rkte_ref[:, ls], vh)
            o = o * lax.rsqrt(jnp.mean(o * o, axis=-1, keepdims=True) + HEAD_NORM_EPS)
            heads.append(o)
        ret_rows.append(jnp.concatenate(heads, axis=1))
    y_ret = gate_ret * jnp.concatenate(ret_rows, axis=0)

    z = proj(AB_Z, MIX_W)
    xbc = _silu(_causal_conv(ext_sc, proj(AB_XBC, SSM_XBC), cw_ref, cb_ref, tm, SSM_CONV))
    xs = xbc[:, :MIX_W]
    bm = xbc[:, MIX_W:MIX_W + LANES]
    cm = xbc[:, MIX_W + LANES:MIX_W + 2 * LANES]
    lane = lax.broadcasted_iota(jnp.int32, (1, LANES), 1)
    dt = _softplus(proj(AB_DT, LANES) + dtb_ref[...])
    a_neg = jnp.where(lane < SSM_HEADS, -jnp.exp(alog_ref[...]), 0.0)
    adt = dt * a_neg
    expand = exp_ref[...]
    dt_e = _sel_dot_lhs(dt, expand)
    adt_e = _sel_dot_lhs(adt, expand)
    xdt = xs * dt_e
    tri = tri_ref[...]
    l_len = SSM_CHUNK
    row = lax.broadcasted_iota(jnp.int32, (l_len, l_len), 0)
    col = lax.broadcasted_iota(jnp.int32, (l_len, l_len), 1)
    causal = row >= col
    first_half = lax.broadcasted_iota(jnp.int32, (1, LANES), 1) < SSM_HEAD_DIM
    grp_mask = [first_half, jnp.logical_not(first_half)]
    srow = lax.broadcasted_iota(jnp.int32, (LANES, MIX_W), 0)
    scol = lax.broadcasted_iota(jnp.int32, (LANES, MIX_W), 1)
    state_mask = (srow // SSM_STATE) == (scol // (MIX_W // SSM_GROUPS))
    ssm_rows = []
    for c in range(tm // l_len):
        rs = slice(c * l_len, (c + 1) * l_len)
        acs = _sel_dot_rhs(tri, adt[rs])
        acs_e = _sel_dot_rhs(tri, adt_e[rs])
        acs_t = acs.T
        last_e = acs_e[l_len - 1:l_len, :]
        bm_c = bm[rs]
        cm_c = cm[rs]
        s_prev = sssm_sc[...]
        y_off = _bdot(cm_c, s_prev) * jnp.exp(acs_e)
        cb = [_bdot_nt(jnp.where(grp_mask[g], cm_c, 0.0), bm_c) for g in range(SSM_GROUPS)]
        tiles = []
        for j in range(MIX_W // LANES):
            xt = xdt[rs, j * LANES:(j + 1) * LANES]
            outs = []
            for hh in (2 * j, 2 * j + 1):
                seg = acs[:, hh:hh + 1] - acs_t[hh:hh + 1, :]
                dec = jnp.where(causal, jnp.exp(jnp.where(causal, seg, 0.0)), 0.0)
                outs.append(_bdot(cb[hh // (SSM_HEADS // SSM_GROUPS)] * dec, xt))
            tiles.append(jnp.where(first_half, outs[0], outs[1]))
        y_diag = jnp.concatenate(tiles, axis=1)
        upd = _bdot_tn(bm_c, xdt[rs] * jnp.exp(last_e - acs_e))
        sssm_sc[...] = s_prev * jnp.exp(last_e) + jnp.where(state_mask, upd, 0.0)
        y = y_diag + y_off + dexp_ref[...] * xs[rs]
        y = y * _silu(z[rs])
        halves = []
        for g in range(SSM_GROUPS):
            yg = y[:, g * (MIX_W // SSM_GROUPS):(g + 1) * (MIX_W // SSM_GROUPS)]
            halves.append(yg * lax.rsqrt(jnp.mean(yg * yg, axis=-1, keepdims=True) + SSM_NORM_EPS))
        ssm_rows.append(jnp.concatenate(halves, axis=1) * sng_ref[...])
    y_ssm = jnp.concatenate(ssm_rows, axis=0)

    o_ref[...] = x + (_bdot(y_ret, wout_ref[0:MIX_W, :]) + _bdot(y_ssm, wout_ref[MIX_W:2 * MIX_W, :]))


def _const_spec(shape):
    nd = len(shape)
    return pl.BlockSpec(shape, lambda i, _nd=nd: (0,) * _nd)


def _mix_ab(x, norm_g, w_in, conv_w, conv_b, dt_bias, a_log, d_skip, norm_ssm, w_out):
    t_len, d = x.shape
    tm = min(MIX_TM, t_len)

    inv_freq = ROPE_BASE ** (-jnp.arange(0, RET_DK, 2, dtype=F32) / RET_DK)
    ang = jnp.arange(t_len, dtype=F32)[:, None] * inv_freq[None, :]
    cos2 = jnp.concatenate([jnp.cos(ang), jnp.cos(ang)], axis=1)
    sin2 = jnp.concatenate([-jnp.sin(ang), jnp.sin(ang)], axis=1)

    c = RET_CHUNK
    log_g = jnp.log1p(-(2.0 ** (-5.0 - jnp.arange(RET_HEADS, dtype=F32))))
    pos = jnp.arange(c, dtype=F32)
    rel = pos[:, None] - pos[None, :]
    rdec = jnp.where(rel >= 0, jnp.exp(jnp.maximum(rel, 0.0)[None] * log_g[:, None, None]), 0.0)
    per_head = lambda t: jnp.repeat(t, RET_DK, axis=1)
    rkte = per_head(jnp.exp((c - 1 - pos)[:, None] * log_g[None, :]))
    rqfs = per_head(jnp.exp((pos + 1.0)[:, None] * log_g[None, :]))
    rcd = per_head(jnp.exp(c * log_g)[None, :])

    w_pad = jnp.pad(w_in, ((0, 0), (0, AB_COLS - w_in.shape[1]))).astype(BF16)
    pad_row = lambda v: jnp.pad(v.astype(F32), (0, LANES - v.shape[0])).reshape(1, LANES)
    expand = (jnp.arange(LANES)[:, None] == (jnp.arange(MIX_W)[None, :] // SSM_HEAD_DIM)).astype(BF16)
    tri = (jnp.arange(SSM_CHUNK)[:, None] >= jnp.arange(SSM_CHUNK)[None, :]).astype(BF16)

    operands = [
        (x, pl.BlockSpec((tm, d), lambda i: (i, 0))),
        (cos2, pl.BlockSpec((tm, RET_DK), lambda i: (i, 0))),
        (sin2, pl.BlockSpec((tm, RET_DK), lambda i: (i, 0))),
        (norm_g.reshape(1, d), None),
        (w_pad, None),
        (rdec, None), (rqfs, None), (rkte, None), (rcd, None),
        (conv_w, None), (conv_b.reshape(1, SSM_XBC), None),
        (pad_row(dt_bias), None), (pad_row(a_log), None),
        (jnp.repeat(d_skip.astype(F32), SSM_HEAD_DIM).reshape(1, MIX_W), None),
        (norm_ssm.reshape(1, MIX_W), None),
        (expand, None), (tri, None),
        (w_out.astype(BF16), None),
    ]
    args = [a for a, _ in operands]
    specs = [s if s is not None else _const_spec(a.shape) for a, s in operands]
    return pl.pallas_call(
        functools.partial(_ab_body, tm=tm),
        grid=(t_len // tm,),
        in_specs=specs,
        out_specs=pl.BlockSpec((tm, d), lambda i: (i, 0)),
        out_shape=jax.ShapeDtypeStruct((t_len, d), F32),
        scratch_shapes=[
            pltpu.VMEM((RET_HEADS, RET_DK, RET_DK), F32),
            pltpu.VMEM((LANES, MIX_W), F32),
            pltpu.VMEM((tm + SUBLANES, SSM_XBC), F32),
        ],
        compiler_params=pltpu.CompilerParams(
            dimension_semantics=("arbitrary",), vmem_limit_bytes=VMEM_LIMIT),
        name="mix_ret_ssd",
    )(*args)


def _head_sum(x, ones_blk):
    tiles = [_sel_dot_lhs(x[:, j * LANES:(j + 1) * LANES], ones_blk) for j in range(x.shape[1] // LANES)]
    return jnp.concatenate(tiles, axis=1)


def _cd_body(x_ref, g_ref, win_ref, lcw_ref, lcb_ref, wa_ref, ba_ref, wx_ref, bx_ref, lam_ref,
             mu_ref, w0_ref, w2_ref, a0_ref, a2_ref, g2_ref, kk_ref, ka_ref, rk_ref, lng_ref, lnb_ref,
             ones_ref, tri_ref, wout_ref, o_ref,
             ext_sc, hcar_sc, pcar_sc, s_sc, lw_sc, r_sc, k_sc, v_sc, kn_sc, b_sc, y_sc, *, tm):
    @pl.when(pl.program_id(0) == 0)
    def _():
        ext_sc[pl.ds(0, SUBLANES), :] = jnp.zeros((SUBLANES, MIX_W), F32)
        hcar_sc[...] = jnp.zeros_like(hcar_sc)
        pcar_sc[...] = jnp.zeros_like(pcar_sc)
        s_sc[...] = jnp.zeros_like(s_sc)

    x = x_ref[...]
    h = _rms_norm(x, g_ref[...], NORM_EPS).astype(BF16)

    def proj(lo, width):
        return jnp.dot(h, win_ref[:, lo:lo + width], preferred_element_type=F32)

    rows = lax.broadcasted_iota(jnp.int32, (tm, 1), 0)

    xc = _causal_conv(ext_sc, proj(CD_XB, MIX_W), lcw_ref, lcb_ref, tm, LRU_CONV)
    r_parts, i_parts = [], []
    for n in range(LRU_BLOCKS):
        ls = slice(n * LRU_BLOCK, (n + 1) * LRU_BLOCK)
        r_parts.append(jax.nn.sigmoid(_bdot(xc[:, ls], wa_ref[n]) + ba_ref[:, ls]))
        i_parts.append(jax.nn.sigmoid(_bdot(xc[:, ls], wx_ref[n]) + bx_ref[:, ls]))
    r_gate = jnp.concatenate(r_parts, axis=1)
    i_gate = jnp.concatenate(i_parts, axis=1)
    log_a = -LRU_C * r_gate * _softplus(-lam_ref[...])
    a_cum = jnp.exp(log_a)
    th = jnp.tanh(log_a)
    hs = jnp.sqrt(-2.0 * th / (1.0 - th)) * (i_gate * xc)
    shift = 1
    while shift < tm:
        valid = rows >= shift
        a_prev = jnp.where(valid, pltpu.roll(a_cum, shift, axis=0), 1.0)
        h_prev = jnp.where(valid, pltpu.roll(hs, shift, axis=0), 0.0)
        hs = hs + a_cum * h_prev
        a_cum = a_cum * a_prev
        shift *= 2
    hs = hs + a_cum * hcar_sc[...]
    hcar_sc[...] = hs[tm - 1:tm, :]
    y_lru = hs * _gelu_tanh(proj(CD_GB, MIX_W))

    pr = proj(CD_PR, CD_PR_COLS)
    prev = jnp.where(rows == 0, pcar_sc[...], pltpu.roll(pr, 1, axis=0))
    pcar_sc[...] = pr[tm - 1:tm, :]
    ps = pr + (prev - pr) * mu_ref[...]
    r = ps[:, 0:MIX_W]
    k = ps[:, MIX_W:2 * MIX_W]
    v = ps[:, 2 * MIX_W:3 * MIX_W]
    wl = ps[:, 3 * MIX_W:3 * MIX_W + LANES]
    al = ps[:, 3 * MIX_W + LANES:3 * MIX_W + 2 * LANES]
    gl = ps[:, 3 * MIX_W + 2 * LANES:3 * MIX_W + 4 * LANES]
    w = -_softplus(-(w0_ref[...] + _bdot(jnp.tanh(wl), w2_ref[...]))) - 0.5
    log_w = -jnp.exp(w)
    a = jax.nn.sigmoid(a0_ref[...] + _bdot(al, a2_ref[...]))
    g = _bdot(jax.nn.sigmoid(gl), g2_ref[...])
    ones_blk = ones_ref[...]
    kk = k * kk_ref[...]
    kk = kk / jnp.maximum(jnp.sqrt(_head_sum(kk * kk, ones_blk)), 1e-12)
    k = k * (1.0 + (a - 1.0) * ka_ref[...])
    lw_sc[...] = log_w
    r_sc[...] = r
    k_sc[...] = k
    v_sc[...] = v
    kn_sc[...] = kk
    b_sc[...] = kk * a

    c_len = RWKV_CHUNK
    pair = 2 * c_len
    tri = tri_ref[...]
    lane = lax.broadcasted_iota(jnp.int32, (1, LANES), 1)
    m0 = lane < RWKV_HEAD_DIM
    prow = lax.broadcasted_iota(jnp.int32, (pair, pair), 0)
    pcol = lax.broadcasted_iota(jnp.int32, (pair, pair), 1)
    strict = prow > pcol
    incl = prow >= pcol
    eye = (prow == pcol).astype(F32)

    def stack(t):
        return jnp.concatenate([jnp.where(m0, t, 0.0), jnp.where(m0, 0.0, t)], axis=0).astype(BF16)

    def chunk_step(ci, carry):
        rs = pl.ds(pl.multiple_of(ci * c_len, c_len), c_len)
        lw_c = lw_sc[rs, :]
        cs = _sel_dot_rhs(tri, lw_c)
        last = cs[c_len - 1:c_len, :]
        e_in = jnp.exp(cs)
        e_ex = jnp.exp(cs - lw_c)
        e_neg = jnp.exp(-cs)
        e_end = jnp.exp(last - cs)
        w_end = jnp.exp(last)
        r_c = r_sc[rs, :]
        k_c = k_sc[rs, :]
        nb_c = -b_sc[rs, :]
        bt = kn_sc[rs, :] * e_ex
        rt = r_c * e_in
        ab = nb_c * e_neg
        kb = k_c * e_neg
        ae = nb_c * e_end
        ke = k_c * e_end
        v_c = v_sc[rs, :]
        for j in range(MIX_W // LANES):
            ls = slice(j * LANES, (j + 1) * LANES)
            bst, rst, abst, kbst = stack(bt[:, ls]), stack(rt[:, ls]), stack(ab[:, ls]), stack(kb[:, ls])
            aest, kest, vst = stack(ae[:, ls]), stack(ke[:, ls]), stack(v_c[:, ls])
            n_mat = jnp.where(strict, _bdot_nt(bst, abst), 0.0)
            a_bk = jnp.where(strict, _bdot_nt(bst, kbst), 0.0)
            a_ra = jnp.where(incl, _bdot_nt(rst, abst), 0.0)
            a_rk = jnp.where(incl, _bdot_nt(rst, kbst), 0.0)
            t_inv = eye + n_mat
            pw = n_mat
            for _ in range(int(math.log2(c_len)) - 1):
                pw = _dot_hi(pw, pw)
                t_inv = t_inv + _dot_hi(t_inv, pw)
            s_prev = s_sc[j]
            u = _bdot(t_inv, _bdot_nt(bst, s_prev) + _bdot(a_bk, vst))
            y2 = _bdot_nt(rst, s_prev) + _bdot(a_ra, u) + _bdot(a_rk, vst)
            y_sc[rs, ls] = y2[0:c_len, :] + y2[c_len:pair, :]
            s_sc[j] = s_prev * w_end[:, ls] + _bdot_tn(u, aest) + _bdot_tn(vst, kest)
        return carry

    lax.fori_loop(0, tm // c_len, chunk_step, 0)

    y = y_sc[...]
    inv_n = 1.0 / RWKV_HEAD_DIM
    mean = _head_sum(y, ones_blk) * inv_n
    yc = y - mean
    var = _head_sum(yc * yc, ones_blk) * inv_n
    y = yc * lax.rsqrt(var + RWKV_LN_EPS) * lng_ref[...] + lnb_ref[...]
    y = y + _head_sum(r * k * rk_ref[...], ones_blk) * v
    y_rwkv = y * g

    o_ref[...] = x + (_bdot(y_lru, wout_ref[0:MIX_W, :]) + _bdot(y_rwkv, wout_ref[MIX_W:2 * MIX_W, :]))


def _mix_cd(x, norm_g, w_in, lru_conv_w, lru_conv_b, lru_wa, lru_ba, lru_wx, lru_bx, lru_lambda,
            mu, w0, w2, a0, a2, g2, k_k, k_a, r_k, ln_g, ln_b, w_out):
    t_len, d = x.shape
    tm = min(MIX_TM, t_len)
    c3 = 3 * MIX_W

    def pad_lora(t, lo, width, padded):
        return jnp.pad(t[..., lo:lo + width], [(0, 0)] * (t.ndim - 1) + [(0, padded - width)])

    def pad_cols(t, base):
        return jnp.concatenate([
            t[..., :base + c3],
            pad_lora(t, base + c3, DECAY_LORA, LANES),
            pad_lora(t, base + c3 + DECAY_LORA, ICL_LORA, LANES),
            pad_lora(t, base + c3 + DECAY_LORA + ICL_LORA, GATE_LORA, 2 * LANES)], axis=-1)

    w_pad = pad_cols(w_in, 2 * MIX_W).astype(BF16)
    mu_pad = pad_cols(mu.reshape(1, -1), 0)
    pad_rows = lambda t, n: jnp.pad(t, ((0, n - t.shape[0]), (0, 0))).astype(BF16)
    row = lambda t: t.reshape(1, -1).astype(F32)
    ones_blk = ((jnp.arange(LANES)[:, None] // RWKV_HEAD_DIM) ==
                (jnp.arange(LANES)[None, :] // RWKV_HEAD_DIM)).astype(BF16)
    tri = (jnp.arange(RWKV_CHUNK)[:, None] >= jnp.arange(RWKV_CHUNK)[None, :]).astype(BF16)

    operands = [
        (x, pl.BlockSpec((tm, d), lambda i: (i, 0))),
        (row(norm_g), None),
        (w_pad, None),
        (lru_conv_w, None), (row(lru_conv_b), None),
        (lru_wa.astype(BF16), None), (row(lru_ba), None),
        (lru_wx.astype(BF16), None), (row(lru_bx), None),
        (row(lru_lambda), None),
        (mu_pad, None),
        (row(w0), None), (pad_rows(w2, LANES), None),
        (row(a0), None), (pad_rows(a2, LANES), None),
        (pad_rows(g2, 2 * LANES), None),
        (row(k_k), None), (row(k_a), None), (row(r_k), None), (row(ln_g), None), (row(ln_b), None),
        (ones_blk, None), (tri, None),
        (w_out.astype(BF16), None),
    ]
    args = [a for a, _ in operands]
    specs = [s if s is not None else _const_spec(a.shape) for a, s in operands]
    wide = lambda: pltpu.VMEM((tm, MIX_W), F32)
    return pl.pallas_call(
        functools.partial(_cd_body, tm=tm),
        grid=(t_len // tm,),
        in_specs=specs,
        out_specs=pl.BlockSpec((tm, d), lambda i: (i, 0)),
        out_shape=jax.ShapeDtypeStruct((t_len, d), F32),
        scratch_shapes=[
            pltpu.VMEM((tm + SUBLANES, MIX_W), F32),
            pltpu.VMEM((1, MIX_W), F32),
            pltpu.VMEM((1, CD_PR_COLS), F32),
            pltpu.VMEM((MIX_W // LANES, LANES, LANES), F32),
            wide(), wide(), wide(), wide(), wide(), wide(), wide(),
        ],
        compiler_params=pltpu.CompilerParams(
            dimension_semantics=("arbitrary",), vmem_limit_bytes=VMEM_LIMIT),
        name="mix_lru_rwkv",
    )(*args)


def kernel(x, ffn1_norm, ffn1_wg, ffn1_wu, ffn1_wd, mix_norm, ffn2_norm, ffn2_wg, ffn2_wu, ffn2_wd,
           ab_w_in, ab_w_out, ssm_conv_w, ssm_conv_b, ssm_dt_bias, ssm_a_log, ssm_d, ssm_norm,
           cd_w_in, cd_w_out, lru_conv_w, lru_conv_b, lru_wa, lru_ba, lru_wx, lru_bx, lru_lambda,
           rwkv_mu, rwkv_w0, rwkv_w2, rwkv_a0, rwkv_a2, rwkv_g2, rwkv_kk, rwkv_ka, rwkv_rk,
           rwkv_ln_g, rwkv_ln_b, final_norm):
    bsz, t_len, d = x.shape
    depth = ffn1_norm.shape[0]
    outs = []
    for b in range(bsz):
        xb = x[b]
        for layer in range(depth):
            j = layer // 2
            xb = _ffn(xb, ffn1_norm[layer], ffn1_wg[layer].astype(BF16), ffn1_wu[layer].astype(BF16),
                      ffn1_wd[layer].astype(BF16), final_norm, final_norm=False)
            if layer % 2 == 0:
                xb = _mix_ab(xb, mix_norm[layer], ab_w_in[j], ssm_conv_w[j], ssm_conv_b[j], ssm_dt_bias[j],
                             ssm_a_log[j], ssm_d[j], ssm_norm[j], ab_w_out[j])
            else:
                xb = _mix_cd(xb, mix_norm[layer], cd_w_in[j], lru_conv_w[j], lru_conv_b[j], lru_wa[j],
                             lru_ba[j], lru_wx[j], lru_bx[j], lru_lambda[j], rwkv_mu[j], rwkv_w0[j],
                             rwkv_w2[j], rwkv_a0[j], rwkv_a2[j], rwkv_g2[j], rwkv_kk[j], rwkv_ka[j],
                             rwkv_rk[j].reshape(-1), rwkv_ln_g[j], rwkv_ln_b[j], cd_w_out[j])
            xb = _ffn(xb, ffn2_norm[layer], ffn2_wg[layer].astype(BF16), ffn2_wu[layer].astype(BF16),
                      ffn2_wd[layer].astype(BF16), final_norm, final_norm=(layer == depth - 1))
        outs.append(xb)
    return jnp.stack(outs, axis=0)
```

```python
import functools
import math

import jax
import jax.numpy as jnp
from jax import lax
from jax.experimental import pallas as pl
from jax.experimental.pallas import tpu as pltpu

F32 = jnp.float32
BF16 = jnp.bfloat16

D_MODEL = 1024
D_FF = 2816
NORM_EPS = 1e-6
MIX_W = 512

RET_HEADS = 4
RET_DK = 128
RET_CHUNK = 128
ROPE_BASE = 10000.0
HEAD_NORM_EPS = 1e-6

SSM_HEADS = 8
SSM_HEAD_DIM = 64
SSM_GROUPS = 2
SSM_STATE = 64
SSM_CONV = 4
SSM_CHUNK = 128
SSM_XBC = 768
SSM_NORM_EPS = 1e-5

LRU_BLOCKS = 4
LRU_BLOCK = 128
LRU_CONV = 4
LRU_C = 8.0

RWKV_HEADS = 8
RWKV_HEAD_DIM = 64
DECAY_LORA = 64
ICL_LORA = 64
GATE_LORA = 160
RWKV_LN_EPS = 64e-5
RWKV_CHUNK = 64

LANES = 128
SUBLANES = 8
VMEM_LIMIT = 56 * 1024 * 1024

FFN_TM = 1024
FFN_TF = 256
MIX_TM = 256

AB_Q, AB_K, AB_V, AB_G, AB_Z, AB_XBC, AB_DT = 0, 512, 1024, 1536, 2048, 2560, 3328
AB_COLS = 3456
CD_XB, CD_GB, CD_PR = 0, 512, 1024
CD_PR_COLS = 2048
CD_COLS = CD_PR + CD_PR_COLS


def _bdot(a, b):
    return jnp.dot(a.astype(BF16), b.astype(BF16), preferred_element_type=F32)


def _bdot_nt(a, b):
    return lax.dot_general(a.astype(BF16), b.astype(BF16), (((1,), (1,)), ((), ())),
                           preferred_element_type=F32)


def _bdot_tn(a, b):
    return lax.dot_general(a.astype(BF16), b.astype(BF16), (((0,), (0,)), ((), ())),
                           preferred_element_type=F32)


def _split3(x):
    hi = x.astype(BF16)
    r1 = x - hi.astype(F32)
    mid = r1.astype(BF16)
    lo = (r1 - mid.astype(F32)).astype(BF16)
    return hi, mid, lo


def _sel_dot_lhs(x, sel):
    hi, mid, lo = _split3(x)
    d = lambda p: jnp.dot(p, sel, preferred_element_type=F32)
    return d(hi) + (d(mid) + d(lo))


def _sel_dot_rhs(sel, x):
    hi, mid, lo = _split3(x)
    d = lambda p: jnp.dot(sel, p, preferred_element_type=F32)
    return d(hi) + (d(mid) + d(lo))


def _rms_norm(x, g, eps):
    return x * lax.rsqrt(jnp.mean(x * x, axis=-1, keepdims=True) + eps) * g


def _silu(x):
    return x * jax.nn.sigmoid(x)


def _softplus(x):
    return jnp.maximum(x, 0.0) + jnp.log1p(jnp.exp(-jnp.abs(x)))


def _gelu_tanh(x):
    c = math.sqrt(2.0 / math.pi)
    return 0.5 * x * (1.0 + jnp.tanh(c * (x + 0.044715 * (x * x * x))))


def _causal_conv(ext_ref, x, w_ref, b_ref, tm, width):
    ext_ref[pl.ds(SUBLANES, tm), :] = x
    acc = b_ref[...] + w_ref[width - 1:width, :] * x
    for k in range(width - 1):
        acc = acc + w_ref[k:k + 1, :] * ext_ref[pl.ds(SUBLANES - (width - 1) + k, tm), :]
    ext_ref[pl.ds(0, SUBLANES), :] = ext_ref[pl.ds(tm, SUBLANES), :]
    return acc


def _ffn_body(x_ref, g_ref, wg_ref, wu_ref, wd_ref, fin_ref, o_ref, h_sc, acc_sc, *, final_norm):
    j = pl.program_id(1)

    @pl.when(j == 0)
    def _():
        h_sc[...] = _rms_norm(x_ref[...], g_ref[...], NORM_EPS).astype(BF16)
        acc_sc[...] = jnp.zeros_like(acc_sc)

    h = h_sc[...]
    gate = jnp.dot(h, wg_ref[...], preferred_element_type=F32)
    up = jnp.dot(h, wu_ref[...], preferred_element_type=F32)
    act = (_silu(gate) * up).astype(BF16)
    acc_sc[...] += jnp.dot(act, wd_ref[...], preferred_element_type=F32)

    @pl.when(j == pl.num_programs(1) - 1)
    def _():
        y = x_ref[...] + 0.5 * acc_sc[...]
        if final_norm:
            y = _rms_norm(y, fin_ref[...], NORM_EPS)
        o_ref[...] = y


def _ffn(x, norm_g, wg, wu, wd, fin_g, *, final_norm):
    t_len, d = x.shape
    tm = min(FFN_TM, t_len)
    tf = FFN_TF
    grid = (t_len // tm, D_FF // tf)
    return pl.pallas_call(
        functools.partial(_ffn_body, final_norm=final_norm),
        grid=grid,
        in_specs=[
            pl.BlockSpec((tm, d), lambda i, j: (i, 0)),
            pl.BlockSpec((1, d), lambda i, j: (0, 0)),
            pl.BlockSpec((d, tf), lambda i, j: (0, j)),
            pl.BlockSpec((d, tf), lambda i, j: (0, j)),
            pl.BlockSpec((tf, d), lambda i, j: (j, 0)),
            pl.BlockSpec((1, d), lambda i, j: (0, 0)),
        ],
        out_specs=pl.BlockSpec((tm, d), lambda i, j: (i, 0)),
        out_shape=jax.ShapeDtypeStruct((t_len, d), F32),
        scratch_shapes=[pltpu.VMEM((tm, d), BF16), pltpu.VMEM((tm, d), F32)],
        compiler_params=pltpu.CompilerParams(
            dimension_semantics=("arbitrary", "arbitrary"), vmem_limit_bytes=VMEM_LIMIT),
        name="ffn_final" if final_norm else "ffn",
    )(x, norm_g.reshape(1, d), wg, wu, wd, fin_g.reshape(1, d))


def _ab_body(x_ref, cos_ref, sin_ref, g_ref, win_ref, rdec_ref, rqfs_ref, rkte_ref, rcd_ref,
             cw_ref, cb_ref, dtb_ref, alog_ref, dexp_ref, sng_ref, exp_ref, tri_ref, wout_ref,
             o_ref, sret_sc, sssm_sc, ext_sc, *, tm):
    @pl.when(pl.program_id(0) == 0)
    def _():
        sret_sc[...] = jnp.zeros_like(sret_sc)
        sssm_sc[...] = jnp.zeros_like(sssm_sc)
        ext_sc[pl.ds(0, SUBLANES), :] = jnp.zeros((SUBLANES, SSM_XBC), F32)

    x = x_ref[...]
    h = _rms_norm(x, g_ref[...], NORM_EPS).astype(BF16)

    def proj(lo, width):
        return jnp.dot(h, win_ref[:, lo:lo + width], preferred_element_type=F32)

    cos2 = cos_ref[...]
    sin2 = sin_ref[...]

    def rotary(t):
        return t * cos2 + pltpu.roll(t, RET_DK // 2, axis=1) * sin2

    q_all = proj(AB_Q, MIX_W)
    k_all = proj(AB_K, MIX_W)
    v_all = proj(AB_V, MIX_W)
    gate_ret = _silu(proj(AB_G, MIX_W))
    scale = RET_DK ** -0.5
    c_len = RET_CHUNK
    ret_rows = []
    for c in range(tm // c_len):
        rs = slice(c * c_len, (c + 1) * c_len)
        heads = []
        for hh in range(RET_HEADS):
            ls = slice(hh * RET_DK, (hh + 1) * RET_DK)
            qh = q_all[rs, ls] * cos2[rs] + pltpu.roll(q_all[rs, ls], RET_DK // 2, axis=1) * sin2[rs]
            kh = (k_all[rs, ls] * cos2[rs] + pltpu.roll(k_all[rs, ls], RET_DK // 2, axis=1) * sin2[rs]) * scale
            vh = v_all[rs, ls]
            s_prev = sret_sc[hh]
            scores = _bdot_nt(qh, kh) * rdec_ref[hh]
            o = _bdot(scores, vh) + _bdot(qh * rqfs_ref[:, ls], s_prev)
            sret_sc[hh] = s_prev * rcd_ref[:, ls] + _bdot_tn(kh * rkte_ref[:, ls], vh)
            o = o * lax.rsqrt(jnp.mean(o * o, axis=-1, keepdims=True) + HEAD_NORM_EPS)
            heads.append(o)
        ret_rows.append(jnp.concatenate(heads, axis=1))
    y_ret = gate_ret * jnp.concatenate(ret_rows, axis=0)

    z = proj(AB_Z, MIX_W)
    xbc = _silu(_causal_conv(ext_sc, proj(AB_XBC, SSM_XBC), cw_ref, cb_ref, tm, SSM_CONV))
    xs = xbc[:, :MIX_W]
    bm = xbc[:, MIX_W:MIX_W + LANES]
    cm = xbc[:, MIX_W + LANES:MIX_W + 2 * LANES]
    lane = lax.broadcasted_iota(jnp.int32, (1, LANES), 1)
    dt = _softplus(proj(AB_DT, LANES) + dtb_ref[...])
    a_neg = jnp.where(lane < SSM_HEADS, -jnp.exp(alog_ref[...]), 0.0)
    adt = dt * a_neg
    expand = exp_ref[...]
    dt_e = _sel_dot_lhs(dt, expand)
    adt_e = _sel_dot_lhs(adt, expand)
    xdt = xs * dt_e
    tri = tri_ref[...]
    l_len = SSM_CHUNK
    row = lax.broadcasted_iota(jnp.int32, (l_len, l_len), 0)
    col = lax.broadcasted_iota(jnp.int32, (l_len, l_len), 1)
    causal = row >= col
    first_half = lax.broadcasted_iota(jnp.int32, (1, LANES), 1) < SSM_HEAD_DIM
    grp_mask = [first_half, jnp.logical_not(first_half)]
    srow = lax.broadcasted_iota(jnp.int32, (LANES, MIX_W), 0)
    scol = lax.broadcasted_iota(jnp.int32, (LANES, MIX_W), 1)
    state_mask = (srow // SSM_STATE) == (scol // (MIX_W // SSM_GROUPS))
    ssm_rows = []
    for c in range(tm // l_len):
        rs = slice(c * l_len, (c + 1) * l_len)
        acs = _sel_dot_rhs(tri, adt[rs])
        acs_e = _sel_dot_rhs(tri, adt_e[rs])
        acs_t = acs.T
        last_e = acs_e[l_len - 1:l_len, :]
        bm_c = bm[rs]
        cm_c = cm[rs]
        s_prev = sssm_sc[...]
        y_off = _bdot(cm_c, s_prev) * jnp.exp(acs_e)
        cb = [_bdot_nt(jnp.where(grp_mask[g], cm_c, 0.0), bm_c) for g in range(SSM_GROUPS)]
        tiles = []
        for j in range(MIX_W // LANES):
            xt = xdt[rs, j * LANES:(j + 1) * LANES]
            outs = []
            for hh in (2 * j, 2 * j + 1):
                seg = acs[:, hh:hh + 1] - acs_t[hh:hh + 1, :]
                dec = jnp.where(causal, jnp.exp(jnp.where(causal, seg, 0.0)), 0.0)
                outs.append(_bdot(cb[hh // (SSM_HEADS // SSM_GROUPS)] * dec, xt))
            tiles.append(jnp.where(first_half, outs[0], outs[1]))
        y_diag = jnp.concatenate(tiles, axis=1)
        upd = _bdot_tn(bm_c, xdt[rs] * jnp.exp(last_e - acs_e))
        sssm_sc[...] = s_prev * jnp.exp(last_e) + jnp.where(state_mask, upd, 0.0)
        y = y_diag + y_off + dexp_ref[...] * xs[rs]
        y = y * _silu(z[rs])
        halves = []
        for g in range(SSM_GROUPS):
            yg = y[:, g * (MIX_W // SSM_GROUPS):(g + 1) * (MIX_W // SSM_GROUPS)]
            halves.append(yg * lax.rsqrt(jnp.mean(yg * yg, axis=-1, keepdims=True) + SSM_NORM_EPS))
        ssm_rows.append(jnp.concatenate(halves, axis=1) * sng_ref[...])
    y_ssm = jnp.concatenate(ssm_rows, axis=0)

    o_ref[...] = x + (_bdot(y_ret, wout_ref[0:MIX_W, :]) + _bdot(y_ssm, wout_ref[MIX_W:2 * MIX_W, :]))


def _const_spec(shape):
    nd = len(shape)
    return pl.BlockSpec(shape, lambda i, _nd=nd: (0,) * _nd)


def _mix_ab(x, norm_g, w_in, conv_w, conv_b, dt_bias, a_log, d_skip, norm_ssm, w_out):
    t_len, d = x.shape
    tm = min(MIX_TM, t_len)

    inv_freq = ROPE_BASE ** (-jnp.arange(0, RET_DK, 2, dtype=F32) / RET_DK)
    ang = jnp.arange(t_len, dtype=F32)[:, None] * inv_freq[None, :]
    cos2 = jnp.concatenate([jnp.cos(ang), jnp.cos(ang)], axis=1)
    sin2 = jnp.concatenate([-jnp.sin(ang), jnp.sin(ang)], axis=1)

    c = RET_CHUNK
    log_g = jnp.log1p(-(2.0 ** (-5.0 - jnp.arange(RET_HEADS, dtype=F32))))
    pos = jnp.arange(c, dtype=F32)
    rel = pos[:, None] - pos[None, :]
    rdec = jnp.where(rel >= 0, jnp.exp(jnp.maximum(rel, 0.0)[None] * log_g[:, None, None]), 0.0)
    per_head = lambda t: jnp.repeat(t, RET_DK, axis=1)
    rkte = per_head(jnp.exp((c - 1 - pos)[:, None] * log_g[None, :]))
    rqfs = per_head(jnp.exp((pos + 1.0)[:, None] * log_g[None, :]))
    rcd = per_head(jnp.exp(c * log_g)[None, :])

    w_pad = jnp.pad(w_in, ((0, 0), (0, AB_COLS - w_in.shape[1]))).astype(BF16)
    pad_row = lambda v: jnp.pad(v.astype(F32), (0, LANES - v.shape[0])).reshape(1, LANES)
    expand = (jnp.arange(LANES)[:, None] == (jnp.arange(MIX_W)[None, :] // SSM_HEAD_DIM)).astype(BF16)
    tri = (jnp.arange(SSM_CHUNK)[:, None] >= jnp.arange(SSM_CHUNK)[None, :]).astype(BF16)

    operands = [
        (x, pl.BlockSpec((tm, d), lambda i: (i, 0))),
        (cos2, pl.BlockSpec((tm, RET_DK), lambda i: (i, 0))),
        (sin2, pl.BlockSpec((tm, RET_DK), lambda i: (i, 0))),
        (norm_g.reshape(1, d), None),
        (w_pad, None),
        (rdec, None), (rqfs, None), (rkte, None), (rcd, None),
        (conv_w, None), (conv_b.reshape(1, SSM_XBC), None),
        (pad_row(dt_bias), None), (pad_row(a_log), None),
        (jnp.repeat(d_skip.astype(F32), SSM_HEAD_DIM).reshape(1, MIX_W), None),
        (norm_ssm.reshape(1, MIX_W), None),
        (expand, None), (tri, None),
        (w_out.astype(BF16), None),
    ]
    args = [a for a, _ in operands]
    specs = [s if s is not None else _const_spec(a.shape) for a, s in operands]
    return pl.pallas_call(
        functools.partial(_ab_body, tm=tm),
        grid=(t_len // tm,),
        in_specs=specs,
        out_specs=pl.BlockSpec((tm, d), lambda i: (i, 0)),
        out_shape=jax.ShapeDtypeStruct((t_len, d), F32),
        scratch_shapes=[
            pltpu.VMEM((RET_HEADS, RET_DK, RET_DK), F32),
            pltpu.VMEM((LANES, MIX_W), F32),
            pltpu.VMEM((tm + SUBLANES, SSM_XBC), F32),
        ],
        compiler_params=pltpu.CompilerParams(
            dimension_semantics=("arbitrary",), vmem_limit_bytes=VMEM_LIMIT),
        name="mix_ret_ssd",
    )(*args)


def _head_sum(x, ones_blk):
    tiles = [_sel_dot_lhs(x[:, j * LANES:(j + 1) * LANES], ones_blk) for j in range(x.shape[1] // LANES)]
    return jnp.concatenate(tiles, axis=1)


def _cd_body(x_ref, g_ref, win_ref, lcw_ref, lcb_ref, wa_ref, ba_ref, wx_ref, bx_ref, lam_ref,
             mu_ref, w0_ref, w2_ref, a0_ref, a2_ref, g2_ref, kk_ref, ka_ref, rk_ref, lng_ref, lnb_ref,
             ones_ref, sel_ref, wout_ref, o_ref,
             ext_sc, hcar_sc, pcar_sc, s_sc, *, tm):
    @pl.when(pl.program_id(0) == 0)
    def _():
        ext_sc[pl.ds(0, SUBLANES), :] = jnp.zeros((SUBLANES, MIX_W), F32)
        hcar_sc[...] = jnp.zeros_like(hcar_sc)
        pcar_sc[...] = jnp.zeros_like(pcar_sc)
        s_sc[...] = jnp.zeros_like(s_sc)

    x = x_ref[...]
    h = _rms_norm(x, g_ref[...], NORM_EPS).astype(BF16)

    def proj(lo, width):
        return jnp.dot(h, win_ref[:, lo:lo + width], preferred_element_type=F32)

    rows = lax.broadcasted_iota(jnp.int32, (tm, 1), 0)

    xc = _causal_conv(ext_sc, proj(CD_XB, MIX_W), lcw_ref, lcb_ref, tm, LRU_CONV)
    r_parts, i_parts = [], []
    for n in range(LRU_BLOCKS):
        ls = slice(n * LRU_BLOCK, (n + 1) * LRU_BLOCK)
        r_parts.append(jax.nn.sigmoid(_bdot(xc[:, ls], wa_ref[n]) + ba_ref[:, ls]))
        i_parts.append(jax.nn.sigmoid(_bdot(xc[:, ls], wx_ref[n]) + bx_ref[:, ls]))
    r_gate = jnp.concatenate(r_parts, axis=1)
    i_gate = jnp.concatenate(i_parts, axis=1)
    log_a = -LRU_C * r_gate * _softplus(-lam_ref[...])
    a_cum = jnp.exp(log_a)
    th = jnp.tanh(log_a)
    hs = jnp.sqrt(-2.0 * th / (1.0 - th)) * (i_gate * xc)
    shift = 1
    while shift < tm:
        valid = rows >= shift
        a_prev = jnp.where(valid, pltpu.roll(a_cum, shift, axis=0), 1.0)
        h_prev = jnp.where(valid, pltpu.roll(hs, shift, axis=0), 0.0)
        hs = hs + a_cum * h_prev
        a_cum = a_cum * a_prev
        shift *= 2
    hs = hs + a_cum * hcar_sc[...]
    hcar_sc[...] = hs[tm - 1:tm, :]
    y_lru = hs * _gelu_tanh(proj(CD_GB, MIX_W))

    pr = proj(CD_PR, CD_PR_COLS)
    prev = jnp.where(rows == 0, pcar_sc[...], pltpu.roll(pr, 1, axis=0))
    pcar_sc[...] = pr[tm - 1:tm, :]
    ps = pr + (prev - pr) * mu_ref[...]
    r = ps[:, 0:MIX_W]
    k = ps[:, MIX_W:2 * MIX_W]
    v = ps[:, 2 * MIX_W:3 * MIX_W]
    wl = ps[:, 3 * MIX_W:3 * MIX_W + LANES]
    al = ps[:, 3 * MIX_W + LANES:3 * MIX_W + 2 * LANES]
    gl = ps[:, 3 * MIX_W + 2 * LANES:3 * MIX_W + 4 * LANES]
    w = -_softplus(-(w0_ref[...] + _bdot(jnp.tanh(wl), w2_ref[...]))) - 0.5
    log_w = -jnp.exp(w)
    a = jax.nn.sigmoid(a0_ref[...] + _bdot(al, a2_ref[...]))
    g = _bdot(jax.nn.sigmoid(gl), g2_ref[...])
    ones_blk = ones_ref[...]
    kk = k * kk_ref[...]
    kk = kk / jnp.maximum(jnp.sqrt(_head_sum(kk * kk, ones_blk)), 1e-12)
    k = k * (1.0 + (a - 1.0) * ka_ref[...])
    c_len = RWKV_CHUNK
    pair = 2 * c_len
    n_chunks = tm // c_len
    n_pairs = MIX_W // LANES

    res = jnp.dot(sel_ref[...], jnp.concatenate(_split3(log_w), axis=1), preferred_element_type=F32)
    res = res[:, 0:MIX_W] + (res[:, MIX_W:2 * MIX_W] + res[:, 2 * MIX_W:3 * MIX_W])
    cs = res[0:tm]
    tot = res[tm:2 * tm]
    e_neg = jnp.exp(-cs)
    e_end = jnp.exp(tot - cs)
    w_end = jnp.exp(tot)
    nb = -(kk * a)
    bt = kk * jnp.exp(cs - log_w)
    rt = r * jnp.exp(cs)
    ab = nb * e_neg
    kb = k * e_neg
    ae = nb * e_end
    ke = k * e_end

    lane = lax.broadcasted_iota(jnp.int32, (1, LANES), 1)
    m0 = lane < RWKV_HEAD_DIM
    prow = lax.broadcasted_iota(jnp.int32, (pair, pair), 0)
    pcol = lax.broadcasted_iota(jnp.int32, (pair, pair), 1)
    strict = prow > pcol
    incl = prow >= pcol
    eye = (prow == pcol).astype(F32)

    def merge_mask(s):
        return ((prow // s) == (pcol // s) + 1) & ((prow // s) % 2 == 1)

    def stack32(t, c, j):
        t = t[c * c_len:(c + 1) * c_len, j * LANES:(j + 1) * LANES]
        return jnp.concatenate([jnp.where(m0, t, 0.0), jnp.where(m0, 0.0, t)], axis=0)

    units = [(c, j) for c in range(n_chunks) for j in range(n_pairs)]
    bst = {u: stack32(bt, *u).astype(BF16) for u in units}
    rst = {u: stack32(rt, *u).astype(BF16) for u in units}
    vst = {u: stack32(v, *u).astype(BF16) for u in units}
    n_mat, lhs_x, lhs_y, lhs_z, w_col = {}, {}, {}, {}, {}
    for u in units:
        gram = _bdot_nt(jnp.concatenate([bst[u], rst[u]], axis=0),
                        jnp.concatenate([stack32(ab, *u), stack32(kb, *u)], axis=0))
        n_mat[u] = jnp.where(strict, gram[0:pair, 0:pair], 0.0)
        lhs_x[u] = jnp.concatenate(
            [bst[u], jnp.where(strict, gram[0:pair, pair:2 * pair], 0.0).astype(BF16)], axis=1)
        lhs_y[u] = jnp.concatenate(
            [rst[u], jnp.where(incl, gram[pair:2 * pair, 0:pair], 0.0).astype(BF16),
             jnp.where(incl, gram[pair:2 * pair, pair:2 * pair], 0.0).astype(BF16)], axis=1)
        lhs_z[u] = jnp.concatenate(
            [stack32(ae, *u).T.astype(BF16), stack32(ke, *u).T.astype(BF16)], axis=1)
        c, j = u
        w_row = w_end[c * c_len:(c + 1) * c_len, j * LANES:(j + 1) * LANES]
        w_col[u] = jnp.concatenate([w_row, w_row], axis=0).T

    t_inv = {u: eye + jnp.where(merge_mask(1), n_mat[u], 0.0) for u in units}
    s = 2
    while s < c_len:
        mask = merge_mask(s)
        half = {u: _bdot(t_inv[u], jnp.where(mask, n_mat[u], 0.0)) for u in units}
        t_inv = {u: t_inv[u] + _bdot(half[u], t_inv[u]) for u in units}
        s *= 2
    t_inv = {u: t_inv[u].astype(BF16) for u in units}

    y_rows = []
    for c in range(n_chunks):
        z_prev = [s_sc[j] for j in range(n_pairs)]
        zb = [z.astype(BF16) for z in z_prev]
        xs = [jnp.dot(lhs_x[(c, j)], jnp.concatenate([zb[j], vst[(c, j)]], axis=0),
                      preferred_element_type=F32) for j in range(n_pairs)]
        us = [jnp.dot(t_inv[(c, j)], xs[j].astype(BF16), preferred_element_type=F32).astype(BF16)
              for j in range(n_pairs)]
        tiles = []
        for j in range(n_pairs):
            y2 = jnp.dot(lhs_y[(c, j)], jnp.concatenate([zb[j], us[j], vst[(c, j)]], axis=0),
                         preferred_element_type=F32)
            tiles.append(y2[0:c_len, :] + y2[c_len:pair, :])
            s_sc[j] = z_prev[j] * w_col[(c, j)] + jnp.dot(
                lhs_z[(c, j)], jnp.concatenate([us[j], vst[(c, j)]], axis=0), preferred_element_type=F32)
        y_rows.append(jnp.concatenate(tiles, axis=1))
    y = jnp.concatenate(y_rows, axis=0)
    inv_n = 1.0 / RWKV_HEAD_DIM
    mean = _head_sum(y, ones_blk) * inv_n
    yc = y - mean
    var = _head_sum(yc * yc, ones_blk) * inv_n
    y = yc * lax.rsqrt(var + RWKV_LN_EPS) * lng_ref[...] + lnb_ref[...]
    y = y + _head_sum(r * k * rk_ref[...], ones_blk) * v
    y_rwkv = y * g

    o_ref[...] = x + (_bdot(y_lru, wout_ref[0:MIX_W, :]) + _bdot(y_rwkv, wout_ref[MIX_W:2 * MIX_W, :]))


def _mix_cd(x, norm_g, w_in, lru_conv_w, lru_conv_b, lru_wa, lru_ba, lru_wx, lru_bx, lru_lambda,
            mu, w0, w2, a0, a2, g2, k_k, k_a, r_k, ln_g, ln_b, w_out):
    t_len, d = x.shape
    tm = min(MIX_TM, t_len)
    c3 = 3 * MIX_W

    def pad_lora(t, lo, width, padded):
        return jnp.pad(t[..., lo:lo + width], [(0, 0)] * (t.ndim - 1) + [(0, padded - width)])

    def pad_cols(t, base):
        return jnp.concatenate([
            t[..., :base + c3],
            pad_lora(t, base + c3, DECAY_LORA, LANES),
            pad_lora(t, base + c3 + DECAY_LORA, ICL_LORA, LANES),
            pad_lora(t, base + c3 + DECAY_LORA + ICL_LORA, GATE_LORA, 2 * LANES)], axis=-1)

    w_pad = pad_cols(w_in, 2 * MIX_W).astype(BF16)
    mu_pad = pad_cols(mu.reshape(1, -1), 0)
    pad_rows = lambda t, n: jnp.pad(t, ((0, n - t.shape[0]), (0, 0))).astype(BF16)
    row = lambda t: t.reshape(1, -1).astype(F32)
    ones_blk = ((jnp.arange(LANES)[:, None] // RWKV_HEAD_DIM) ==
                (jnp.arange(LANES)[None, :] // RWKV_HEAD_DIM)).astype(BF16)
    t_idx = jnp.arange(tm)
    same_chunk = (t_idx[:, None] // RWKV_CHUNK) == (t_idx[None, :] // RWKV_CHUNK)
    sel = jnp.concatenate([same_chunk & (t_idx[:, None] >= t_idx[None, :]), same_chunk], axis=0).astype(BF16)

    operands = [
        (x, pl.BlockSpec((tm, d), lambda i: (i, 0))),
        (row(norm_g), None),
        (w_pad, None),
        (lru_conv_w, None), (row(lru_conv_b), None),
        (lru_wa.astype(BF16), None), (row(lru_ba), None),
        (lru_wx.astype(BF16), None), (row(lru_bx), None),
        (row(lru_lambda), None),
        (mu_pad, None),
        (row(w0), None), (pad_rows(w2, LANES), None),
        (row(a0), None), (pad_rows(a2, LANES), None),
        (pad_rows(g2, 2 * LANES), None),
        (row(k_k), None), (row(k_a), None), (row(r_k), None), (row(ln_g), None), (row(ln_b), None),
        (ones_blk, None), (sel, None),
        (w_out.astype(BF16), None),
    ]
    args = [a for a, _ in operands]
    specs = [s if s is not None else _const_spec(a.shape) for a, s in operands]
    return pl.pallas_call(
        functools.partial(_cd_body, tm=tm),
        grid=(t_len // tm,),
        in_specs=specs,
        out_specs=pl.BlockSpec((tm, d), lambda i: (i, 0)),
        out_shape=jax.ShapeDtypeStruct((t_len, d), F32),
        scratch_shapes=[
            pltpu.VMEM((tm + SUBLANES, MIX_W), F32),
            pltpu.VMEM((1, MIX_W), F32),
            pltpu.VMEM((1, CD_PR_COLS), F32),
            pltpu.VMEM((MIX_W // LANES, LANES, LANES), F32),
        ],
        compiler_params=pltpu.CompilerParams(
            dimension_semantics=("arbitrary",), vmem_limit_bytes=VMEM_LIMIT),
        name="mix_lru_rwkv",
    )(*args)


def kernel(x, ffn1_norm, ffn1_wg, ffn1_wu, ffn1_wd, mix_norm, ffn2_norm, ffn2_wg, ffn2_wu, ffn2_wd,
           ab_w_in, ab_w_out, ssm_conv_w, ssm_conv_b, ssm_dt_bias, ssm_a_log, ssm_d, ssm_norm,
           cd_w_in, cd_w_out, lru_conv_w, lru_conv_b, lru_wa, lru_ba, lru_wx, lru_bx, lru_lambda,
           rwkv_mu, rwkv_w0, rwkv_w2, rwkv_a0, rwkv_a2, rwkv_g2, rwkv_kk, rwkv_ka, rwkv_rk,
           rwkv_ln_g, rwkv_ln_b, final_norm):
    bsz, t_len, d = x.shape
    depth = ffn1_norm.shape[0]
    outs = []
    for b in range(bsz):
        xb = x[b]
        for layer in range(depth):
            j = layer // 2
            xb = _ffn(xb, ffn1_norm[layer], ffn1_wg[layer].astype(BF16), ffn1_wu[layer].astype(BF16),
                      ffn1_wd[layer].astype(BF16), final_norm, final_norm=False)
            if layer % 2 == 0:
                xb = _mix_ab(xb, mix_norm[layer], ab_w_in[j], ssm_conv_w[j], ssm_conv_b[j], ssm_dt_bias[j],
                             ssm_a_log[j], ssm_d[j], ssm_norm[j], ab_w_out[j])
            else:
                xb = _mix_cd(xb, mix_norm[layer], cd_w_in[j], lru_conv_w[j], lru_conv_b[j], lru_wa[j],
                             lru_ba[j], lru_wx[j], lru_bx[j], lru_lambda[j], rwkv_mu[j], rwkv_w0[j],
                             rwkv_w2[j], rwkv_a0[j], rwkv_a2[j], rwkv_g2[j], rwkv_kk[j], rwkv_ka[j],
                             rwkv_rk[j].reshape(-1), rwkv_ln_g[j], rwkv_ln_b[j], cd_w_out[j])
            xb = _ffn(xb, ffn2_norm[layer], ffn2_wg[layer].astype(BF16), ffn2_wu[layer].astype(BF16),
                      ffn2_wd[layer].astype(BF16), final_norm, final_norm=(layer == depth - 1))
        outs.append(xb)
    return jnp.stack(outs, axis=0)
```

```python
import functools
import math

import jax
import jax.numpy as jnp
from jax import lax
from jax.experimental import pallas as pl
from jax.experimental.pallas import tpu as pltpu

F32 = jnp.float32
BF16 = jnp.bfloat16

D_MODEL = 1024
D_FF = 2816
NORM_EPS = 1e-6
MIX_W = 512

RET_HEADS = 4
RET_DK = 128
RET_CHUNK = 128
ROPE_BASE = 10000.0
HEAD_NORM_EPS = 1e-6

SSM_HEADS = 8
SSM_HEAD_DIM = 64
SSM_GROUPS = 2
SSM_STATE = 64
SSM_CONV = 4
SSM_CHUNK = 128
SSM_XBC = 768
SSM_NORM_EPS = 1e-5

LRU_BLOCKS = 4
LRU_BLOCK = 128
LRU_CONV = 4
LRU_C = 8.0

RWKV_HEADS = 8
RWKV_HEAD_DIM = 64
DECAY_LORA = 64
ICL_LORA = 64
GATE_LORA = 160
RWKV_LN_EPS = 64e-5
RWKV_CHUNK = 64

LANES = 128
SUBLANES = 8
VMEM_LIMIT = 56 * 1024 * 1024

FFN_TM = 1024
FFN_TF = 256
MIX_TM = 256

AB_Q, AB_K, AB_V, AB_G, AB_Z, AB_XBC, AB_DT = 0, 512, 1024, 1536, 2048, 2560, 3328
AB_COLS = 3456
CD_XB, CD_GB, CD_PR = 0, 512, 1024
CD_PR_COLS = 2048
CD_COLS = CD_PR + CD_PR_COLS


def _bdot(a, b):
    return jnp.dot(a.astype(BF16), b.astype(BF16), preferred_element_type=F32)


def _bdot_nt(a, b):
    return lax.dot_general(a.astype(BF16), b.astype(BF16), (((1,), (1,)), ((), ())),
                           preferred_element_type=F32)


def _bdot_tn(a, b):
    return lax.dot_general(a.astype(BF16), b.astype(BF16), (((0,), (0,)), ((), ())),
                           preferred_element_type=F32)


def _split(x, pieces):
    out = []
    for _ in range(pieces - 1):
        p = x.astype(BF16)
        out.append(p)
        x = x - p.astype(F32)
    out.append(x.astype(BF16))
    return out


def _sel_dot_lhs(x, sel_stack, pieces):
    return jnp.dot(jnp.concatenate(_split(x, pieces), axis=1), sel_stack, preferred_element_type=F32)


def _sel_dot_rhs(sel, x, pieces):
    n = x.shape[1]
    res = jnp.dot(sel, jnp.concatenate(_split(x, pieces), axis=1), preferred_element_type=F32)
    out = res[:, 0:n]
    for p in range(1, pieces):
        out = out + res[:, p * n:(p + 1) * n]
    return out


def _rms_norm(x, g, eps):
    return x * lax.rsqrt(jnp.mean(x * x, axis=-1, keepdims=True) + eps) * g


def _silu(x):
    return x * jax.nn.sigmoid(x)


def _softplus(x):
    return jnp.maximum(x, 0.0) + jnp.log1p(jnp.exp(-jnp.abs(x)))


def _gelu_tanh(x):
    c = math.sqrt(2.0 / math.pi)
    return 0.5 * x * (1.0 + jnp.tanh(c * (x + 0.044715 * (x * x * x))))


def _causal_conv(ext_ref, x, w_ref, b_ref, tm, width):
    ext_ref[pl.ds(SUBLANES, tm), :] = x
    acc = b_ref[...] + w_ref[width - 1:width, :] * x
    for k in range(width - 1):
        acc = acc + w_ref[k:k + 1, :] * ext_ref[pl.ds(SUBLANES - (width - 1) + k, tm), :]
    ext_ref[pl.ds(0, SUBLANES), :] = ext_ref[pl.ds(tm, SUBLANES), :]
    return acc


def _ffn_body(x_ref, g_ref, wg_ref, wu_ref, wd_ref, fin_ref, o_ref, act_sc, *, final_norm):
    x = x_ref[...]
    h = _rms_norm(x, g_ref[...], NORM_EPS).astype(BF16)
    for j in range(D_FF // FFN_TF):
        cols = slice(j * FFN_TF, (j + 1) * FFN_TF)
        gate = jnp.dot(h, wg_ref[:, cols], preferred_element_type=F32)
        up = jnp.dot(h, wu_ref[:, cols], preferred_element_type=F32)
        act_sc[:, cols] = (_silu(gate) * up).astype(BF16)
    y = x + 0.5 * jnp.dot(act_sc[...], wd_ref[...], preferred_element_type=F32)
    if final_norm:
        y = _rms_norm(y, fin_ref[...], NORM_EPS)
    o_ref[...] = y


def _resident_spec(shape):
    nd = len(shape)
    return pl.BlockSpec(shape, lambda i, _nd=nd: (0,) * _nd, pipeline_mode=pl.Buffered(1))


def _ffn(x, norm_g, wg, wu, wd, fin_g, *, final_norm):
    t_len, d = x.shape
    tm = min(FFN_TM, t_len)
    return pl.pallas_call(
        functools.partial(_ffn_body, final_norm=final_norm),
        grid=(t_len // tm,),
        in_specs=[
            pl.BlockSpec((tm, d), lambda i: (i, 0)),
            _resident_spec((1, d)),
            _resident_spec((d, D_FF)),
            _resident_spec((d, D_FF)),
            _resident_spec((D_FF, d)),
            _resident_spec((1, d)),
        ],
        out_specs=pl.BlockSpec((tm, d), lambda i: (i, 0)),
        out_shape=jax.ShapeDtypeStruct((t_len, d), F32),
        scratch_shapes=[pltpu.VMEM((tm, D_FF), BF16)],
        compiler_params=pltpu.CompilerParams(
            dimension_semantics=("arbitrary",), vmem_limit_bytes=VMEM_LIMIT),
        name="ffn_final" if final_norm else "ffn",
    )(x, norm_g.reshape(1, d), wg, wu, wd, fin_g.reshape(1, d))


def _ab_body(x_ref, cos_ref, sin_ref, g_ref, win_ref, rdec_ref, rqfs_ref, rkte_ref, rcd_ref,
             cw_ref, cb_ref, dtb_ref, alog_ref, dexp_ref, sng_ref, exp_ref, tri_ref, wout_ref,
             o_ref, sret_sc, sssm_sc, ext_sc, *, tm):
    @pl.when(pl.program_id(0) == 0)
    def _():
        sret_sc[...] = jnp.zeros_like(sret_sc)
        sssm_sc[...] = jnp.zeros_like(sssm_sc)
        ext_sc[pl.ds(0, SUBLANES), :] = jnp.zeros((SUBLANES, SSM_XBC), F32)

    x = x_ref[...]
    h = _rms_norm(x, g_ref[...], NORM_EPS).astype(BF16)

    def proj(lo, width):
        return jnp.dot(h, win_ref[:, lo:lo + width], preferred_element_type=F32)

    cos2 = cos_ref[...]
    sin2 = sin_ref[...]

    def rotary(t):
        return t * cos2 + pltpu.roll(t, RET_DK // 2, axis=1) * sin2

    q_all = proj(AB_Q, MIX_W)
    k_all = proj(AB_K, MIX_W)
    v_all = proj(AB_V, MIX_W)
    gate_ret = _silu(proj(AB_G, MIX_W))
    scale = RET_DK ** -0.5
    c_len = RET_CHUNK
    ret_rows = []
    for c in range(tm // c_len):
        rs = slice(c * c_len, (c + 1) * c_len)
        heads = []
        for hh in range(RET_HEADS):
            ls = slice(hh * RET_DK, (hh + 1) * RET_DK)
            qh = q_all[rs, ls] * cos2[rs] + pltpu.roll(q_all[rs, ls], RET_DK // 2, axis=1) * sin2[rs]
            kh = (k_all[rs, ls] * cos2[rs] + pltpu.roll(k_all[rs, ls], RET_DK // 2, axis=1) * sin2[rs]) * scale
            vh = v_all[rs, ls]
            s_prev = sret_sc[hh]
            scores = _bdot_nt(qh, kh) * rdec_ref[hh]
            o = _bdot(scores, vh) + _bdot(qh * rqfs_ref[:, ls], s_prev)
            sret_sc[hh] = s_prev * rcd_ref[:, ls] + _bdot_tn(kh * rkte_ref[:, ls], vh)
            o = o * lax.rsqrt(jnp.mean(o * o, axis=-1, keepdims=True) + HEAD_NORM_EPS)
            heads.append(o)
        ret_rows.append(jnp.concatenate(heads, axis=1))
    y_ret = gate_ret * jnp.concatenate(ret_rows, axis=0)

    z = proj(AB_Z, MIX_W)
    xbc = _silu(_causal_conv(ext_sc, proj(AB_XBC, SSM_XBC), cw_ref, cb_ref, tm, SSM_CONV))
    xs = xbc[:, :MIX_W]
    bm = xbc[:, MIX_W:MIX_W + LANES]
    cm = xbc[:, MIX_W + LANES:MIX_W + 2 * LANES]
    lane = lax.broadcasted_iota(jnp.int32, (1, LANES), 1)
    dt = _softplus(proj(AB_DT, LANES) + dtb_ref[...])
    a_neg = jnp.where(lane < SSM_HEADS, -jnp.exp(alog_ref[...]), 0.0)
    adt = dt * a_neg
    xdt = xs * _sel_dot_lhs(dt, exp_ref[0:2 * LANES, :], 2)
    acs_all = _sel_dot_rhs(tri_ref[...], adt, 3)
    acs_e_all = _sel_dot_lhs(acs_all, exp_ref[...], 3)
    l_len = SSM_CHUNK
    row = lax.broadcasted_iota(jnp.int32, (l_len, l_len), 0)
    col = lax.broadcasted_iota(jnp.int32, (l_len, l_len), 1)
    causal = row >= col
    first_half = lax.broadcasted_iota(jnp.int32, (1, LANES), 1) < SSM_HEAD_DIM
    grp_mask = [first_half, jnp.logical_not(first_half)]
    srow = lax.broadcasted_iota(jnp.int32, (LANES, MIX_W), 0)
    scol = lax.broadcasted_iota(jnp.int32, (LANES, MIX_W), 1)
    state_mask = (srow // SSM_STATE) == (scol // (MIX_W // SSM_GROUPS))
    ssm_rows = []
    for c in range(tm // l_len):
        rs = slice(c * l_len, (c + 1) * l_len)
        acs = acs_all[rs]
        acs_e = acs_e_all[rs]
        acs_t = acs.T
        last_e = acs_e[l_len - 1:l_len, :]
        bm_c = bm[rs]
        cm_c = cm[rs]
        s_prev = sssm_sc[...]
        y_off = _bdot(cm_c, s_prev) * jnp.exp(acs_e)
        cb = [_bdot_nt(jnp.where(grp_mask[g], cm_c, 0.0), bm_c) for g in range(SSM_GROUPS)]
        tiles = []
        for j in range(MIX_W // LANES):
            xt = xdt[rs, j * LANES:(j + 1) * LANES]
            outs = []
            for hh in (2 * j, 2 * j + 1):
                seg = acs[:, hh:hh + 1] - acs_t[hh:hh + 1, :]
                dec = jnp.where(causal, jnp.exp(jnp.where(causal, seg, 0.0)), 0.0)
                outs.append(_bdot(cb[hh // (SSM_HEADS // SSM_GROUPS)] * dec, xt))
            tiles.append(jnp.where(first_half, outs[0], outs[1]))
        y_diag = jnp.concatenate(tiles, axis=1)
        upd = _bdot_tn(bm_c, xdt[rs] * jnp.exp(last_e - acs_e))
        sssm_sc[...] = s_prev * jnp.exp(last_e) + jnp.where(state_mask, upd, 0.0)
        y = y_diag + y_off + dexp_ref[...] * xs[rs]
        y = y * _silu(z[rs])
        halves = []
        for g in range(SSM_GROUPS):
            yg = y[:, g * (MIX_W // SSM_GROUPS):(g + 1) * (MIX_W // SSM_GROUPS)]
            halves.append(yg * lax.rsqrt(jnp.mean(yg * yg, axis=-1, keepdims=True) + SSM_NORM_EPS))
        ssm_rows.append(jnp.concatenate(halves, axis=1) * sng_ref[...])
    y_ssm = jnp.concatenate(ssm_rows, axis=0)

    o_ref[...] = x + (_bdot(y_ret, wout_ref[0:MIX_W, :]) + _bdot(y_ssm, wout_ref[MIX_W:2 * MIX_W, :]))


def _mix_ab(x, norm_g, w_in, conv_w, conv_b, dt_bias, a_log, d_skip, norm_ssm, w_out):
    t_len, d = x.shape
    tm = min(MIX_TM, t_len)

    inv_freq = ROPE_BASE ** (-jnp.arange(0, RET_DK, 2, dtype=F32) / RET_DK)
    ang = jnp.arange(t_len, dtype=F32)[:, None] * inv_freq[None, :]
    cos2 = jnp.concatenate([jnp.cos(ang), jnp.cos(ang)], axis=1)
    sin2 = jnp.concatenate([-jnp.sin(ang), jnp.sin(ang)], axis=1)

    c = RET_CHUNK
    log_g = jnp.log1p(-(2.0 ** (-5.0 - jnp.arange(RET_HEADS, dtype=F32))))
    pos = jnp.arange(c, dtype=F32)
    rel = pos[:, None] - pos[None, :]
    rdec = jnp.where(rel >= 0, jnp.exp(jnp.maximum(rel, 0.0)[None] * log_g[:, None, None]), 0.0)
    per_head = lambda t: jnp.repeat(t, RET_DK, axis=1)
    rkte = per_head(jnp.exp((c - 1 - pos)[:, None] * log_g[None, :]))
    rqfs = per_head(jnp.exp((pos + 1.0)[:, None] * log_g[None, :]))
    rcd = per_head(jnp.exp(c * log_g)[None, :])

    w_pad = jnp.pad(w_in, ((0, 0), (0, AB_COLS - w_in.shape[1]))).astype(BF16)
    pad_row = lambda v: jnp.pad(v.astype(F32), (0, LANES - v.shape[0])).reshape(1, LANES)
    expand = (jnp.arange(LANES)[:, None] == (jnp.arange(MIX_W)[None, :] // SSM_HEAD_DIM)).astype(BF16)
    expand = jnp.concatenate([expand] * 3, axis=0)
    t_idx = jnp.arange(tm)
    tri = (((t_idx[:, None] // SSM_CHUNK) == (t_idx[None, :] // SSM_CHUNK)) &
           (t_idx[:, None] >= t_idx[None, :])).astype(BF16)

    operands = [
        (x, pl.BlockSpec((tm, d), lambda i: (i, 0))),
        (cos2, pl.BlockSpec((tm, RET_DK), lambda i: (i, 0))),
        (sin2, pl.BlockSpec((tm, RET_DK), lambda i: (i, 0))),
        (norm_g.reshape(1, d), None),
        (w_pad, None),
        (rdec, None), (rqfs, None), (rkte, None), (rcd, None),
        (conv_w, None), (conv_b.reshape(1, SSM_XBC), None),
        (pad_row(dt_bias), None), (pad_row(a_log), None),
        (jnp.repeat(d_skip.astype(F32), SSM_HEAD_DIM).reshape(1, MIX_W), None),
        (norm_ssm.reshape(1, MIX_W), None),
        (expand, None), (tri, None),
        (w_out.astype(BF16), None),
    ]
    args = [a for a, _ in operands]
    specs = [s if s is not None else _resident_spec(a.shape) for a, s in operands]
    return pl.pallas_call(
        functools.partial(_ab_body, tm=tm),
        grid=(t_len // tm,),
        in_specs=specs,
        out_specs=pl.BlockSpec((tm, d), lambda i: (i, 0)),
        out_shape=jax.ShapeDtypeStruct((t_len, d), F32),
        scratch_shapes=[
            pltpu.VMEM((RET_HEADS, RET_DK, RET_DK), F32),
            pltpu.VMEM((LANES, MIX_W), F32),
            pltpu.VMEM((tm + SUBLANES, SSM_XBC), F32),
        ],
        compiler_params=pltpu.CompilerParams(
            dimension_semantics=("arbitrary",), vmem_limit_bytes=VMEM_LIMIT),
        name="mix_ret_ssd",
    )(*args)


def _head_sum(x, ones_blk):
    tiles = [_sel_dot_lhs(x[:, j * LANES:(j + 1) * LANES], ones_blk, 2) for j in range(x.shape[1] // LANES)]
    return jnp.concatenate(tiles, axis=1)


def _cd_body(x_ref, g_ref, win_ref, lcw_ref, lcb_ref, wa_ref, ba_ref, wx_ref, bx_ref, lam_ref,
             mu_ref, w0_ref, w2_ref, a0_ref, a2_ref, g2_ref, kk_ref, ka_ref, rk_ref, lng_ref, lnb_ref,
             ones_ref, sel_ref, wout_ref, o_ref,
             ext_sc, hcar_sc, pcar_sc, s_sc, *, tm):
    @pl.when(pl.program_id(0) == 0)
    def _():
        ext_sc[pl.ds(0, SUBLANES), :] = jnp.zeros((SUBLANES, MIX_W), F32)
        hcar_sc[...] = jnp.zeros_like(hcar_sc)
        pcar_sc[...] = jnp.zeros_like(pcar_sc)
        s_sc[...] = jnp.zeros_like(s_sc)

    x = x_ref[...]
    h = _rms_norm(x, g_ref[...], NORM_EPS).astype(BF16)

    def proj(lo, width):
        return jnp.dot(h, win_ref[:, lo:lo + width], preferred_element_type=F32)

    rows = lax.broadcasted_iota(jnp.int32, (tm, 1), 0)

    xc = _causal_conv(ext_sc, proj(CD_XB, MIX_W), lcw_ref, lcb_ref, tm, LRU_CONV)
    r_parts, i_parts = [], []
    for n in range(LRU_BLOCKS):
        ls = slice(n * LRU_BLOCK, (n + 1) * LRU_BLOCK)
        r_parts.append(jax.nn.sigmoid(_bdot(xc[:, ls], wa_ref[n]) + ba_ref[:, ls]))
        i_parts.append(jax.nn.sigmoid(_bdot(xc[:, ls], wx_ref[n]) + bx_ref[:, ls]))
    r_gate = jnp.concatenate(r_parts, axis=1)
    i_gate = jnp.concatenate(i_parts, axis=1)
    log_a = -LRU_C * r_gate * _softplus(-lam_ref[...])
    a_cum = jnp.exp(log_a)
    th = jnp.tanh(log_a)
    hs = jnp.sqrt(-2.0 * th / (1.0 - th)) * (i_gate * xc)
    shift = 1
    while shift < tm:
        valid = rows >= shift
        a_prev = jnp.where(valid, pltpu.roll(a_cum, shift, axis=0), 1.0)
        h_prev = jnp.where(valid, pltpu.roll(hs, shift, axis=0), 0.0)
        hs = hs + a_cum * h_prev
        a_cum = a_cum * a_prev
        shift *= 2
    hs = hs + a_cum * hcar_sc[...]
    hcar_sc[...] = hs[tm - 1:tm, :]
    y_lru = hs * _gelu_tanh(proj(CD_GB, MIX_W))

    pr = proj(CD_PR, CD_PR_COLS)
    prev = jnp.where(rows == 0, pcar_sc[...], pltpu.roll(pr, 1, axis=0))
    pcar_sc[...] = pr[tm - 1:tm, :]
    ps = pr + (prev - pr) * mu_ref[...]
    r = ps[:, 0:MIX_W]
    k = ps[:, MIX_W:2 * MIX_W]
    v = ps[:, 2 * MIX_W:3 * MIX_W]
    wl = ps[:, 3 * MIX_W:3 * MIX_W + LANES]
    al = ps[:, 3 * MIX_W + LANES:3 * MIX_W + 2 * LANES]
    gl = ps[:, 3 * MIX_W + 2 * LANES:3 * MIX_W + 4 * LANES]
    w = -_softplus(-(w0_ref[...] + _bdot(jnp.tanh(wl), w2_ref[...]))) - 0.5
    log_w = -jnp.exp(w)
    a = jax.nn.sigmoid(a0_ref[...] + _bdot(al, a2_ref[...]))
    g = _bdot(jax.nn.sigmoid(gl), g2_ref[...])
    ones_blk = ones_ref[...]
    kk = k * kk_ref[...]
    kk = kk / jnp.maximum(jnp.sqrt(_head_sum(kk * kk, ones_blk)), 1e-12)
    k = k * (1.0 + (a - 1.0) * ka_ref[...])
    c_len = RWKV_CHUNK
    pair = 2 * c_len
    n_chunks = tm // c_len
    n_pairs = MIX_W // LANES

    res = _sel_dot_rhs(sel_ref[...], log_w, 3)
    cs = res[0:tm]
    tot = res[tm:2 * tm]
    e_neg = jnp.exp(-cs)
    e_end = jnp.exp(tot - cs)
    w_end = jnp.exp(tot)
    nb = -(kk * a)
    bt = kk * jnp.exp(cs - log_w)
    rt = r * jnp.exp(cs)
    ab = nb * e_neg
    kb = k * e_neg
    ae = nb * e_end
    ke = k * e_end

    lane = lax.broadcasted_iota(jnp.int32, (1, LANES), 1)
    m0 = lane < RWKV_HEAD_DIM
    trow = lax.broadcasted_iota(jnp.int32, (c_len, LANES), 0)
    scol = lax.broadcasted_iota(jnp.int32, (c_len, LANES), 1) % c_len
    strict = trow > scol
    incl = trow >= scol
    eye = (trow == scol).astype(F32)

    def merge_mask(s):
        return ((trow // s) == (scol // s) + 1) & ((trow // s) % 2 == 1)

    def blockdiag(t):
        return jnp.concatenate([jnp.where(m0, t, 0.0), jnp.where(m0, 0.0, t)], axis=0)

    def tile(t, c, j):
        return t[c * c_len:(c + 1) * c_len, j * LANES:(j + 1) * LANES]

    units = [(c, j) for c in range(n_chunks) for j in range(n_pairs)]
    vbd = {u: blockdiag(tile(v, *u)).astype(BF16) for u in units}
    n_mat, lhs_x, lhs_y, lhs_z, w_col = {}, {}, {}, {}, {}
    for u in units:
        bt_u = tile(bt, *u).astype(BF16)
        rt_u = tile(rt, *u).astype(BF16)
        gram = _bdot_nt(jnp.concatenate([bt_u, rt_u], axis=0),
                        jnp.concatenate([blockdiag(tile(ab, *u)), blockdiag(tile(kb, *u))], axis=0))
        n_mat[u] = jnp.where(strict, gram[0:c_len, 0:LANES], 0.0)
        lhs_x[u] = jnp.concatenate(
            [bt_u, jnp.where(strict, gram[0:c_len, LANES:2 * LANES], 0.0).astype(BF16)], axis=1)
        lhs_y[u] = jnp.concatenate(
            [rt_u, jnp.where(incl, gram[c_len:pair, 0:LANES], 0.0).astype(BF16),
             jnp.where(incl, gram[c_len:pair, LANES:2 * LANES], 0.0).astype(BF16)], axis=1)
        lhs_z[u] = jnp.concatenate(
            [blockdiag(tile(ae, *u)).T.astype(BF16), blockdiag(tile(ke, *u)).T.astype(BF16)], axis=1)
        w_row = tile(w_end, *u)
        w_col[u] = jnp.concatenate([w_row, w_row], axis=0).T

    t_inv = {u: eye + jnp.where(merge_mask(1), n_mat[u], 0.0) for u in units}
    s = 2
    while s < c_len:
        mask = merge_mask(s)
        half = {u: _bdot(t_inv[u], blockdiag(jnp.where(mask, n_mat[u], 0.0))) for u in units}
        t_inv = {u: t_inv[u] + _bdot(half[u], blockdiag(t_inv[u])) for u in units}
        s *= 2
    t_inv = {u: t_inv[u].astype(BF16) for u in units}

    y_rows = []
    for c in range(n_chunks):
        z_prev = [s_sc[j] for j in range(n_pairs)]
        zb = [z.astype(BF16) for z in z_prev]
        xs = [jnp.dot(lhs_x[(c, j)], jnp.concatenate([zb[j], vbd[(c, j)]], axis=0),
                      preferred_element_type=F32) for j in range(n_pairs)]
        us = [blockdiag(jnp.dot(t_inv[(c, j)], blockdiag(xs[j]).astype(BF16),
                                preferred_element_type=F32)).astype(BF16) for j in range(n_pairs)]
        tiles = []
        for j in range(n_pairs):
            tiles.append(jnp.dot(lhs_y[(c, j)], jnp.concatenate([zb[j], us[j], vbd[(c, j)]], axis=0),
                                 preferred_element_type=F32))
            s_sc[j] = z_prev[j] * w_col[(c, j)] + jnp.dot(
                lhs_z[(c, j)], jnp.concatenate([us[j], vbd[(c, j)]], axis=0), preferred_element_type=F32)
        y_rows.append(jnp.concatenate(tiles, axis=1))
    y = jnp.concatenate(y_rows, axis=0)
    inv_n = 1.0 / RWKV_HEAD_DIM
    mean = _head_sum(y, ones_blk) * inv_n
    yc = y - mean
    var = _head_sum(yc * yc, ones_blk) * inv_n
    y = yc * lax.rsqrt(var + RWKV_LN_EPS) * lng_ref[...] + lnb_ref[...]
    y = y + _head_sum(r * k * rk_ref[...], ones_blk) * v
    y_rwkv = y * g

    o_ref[...] = x + (_bdot(y_lru, wout_ref[0:MIX_W, :]) + _bdot(y_rwkv, wout_ref[MIX_W:2 * MIX_W, :]))


def _mix_cd(x, norm_g, w_in, lru_conv_w, lru_conv_b, lru_wa, lru_ba, lru_wx, lru_bx, lru_lambda,
            mu, w0, w2, a0, a2, g2, k_k, k_a, r_k, ln_g, ln_b, w_out):
    t_len, d = x.shape
    tm = min(MIX_TM, t_len)
    c3 = 3 * MIX_W

    def pad_lora(t, lo, width, padded):
        return jnp.pad(t[..., lo:lo + width], [(0, 0)] * (t.ndim - 1) + [(0, padded - width)])

    def pad_cols(t, base):
        return jnp.concatenate([
            t[..., :base + c3],
            pad_lora(t, base + c3, DECAY_LORA, LANES),
            pad_lora(t, base + c3 + DECAY_LORA, ICL_LORA, LANES),
            pad_lora(t, base + c3 + DECAY_LORA + ICL_LORA, GATE_LORA, 2 * LANES)], axis=-1)

    w_pad = pad_cols(w_in, 2 * MIX_W).astype(BF16)
    mu_pad = pad_cols(mu.reshape(1, -1), 0)
    pad_rows = lambda t, n: jnp.pad(t, ((0, n - t.shape[0]), (0, 0))).astype(BF16)
    row = lambda t: t.reshape(1, -1).astype(F32)
    ones_blk = ((jnp.arange(LANES)[:, None] // RWKV_HEAD_DIM) ==
                (jnp.arange(LANES)[None, :] // RWKV_HEAD_DIM)).astype(BF16)
    ones_blk = jnp.concatenate([ones_blk] * 2, axis=0)
    t_idx = jnp.arange(tm)
    same_chunk = (t_idx[:, None] // RWKV_CHUNK) == (t_idx[None, :] // RWKV_CHUNK)
    sel = jnp.concatenate([same_chunk & (t_idx[:, None] >= t_idx[None, :]), same_chunk], axis=0).astype(BF16)

    operands = [
        (x, pl.BlockSpec((tm, d), lambda i: (i, 0))),
        (row(norm_g), None),
        (w_pad, None),
        (lru_conv_w, None), (row(lru_conv_b), None),
        (lru_wa.astype(BF16), None), (row(lru_ba), None),
        (lru_wx.astype(BF16), None), (row(lru_bx), None),
        (row(lru_lambda), None),
        (mu_pad, None),
        (row(w0), None), (pad_rows(w2, LANES), None),
        (row(a0), None), (pad_rows(a2, LANES), None),
        (pad_rows(g2, 2 * LANES), None),
        (row(k_k), None), (row(k_a), None), (row(r_k), None), (row(ln_g), None), (row(ln_b), None),
        (ones_blk, None), (sel, None),
        (w_out.astype(BF16), None),
    ]
    args = [a for a, _ in operands]
    specs = [s if s is not None else _resident_spec(a.shape) for a, s in operands]
    return pl.pallas_call(
        functools.partial(_cd_body, tm=tm),
        grid=(t_len // tm,),
        in_specs=specs,
        out_specs=pl.BlockSpec((tm, d), lambda i: (i, 0)),
        out_shape=jax.ShapeDtypeStruct((t_len, d), F32),
        scratch_shapes=[
            pltpu.VMEM((tm + SUBLANES, MIX_W), F32),
            pltpu.VMEM((1, MIX_W), F32),
            pltpu.VMEM((1, CD_PR_COLS), F32),
            pltpu.VMEM((MIX_W // LANES, LANES, LANES), F32),
        ],
        compiler_params=pltpu.CompilerParams(
            dimension_semantics=("arbitrary",), vmem_limit_bytes=VMEM_LIMIT),
        name="mix_lru_rwkv",
    )(*args)


def kernel(x, ffn1_norm, ffn1_wg, ffn1_wu, ffn1_wd, mix_norm, ffn2_norm, ffn2_wg, ffn2_wu, ffn2_wd,
           ab_w_in, ab_w_out, ssm_conv_w, ssm_conv_b, ssm_dt_bias, ssm_a_log, ssm_d, ssm_norm,
           cd_w_in, cd_w_out, lru_conv_w, lru_conv_b, lru_wa, lru_ba, lru_wx, lru_bx, lru_lambda,
           rwkv_mu, rwkv_w0, rwkv_w2, rwkv_a0, rwkv_a2, rwkv_g2, rwkv_kk, rwkv_ka, rwkv_rk,
           rwkv_ln_g, rwkv_ln_b, final_norm):
    bsz, t_len, d = x.shape
    depth = ffn1_norm.shape[0]
    outs = []
    for b in range(bsz):
        xb = x[b]
        for layer in range(depth):
            j = layer // 2
            xb = _ffn(xb, ffn1_norm[layer], ffn1_wg[layer].astype(BF16), ffn1_wu[layer].astype(BF16),
                      ffn1_wd[layer].astype(BF16), final_norm, final_norm=False)
            if layer % 2 == 0:
                xb = _mix_ab(xb, mix_norm[layer], ab_w_in[j], ssm_conv_w[j], ssm_conv_b[j], ssm_dt_bias[j],
                             ssm_a_log[j], ssm_d[j], ssm_norm[j], ab_w_out[j])
            else:
                xb = _mix_cd(xb, mix_norm[layer], cd_w_in[j], lru_conv_w[j], lru_conv_b[j], lru_wa[j],
                             lru_ba[j], lru_wx[j], lru_bx[j], lru_lambda[j], rwkv_mu[j], rwkv_w0[j],
                             rwkv_w2[j], rwkv_a0[j], rwkv_a2[j], rwkv_g2[j], rwkv_kk[j], rwkv_ka[j],
                             rwkv_rk[j].reshape(-1), rwkv_ln_g[j], rwkv_ln_b[j], cd_w_out[j])
            xb = _ffn(xb, ffn2_norm[layer], ffn2_wg[layer].astype(BF16), ffn2_wu[layer].astype(BF16),
                      ffn2_wd[layer].astype(BF16), final_norm, final_norm=(layer == depth - 1))
        outs.append(xb)
    return jnp.stack(outs, axis=0)
```

```python
import functools
import math

import jax
import jax.numpy as jnp
import numpy as np
from jax import lax
from jax.experimental import pallas as pl
from jax.experimental.pallas import tpu as pltpu

F32 = jnp.float32
BF16 = jnp.bfloat16

D_MODEL = 1024
D_FF = 2816
NORM_EPS = 1e-6
MIX_W = 512

RET_HEADS = 4
RET_DK = 128
RET_CHUNK = 128
ROPE_BASE = 10000.0
HEAD_NORM_EPS = 1e-6

SSM_HEADS = 8
SSM_HEAD_DIM = 64
SSM_GROUPS = 2
SSM_STATE = 64
SSM_CONV = 4
SSM_CHUNK = 128
SSM_XBC = 768
SSM_NORM_EPS = 1e-5

LRU_BLOCKS = 4
LRU_BLOCK = 128
LRU_CONV = 4
LRU_C = 8.0

RWKV_HEADS = 8
RWKV_HEAD_DIM = 64
DECAY_LORA = 64
ICL_LORA = 64
GATE_LORA = 160
RWKV_LN_EPS = 64e-5
RWKV_CHUNK = 64

LANES = 128
SUBLANES = 8
VMEM_LIMIT = 56 * 1024 * 1024

FFN_TM = 512
FFN_TF = 256
MXU_DIM = 256
FFN_TK = (0, 6 * MXU_DIM, D_FF)
MIX_TM = 256

AB_Q, AB_K, AB_V, AB_G, AB_Z, AB_XBC, AB_DT = 0, 512, 1024, 1536, 2048, 2560, 3328
CD_XB, CD_GB, CD_PR, CD_LORA = 0, 512, 1024, 2560
CD_PR_COLS = 2048


def _bdot(a, b):
    return jnp.dot(a.astype(BF16), b.astype(BF16), preferred_element_type=F32)


def _bdot_nt(a, b):
    return lax.dot_general(a.astype(BF16), b.astype(BF16), (((1,), (1,)), ((), ())),
                           preferred_element_type=F32)


def _bdot_tn(a, b):
    return lax.dot_general(a.astype(BF16), b.astype(BF16), (((0,), (0,)), ((), ())),
                           preferred_element_type=F32)


def _split(x, pieces):
    out = []
    for _ in range(pieces - 1):
        p = x.astype(BF16)
        out.append(p)
        x = x - p.astype(F32)
    out.append(x.astype(BF16))
    return out


def _sel_dot_lhs(x, sel_stack, pieces):
    return jnp.dot(jnp.concatenate(_split(x, pieces), axis=1), sel_stack, preferred_element_type=F32)


def _sel_dot_rhs(sel, x, pieces):
    n = x.shape[1]
    res = jnp.dot(sel, jnp.concatenate(_split(x, pieces), axis=1), preferred_element_type=F32)
    out = res[:, 0:n]
    for p in range(1, pieces):
        out = out + res[:, p * n:(p + 1) * n]
    return out


def _rms_norm(x, g, eps):
    return x * lax.rsqrt(jnp.mean(x * x, axis=-1, keepdims=True) + eps) * g


def _silu(x):
    return x * jax.nn.sigmoid(x)


def _softplus(x):
    return jnp.maximum(x, 0.0) + jnp.log1p(jnp.exp(-jnp.abs(x)))


def _gelu_tanh(x):
    c = math.sqrt(2.0 / math.pi)
    return 0.5 * x * (1.0 + jnp.tanh(c * (x + 0.044715 * (x * x * x))))


def _causal_conv(ext_ref, x, w_ref, b_ref, tm, width):
    ext_ref[pl.ds(SUBLANES, tm), :] = x
    acc = b_ref[...] + w_ref[width - 1:width, :] * x
    for k in range(width - 1):
        acc = acc + w_ref[k:k + 1, :] * ext_ref[pl.ds(SUBLANES - (width - 1) + k, tm), :]
    ext_ref[pl.ds(0, SUBLANES), :] = ext_ref[pl.ds(tm, SUBLANES), :]
    return acc


def _ffn_body(x_ref, g_ref, wg_ref, wu_ref, wd_ref, fin_ref, o_ref, act_sc, *, final_norm):
    x = x_ref[...]
    h = _rms_norm(x, g_ref[...], NORM_EPS).astype(BF16)
    for j in range(D_FF // FFN_TF):
        cols = slice(j * FFN_TF, (j + 1) * FFN_TF)
        gate = jnp.dot(h, wg_ref[:, cols].astype(BF16), preferred_element_type=F32)
        up = jnp.dot(h, wu_ref[:, cols].astype(BF16), preferred_element_type=F32)
        act_sc[:, cols] = (_silu(gate) * up).astype(BF16)
    down = None
    for lo, hi in zip(FFN_TK[:-1], FFN_TK[1:]):
        part = jnp.dot(act_sc[:, lo:hi], wd_ref[lo:hi, :].astype(BF16), preferred_element_type=F32)
        down = part if down is None else down + part
    y = x + 0.5 * down
    if final_norm:
        y = _rms_norm(y, fin_ref[...], NORM_EPS)
    o_ref[...] = y


def _resident_spec(shape):
    nd = len(shape)
    return pl.BlockSpec(shape, lambda i, _nd=nd: (0,) * _nd, pipeline_mode=pl.Buffered(1))


def _layer_spec(shape, layer):
    nd = len(shape)
    return pl.BlockSpec((None,) + tuple(shape), lambda i, _nd=nd: (layer,) + (0,) * _nd,
                        pipeline_mode=pl.Buffered(1))


def _ffn(x, norm_g, wg, wu, wd, layer, fin_g, *, final_norm):
    t_len, d = x.shape
    tm = min(FFN_TM, t_len)
    return pl.pallas_call(
        functools.partial(_ffn_body, final_norm=final_norm),
        grid=(t_len // tm,),
        in_specs=[
            pl.BlockSpec((tm, d), lambda i: (i, 0)),
            _resident_spec((1, d)),
            _layer_spec((d, D_FF), layer),
            _layer_spec((d, D_FF), layer),
            _layer_spec((D_FF, d), layer),
            _resident_spec((1, d)),
        ],
        out_specs=pl.BlockSpec((tm, d), lambda i: (i, 0)),
        out_shape=jax.ShapeDtypeStruct((t_len, d), F32),
        scratch_shapes=[pltpu.VMEM((tm, D_FF), BF16)],
        compiler_params=pltpu.CompilerParams(
            dimension_semantics=("arbitrary",), vmem_limit_bytes=VMEM_LIMIT),
        name="ffn_final" if final_norm else "ffn",
    )(x, norm_g.reshape(1, d), wg, wu, wd, fin_g.reshape(1, d))


def _cast_weight(dst_sc, src_ref):
    for lo in range(0, src_ref.shape[1], MXU_DIM):
        hi = min(lo + MXU_DIM, src_ref.shape[1])
        dst_sc[:, lo:hi] = src_ref[:, lo:hi].astype(BF16)


def _ab_body(x_ref, rota_ref, rotb_ref, g_ref, win_ref, wdt_ref, rdec_ref, rqfs_ref, rkte_ref, rcd_ref,
             cw_ref, cb_ref, dtb_ref, alog_ref, dexp_ref, sng_ref, exp_ref, tri_ref, wout_ref,
             o_ref, sret_sc, sssm_sc, ext_sc, winb_sc, woutb_sc, *, tm):
    @pl.when(pl.program_id(0) == 0)
    def _():
        sret_sc[...] = jnp.zeros_like(sret_sc)
        sssm_sc[...] = jnp.zeros_like(sssm_sc)
        ext_sc[pl.ds(0, SUBLANES), :] = jnp.zeros((SUBLANES, SSM_XBC), F32)
        _cast_weight(winb_sc, win_ref)
        _cast_weight(woutb_sc, wout_ref)

    x = x_ref[...]
    h = _rms_norm(x, g_ref[...], NORM_EPS).astype(BF16)

    def proj(lo, width):
        return jnp.dot(h, winb_sc[:, lo:lo + width], preferred_element_type=F32)

    cos_a, sin_a = rota_ref[0:1, :], rota_ref[1:2, :]
    sin_a_sgn, cos_a_sgn = rota_ref[2:3, :], rota_ref[3:4, :]
    cos_b, sin_b = rotb_ref[:, 0:RET_DK], rotb_ref[:, RET_DK:2 * RET_DK]
    cos2 = cos_a * cos_b - sin_a * sin_b
    sin2 = sin_a_sgn * cos_b + cos_a_sgn * sin_b

    q_all = proj(AB_Q, MIX_W)
    k_all = proj(AB_K, MIX_W)
    v_all = proj(AB_V, MIX_W)
    gate_ret = _silu(proj(AB_G, MIX_W))
    scale = RET_DK ** -0.5
    c_len = RET_CHUNK
    ret_rows = []
    for c in range(tm // c_len):
        rs = slice(c * c_len, (c + 1) * c_len)
        heads = []
        for hh in range(RET_HEADS):
            ls = slice(hh * RET_DK, (hh + 1) * RET_DK)
            qh = q_all[rs, ls] * cos2[rs] + pltpu.roll(q_all[rs, ls], RET_DK // 2, axis=1) * sin2[rs]
            kh = (k_all[rs, ls] * cos2[rs] + pltpu.roll(k_all[rs, ls], RET_DK // 2, axis=1) * sin2[rs]) * scale
            vh = v_all[rs, ls]
            s_prev = sret_sc[hh]
            scores = _bdot_nt(qh, kh) * rdec_ref[hh]
            o = _bdot(jnp.concatenate([scores, qh * rqfs_ref[:, ls]], axis=1),
                      jnp.concatenate([vh, s_prev], axis=0))
            sret_sc[hh] = s_prev * rcd_ref[:, ls] + _bdot_tn(kh * rkte_ref[:, ls], vh)
            o = o * lax.rsqrt(jnp.mean(o * o, axis=-1, keepdims=True) + HEAD_NORM_EPS)
            heads.append(o)
        ret_rows.append(jnp.concatenate(heads, axis=1))
    y_ret = gate_ret * jnp.concatenate(ret_rows, axis=0)

    z = proj(AB_Z, MIX_W)
    xbc = _silu(_causal_conv(ext_sc, proj(AB_XBC, SSM_XBC), cw_ref, cb_ref, tm, SSM_CONV))
    xs = xbc[:, :MIX_W]
    bm = xbc[:, MIX_W:MIX_W + LANES]
    cm = xbc[:, MIX_W + LANES:MIX_W + 2 * LANES]
    lane = lax.broadcasted_iota(jnp.int32, (1, LANES), 1)
    dt = _softplus(jnp.dot(h, wdt_ref[...], preferred_element_type=F32) + dtb_ref[...])
    a_neg = jnp.where(lane < SSM_HEADS, -jnp.exp(alog_ref[...]), 0.0)
    adt = dt * a_neg
    xdt = xs * _sel_dot_lhs(dt, exp_ref[0:2 * LANES, :], 2)
    acs_all = _sel_dot_rhs(tri_ref[...], adt, 3)
    acs_e_all = _sel_dot_lhs(acs_all, exp_ref[...], 3)
    l_len = SSM_CHUNK
    row = lax.broadcasted_iota(jnp.int32, (l_len, l_len), 0)
    col = lax.broadcasted_iota(jnp.int32, (l_len, l_len), 1)
    causal = row >= col
    first_half = lax.broadcasted_iota(jnp.int32, (1, LANES), 1) < SSM_HEAD_DIM
    grp_mask = [first_half, jnp.logical_not(first_half)]
    srow = lax.broadcasted_iota(jnp.int32, (LANES, MIX_W), 0)
    scol = lax.broadcasted_iota(jnp.int32, (LANES, MIX_W), 1)
    state_mask = (srow // SSM_STATE) == (scol // (MIX_W // SSM_GROUPS))
    ssm_rows = []
    for c in range(tm // l_len):
        rs = slice(c * l_len, (c + 1) * l_len)
        acs = acs_all[rs]
        acs_e = acs_e_all[rs]
        acs_t = acs.T
        last_e = acs_e[l_len - 1:l_len, :]
        bm_c = bm[rs]
        cm_c = cm[rs]
        s_prev = sssm_sc[...]
        y_off = _bdot(cm_c, s_prev) * jnp.exp(acs_e)
        cb = [_bdot_nt(jnp.where(grp_mask[g], cm_c, 0.0), bm_c) for g in range(SSM_GROUPS)]
        tiles = []
        for j in range(MIX_W // LANES):
            xt = xdt[rs, j * LANES:(j + 1) * LANES]
            outs = []
            for hh in (2 * j, 2 * j + 1):
                seg = acs[:, hh:hh + 1] - acs_t[hh:hh + 1, :]
                dec = jnp.where(causal, jnp.exp(jnp.where(causal, seg, 0.0)), 0.0)
                outs.append(_bdot(cb[hh // (SSM_HEADS // SSM_GROUPS)] * dec, xt))
            tiles.append(jnp.where(first_half, outs[0], outs[1]))
        y_diag = jnp.concatenate(tiles, axis=1)
        upd = _bdot_tn(bm_c, xdt[rs] * jnp.exp(last_e - acs_e))
        sssm_sc[...] = s_prev * jnp.exp(last_e) + jnp.where(state_mask, upd, 0.0)
        y = y_diag + y_off + dexp_ref[...] * xs[rs]
        y = y * _silu(z[rs])
        halves = []
        for g in range(SSM_GROUPS):
            yg = y[:, g * (MIX_W // SSM_GROUPS):(g + 1) * (MIX_W // SSM_GROUPS)]
            halves.append(yg * lax.rsqrt(jnp.mean(yg * yg, axis=-1, keepdims=True) + SSM_NORM_EPS))
        ssm_rows.append(jnp.concatenate(halves, axis=1) * sng_ref[...])
    y_ssm = jnp.concatenate(ssm_rows, axis=0)

    o_ref[...] = x + _bdot(jnp.concatenate([y_ret, y_ssm], axis=1), woutb_sc[...])


def _rotary_tables(n_tiles, tm):
    inv_freq = ROPE_BASE ** (-np.arange(0, RET_DK, 2, dtype=np.float64) / RET_DK)
    two = lambda t: np.concatenate([t, t], axis=-1)
    sign = np.concatenate([-np.ones(RET_DK // 2), np.ones(RET_DK // 2)])
    ang_a = (np.arange(n_tiles, dtype=np.float64) * tm)[:, None] * inv_freq[None, :]
    ang_b = np.arange(tm, dtype=np.float64)[:, None] * inv_freq[None, :]
    cos_a, sin_a = two(np.cos(ang_a)), two(np.sin(ang_a))
    rot_a = np.zeros((n_tiles, SUBLANES, RET_DK))
    rot_a[:, 0], rot_a[:, 1], rot_a[:, 2], rot_a[:, 3] = cos_a, sin_a, sign * sin_a, sign * cos_a
    rot_b = np.concatenate([two(np.cos(ang_b)), two(np.sin(ang_b))], axis=1)
    return jnp.asarray(rot_a, F32), jnp.asarray(rot_b, F32)


def _retention_tables():
    c = RET_CHUNK
    log_g = np.log1p(-(2.0 ** (-5.0 - np.arange(RET_HEADS, dtype=np.float64))))
    pos = np.arange(c, dtype=np.float64)
    rel = pos[:, None] - pos[None, :]
    rdec = np.where(rel >= 0, np.exp(np.maximum(rel, 0.0)[None] * log_g[:, None, None]), 0.0)
    per_head = lambda t: np.repeat(t, RET_DK, axis=1)
    rkte = per_head(np.exp((c - 1 - pos)[:, None] * log_g[None, :]))
    rqfs = per_head(np.exp((pos + 1.0)[:, None] * log_g[None, :]))
    rcd = per_head(np.exp(c * log_g)[None, :])
    return tuple(jnp.asarray(t, F32) for t in (rdec, rqfs, rkte, rcd))


def _mix_ab(x, norm_g, w_in_all, w_out_all, j, conv_w, conv_b, dt_bias, a_log, d_skip, norm_ssm):
    t_len, d = x.shape
    tm = min(MIX_TM, t_len)
    rot_a, rot_b = _rotary_tables(t_len // tm, tm)
    rdec, rqfs, rkte, rcd = _retention_tables()

    w_dt = jnp.pad(w_in_all[j][:, AB_DT:], ((0, 0), (0, LANES - SSM_HEADS))).astype(BF16)
    pad_row = lambda v: jnp.pad(v.astype(F32), (0, LANES - v.shape[0])).reshape(1, LANES)
    expand = (jnp.arange(LANES)[:, None] == (jnp.arange(MIX_W)[None, :] // SSM_HEAD_DIM)).astype(BF16)
    expand = jnp.concatenate([expand] * 3, axis=0)
    t_idx = jnp.arange(tm)
    tri = (((t_idx[:, None] // SSM_CHUNK) == (t_idx[None, :] // SSM_CHUNK)) &
           (t_idx[:, None] >= t_idx[None, :])).astype(BF16)

    operands = [
        (x, pl.BlockSpec((tm, d), lambda i: (i, 0))),
        (rot_a, pl.BlockSpec((None, SUBLANES, RET_DK), lambda i: (i, 0, 0))),
        (rot_b, None),
        (norm_g.reshape(1, d), None),
        (w_in_all, _layer_spec((d, AB_DT), j)),
        (w_dt, None),
        (rdec, None), (rqfs, None), (rkte, None), (rcd, None),
        (conv_w, None), (conv_b.reshape(1, SSM_XBC), None),
        (pad_row(dt_bias), None), (pad_row(a_log), None),
        (jnp.repeat(d_skip.astype(F32), SSM_HEAD_DIM).reshape(1, MIX_W), None),
        (norm_ssm.reshape(1, MIX_W), None),
        (expand, None), (tri, None),
        (w_out_all, _layer_spec((2 * MIX_W, d), j)),
    ]
    args = [a for a, _ in operands]
    specs = [s if s is not None else _resident_spec(a.shape) for a, s in operands]
    return pl.pallas_call(
        functools.partial(_ab_body, tm=tm),
        grid=(t_len // tm,),
        in_specs=specs,
        out_specs=pl.BlockSpec((tm, d), lambda i: (i, 0)),
        out_shape=jax.ShapeDtypeStruct((t_len, d), F32),
        scratch_shapes=[
            pltpu.VMEM((RET_HEADS, RET_DK, RET_DK), F32),
            pltpu.VMEM((LANES, MIX_W), F32),
            pltpu.VMEM((tm + SUBLANES, SSM_XBC), F32),
            pltpu.VMEM((d, AB_DT), BF16),
            pltpu.VMEM((2 * MIX_W, d), BF16),
        ],
        compiler_params=pltpu.CompilerParams(
            dimension_semantics=("arbitrary",), vmem_limit_bytes=VMEM_LIMIT),
        name="mix_ret_ssd",
    )(*args)


def _head_sum(x, ones_blk):
    tiles = [_sel_dot_lhs(x[:, j * LANES:(j + 1) * LANES], ones_blk, 2) for j in range(x.shape[1] // LANES)]
    return jnp.concatenate(tiles, axis=1)


def _cd_body(x_ref, g_ref, win_ref, lcw_ref, lcb_ref, wa_ref, ba_ref, wx_ref, bx_ref, lam_ref,
             mu_ref, w0_ref, w2_ref, a0_ref, a2_ref, g2_ref, kk_ref, ka_ref, rk_ref, lng_ref, lnb_ref,
             ones_ref, sel_ref, wlora_ref, wout_ref, o_ref,
             ext_sc, hcar_sc, pcar_sc, s_sc, winb_sc, woutb_sc, *, tm):
    @pl.when(pl.program_id(0) == 0)
    def _():
        ext_sc[pl.ds(0, SUBLANES), :] = jnp.zeros((SUBLANES, MIX_W), F32)
        hcar_sc[...] = jnp.zeros_like(hcar_sc)
        pcar_sc[...] = jnp.zeros_like(pcar_sc)
        s_sc[...] = jnp.zeros_like(s_sc)
        _cast_weight(winb_sc, win_ref)
        _cast_weight(woutb_sc, wout_ref)

    x = x_ref[...]
    h = _rms_norm(x, g_ref[...], NORM_EPS).astype(BF16)

    def proj(lo, width):
        return jnp.dot(h, winb_sc[:, lo:lo + width], preferred_element_type=F32)

    rows = lax.broadcasted_iota(jnp.int32, (tm, 1), 0)

    xc = _causal_conv(ext_sc, proj(CD_XB, MIX_W), lcw_ref, lcb_ref, tm, LRU_CONV)
    r_parts, i_parts = [], []
    for n in range(LRU_BLOCKS):
        ls = slice(n * LRU_BLOCK, (n + 1) * LRU_BLOCK)
        r_parts.append(jax.nn.sigmoid(_bdot(xc[:, ls], wa_ref[n]) + ba_ref[:, ls]))
        i_parts.append(jax.nn.sigmoid(_bdot(xc[:, ls], wx_ref[n]) + bx_ref[:, ls]))
    r_gate = jnp.concatenate(r_parts, axis=1)
    i_gate = jnp.concatenate(i_parts, axis=1)
    log_a = -LRU_C * r_gate * _softplus(-lam_ref[...])
    a_cum = jnp.exp(log_a)
    th = jnp.tanh(log_a)
    hs = jnp.sqrt(-2.0 * th / (1.0 - th)) * (i_gate * xc)
    in_group = rows % SUBLANES
    shift = 1
    while shift < SUBLANES:
        valid = in_group >= shift
        a_prev = jnp.where(valid, pltpu.roll(a_cum, shift, axis=0), 1.0)
        h_prev = jnp.where(valid, pltpu.roll(hs, shift, axis=0), 0.0)
        hs = hs + a_cum * h_prev
        a_cum = a_cum * a_prev
        shift *= 2
    carry = hcar_sc[...]
    groups = []
    for gi in range(tm // SUBLANES):
        grp = slice(gi * SUBLANES, (gi + 1) * SUBLANES)
        hg = hs[grp] + a_cum[grp] * carry
        groups.append(hg)
        carry = hg[SUBLANES - 1:SUBLANES, :]
    hs = jnp.concatenate(groups, axis=0)
    hcar_sc[...] = carry
    y_lru = hs * _gelu_tanh(proj(CD_GB, MIX_W))

    pr = jnp.concatenate([proj(CD_PR, 3 * MIX_W), jnp.dot(h, wlora_ref[...], preferred_element_type=F32)], axis=1)
    prev = jnp.where(rows == 0, pcar_sc[...], pltpu.roll(pr, 1, axis=0))
    pcar_sc[...] = pr[tm - 1:tm, :]
    ps = pr + (prev - pr) * mu_ref[...]
    r = ps[:, 0:MIX_W]
    k = ps[:, MIX_W:2 * MIX_W]
    v = ps[:, 2 * MIX_W:3 * MIX_W]
    wl = ps[:, 3 * MIX_W:3 * MIX_W + LANES]
    al = ps[:, 3 * MIX_W + LANES:3 * MIX_W + 2 * LANES]
    gl = ps[:, 3 * MIX_W + 2 * LANES:3 * MIX_W + 4 * LANES]
    w = -_softplus(-(w0_ref[...] + _bdot(jnp.tanh(wl), w2_ref[...]))) - 0.5
    log_w = -jnp.exp(w)
    a = jax.nn.sigmoid(a0_ref[...] + _bdot(al, a2_ref[...]))
    g = _bdot(jax.nn.sigmoid(gl), g2_ref[...])
    ones_blk = ones_ref[...]
    kk = k * kk_ref[...]
    kk = kk / jnp.maximum(jnp.sqrt(_head_sum(kk * kk, ones_blk)), 1e-12)
    k = k * (1.0 + (a - 1.0) * ka_ref[...])
    c_len = RWKV_CHUNK
    pair = 2 * c_len
    n_chunks = tm // c_len
    n_pairs = MIX_W // LANES

    res = _sel_dot_rhs(sel_ref[...], log_w, 3)
    cs = res[0:tm]
    tot = res[tm:2 * tm]
    e_neg = jnp.exp(-cs)
    e_end = jnp.exp(tot - cs)
    w_end = jnp.exp(tot)
    nb = -(kk * a)
    bt = kk * jnp.exp(cs - log_w)
    rt = r * jnp.exp(cs)
    ab = nb * e_neg
    kb = k * e_neg
    ae = nb * e_end
    ke = k * e_end

    lane = lax.broadcasted_iota(jnp.int32, (1, LANES), 1)
    m0 = lane < RWKV_HEAD_DIM
    trow = lax.broadcasted_iota(jnp.int32, (c_len, LANES), 0)
    scol = lax.broadcasted_iota(jnp.int32, (c_len, LANES), 1) % c_len
    strict = trow > scol
    incl = trow >= scol
    eye = (trow == scol).astype(F32)

    def merge_mask(s):
        return ((trow // s) == (scol // s) + 1) & ((trow // s) % 2 == 1)

    def blockdiag(t):
        return jnp.concatenate([jnp.where(m0, t, 0.0), jnp.where(m0, 0.0, t)], axis=0)

    def tile(t, c, j):
        return t[c * c_len:(c + 1) * c_len, j * LANES:(j + 1) * LANES]

    units = [(c, j) for c in range(n_chunks) for j in range(n_pairs)]
    vbd = {u: blockdiag(tile(v, *u)).astype(BF16) for u in units}
    n_mat, lhs_x, lhs_y, lhs_z, w_col = {}, {}, {}, {}, {}
    for u in units:
        bt_u = tile(bt, *u).astype(BF16)
        rt_u = tile(rt, *u).astype(BF16)
        gram = _bdot_nt(jnp.concatenate([bt_u, rt_u], axis=0),
                        jnp.concatenate([blockdiag(tile(ab, *u)), blockdiag(tile(kb, *u))], axis=0))
        n_mat[u] = jnp.where(strict, gram[0:c_len, 0:LANES], 0.0)
        lhs_x[u] = jnp.concatenate(
            [bt_u, jnp.where(strict, gram[0:c_len, LANES:2 * LANES], 0.0).astype(BF16)], axis=1)
        lhs_y[u] = jnp.concatenate(
            [rt_u, jnp.where(incl, gram[c_len:pair, 0:LANES], 0.0).astype(BF16),
             jnp.where(incl, gram[c_len:pair, LANES:2 * LANES], 0.0).astype(BF16)], axis=1)
        lhs_z[u] = jnp.concatenate(
            [blockdiag(tile(ae, *u)).T.astype(BF16), blockdiag(tile(ke, *u)).T.astype(BF16)], axis=1)
        w_row = tile(w_end, *u)
        w_col[u] = jnp.concatenate([w_row, w_row], axis=0).T

    t_inv = {u: eye + jnp.where(merge_mask(1), n_mat[u], 0.0) for u in units}
    s = 2
    while s < c_len:
        mask = merge_mask(s)
        half = {u: _bdot(t_inv[u], blockdiag(jnp.where(mask, n_mat[u], 0.0))) for u in units}
        t_inv = {u: t_inv[u] + _bdot(half[u], blockdiag(t_inv[u])) for u in units}
        s *= 2
    t_inv = {u: t_inv[u].astype(BF16) for u in units}

    y_rows = []
    for c in range(n_chunks):
        z_prev = [s_sc[j] for j in range(n_pairs)]
        zb = [z.astype(BF16) for z in z_prev]
        xs = [jnp.dot(lhs_x[(c, j)], jnp.concatenate([zb[j], vbd[(c, j)]], axis=0),
                      preferred_element_type=F32) for j in range(n_pairs)]
        us = [blockdiag(jnp.dot(t_inv[(c, j)], blockdiag(xs[j]).astype(BF16),
                                preferred_element_type=F32)).astype(BF16) for j in range(n_pairs)]
        tiles = []
        for j in range(n_pairs):
            tiles.append(jnp.dot(lhs_y[(c, j)], jnp.concatenate([zb[j], us[j], vbd[(c, j)]], axis=0),
                                 preferred_element_type=F32))
            s_sc[j] = z_prev[j] * w_col[(c, j)] + jnp.dot(
                lhs_z[(c, j)], jnp.concatenate([us[j], vbd[(c, j)]], axis=0), preferred_element_type=F32)
        y_rows.append(jnp.concatenate(tiles, axis=1))
    y = jnp.concatenate(y_rows, axis=0)
    inv_n = 1.0 / RWKV_HEAD_DIM
    mean = _head_sum(y, ones_blk) * inv_n
    yc = y - mean
    var = _head_sum(yc * yc, ones_blk) * inv_n
    y = yc * lax.rsqrt(var + RWKV_LN_EPS) * lng_ref[...] + lnb_ref[...]
    y = y + _head_sum(r * k * rk_ref[...], ones_blk) * v
    y_rwkv = y * g

    o_ref[...] = x + _bdot(jnp.concatenate([y_lru, y_rwkv], axis=1), woutb_sc[...])


def _mix_cd(x, norm_g, w_in_all, w_out_all, j, lru_conv_w, lru_conv_b, lru_wa, lru_ba, lru_wx, lru_bx,
            lru_lambda, mu, w0, w2, a0, a2, g2, k_k, k_a, r_k, ln_g, ln_b):
    t_len, d = x.shape
    tm = min(MIX_TM, t_len)

    def pad_lora(t, lo, width, padded):
        return jnp.pad(t[..., lo:lo + width], [(0, 0)] * (t.ndim - 1) + [(0, padded - width)])

    def lora_cols(t, base):
        return jnp.concatenate([
            pad_lora(t, base, DECAY_LORA, LANES),
            pad_lora(t, base + DECAY_LORA, ICL_LORA, LANES),
            pad_lora(t, base + DECAY_LORA + ICL_LORA, GATE_LORA, 2 * LANES)], axis=-1)

    w_lora = lora_cols(w_in_all[j], CD_LORA).astype(BF16)
    mu_row = mu.reshape(1, -1)
    mu_pad = jnp.concatenate([mu_row[:, :3 * MIX_W], lora_cols(mu_row, 3 * MIX_W)], axis=-1)
    pad_rows = lambda t, n: jnp.pad(t, ((0, n - t.shape[0]), (0, 0))).astype(BF16)
    row = lambda t: t.reshape(1, -1).astype(F32)
    ones_blk = ((jnp.arange(LANES)[:, None] // RWKV_HEAD_DIM) ==
                (jnp.arange(LANES)[None, :] // RWKV_HEAD_DIM)).astype(BF16)
    ones_blk = jnp.concatenate([ones_blk] * 2, axis=0)
    t_idx = jnp.arange(tm)
    same_chunk = (t_idx[:, None] // RWKV_CHUNK) == (t_idx[None, :] // RWKV_CHUNK)
    sel = jnp.concatenate([same_chunk & (t_idx[:, None] >= t_idx[None, :]), same_chunk], axis=0).astype(BF16)

    operands = [
        (x, pl.BlockSpec((tm, d), lambda i: (i, 0))),
        (row(norm_g), None),
        (w_in_all, _layer_spec((d, CD_LORA), j)),
        (lru_conv_w, None), (row(lru_conv_b), None),
        (lru_wa.astype(BF16), None), (row(lru_ba), None),
        (lru_wx.astype(BF16), None), (row(lru_bx), None),
        (row(lru_lambda), None),
        (mu_pad, None),
        (row(w0), None), (pad_rows(w2, LANES), None),
        (row(a0), None), (pad_rows(a2, LANES), None),
        (pad_rows(g2, 2 * LANES), None),
        (row(k_k), None), (row(k_a), None), (row(r_k), None), (row(ln_g), None), (row(ln_b), None),
        (ones_blk, None), (sel, None),
        (w_lora, None),
        (w_out_all, _layer_spec((2 * MIX_W, d), j)),
    ]
    args = [a for a, _ in operands]
    specs = [s if s is not None else _resident_spec(a.shape) for a, s in operands]
    return pl.pallas_call(
        functools.partial(_cd_body, tm=tm),
        grid=(t_len // tm,),
        in_specs=specs,
        out_specs=pl.BlockSpec((tm, d), lambda i: (i, 0)),
        out_shape=jax.ShapeDtypeStruct((t_len, d), F32),
        scratch_shapes=[
            pltpu.VMEM((tm + SUBLANES, MIX_W), F32),
            pltpu.VMEM((1, MIX_W), F32),
            pltpu.VMEM((1, CD_PR_COLS), F32),
            pltpu.VMEM((MIX_W // LANES, LANES, LANES), F32),
            pltpu.VMEM((d, CD_LORA), BF16),
            pltpu.VMEM((2 * MIX_W, d), BF16),
        ],
        compiler_params=pltpu.CompilerParams(
            dimension_semantics=("arbitrary",), vmem_limit_bytes=VMEM_LIMIT),
        name="mix_lru_rwkv",
    )(*args)


def kernel(x, ffn1_norm, ffn1_wg, ffn1_wu, ffn1_wd, mix_norm, ffn2_norm, ffn2_wg, ffn2_wu, ffn2_wd,
           ab_w_in, ab_w_out, ssm_conv_w, ssm_conv_b, ssm_dt_bias, ssm_a_log, ssm_d, ssm_norm,
           cd_w_in, cd_w_out, lru_conv_w, lru_conv_b, lru_wa, lru_ba, lru_wx, lru_bx, lru_lambda,
           rwkv_mu, rwkv_w0, rwkv_w2, rwkv_a0, rwkv_a2, rwkv_g2, rwkv_kk, rwkv_ka, rwkv_rk,
           rwkv_ln_g, rwkv_ln_b, final_norm):
    bsz, t_len, d = x.shape
    depth = ffn1_norm.shape[0]
    outs = []
    for b in range(bsz):
        xb = x.reshape(t_len, d) if bsz == 1 else x[b]
        for layer in range(depth):
            j = layer // 2
            xb = _ffn(xb, ffn1_norm[layer], ffn1_wg, ffn1_wu, ffn1_wd, layer, final_norm, final_norm=False)
            if layer % 2 == 0:
                xb = _mix_ab(xb, mix_norm[layer], ab_w_in, ab_w_out, j, ssm_conv_w[j], ssm_conv_b[j],
                             ssm_dt_bias[j], ssm_a_log[j], ssm_d[j], ssm_norm[j])
            else:
                xb = _mix_cd(xb, mix_norm[layer], cd_w_in, cd_w_out, j, lru_conv_w[j], lru_conv_b[j], lru_wa[j],
                             lru_ba[j], lru_wx[j], lru_bx[j], lru_lambda[j], rwkv_mu[j], rwkv_w0[j],
                             rwkv_w2[j], rwkv_a0[j], rwkv_a2[j], rwkv_g2[j], rwkv_kk[j], rwkv_ka[j],
                             rwkv_rk[j].reshape(-1), rwkv_ln_g[j], rwkv_ln_b[j])
            xb = _ffn(xb, ffn2_norm[layer], ffn2_wg, ffn2_wu, ffn2_wd, layer, final_norm,
                      final_norm=(layer == depth - 1))
        outs.append(xb)
    return outs[0].reshape(1, t_len, d) if bsz == 1 else jnp.stack(outs, axis=0)
```

```python
import functools
import math

import jax
import jax.numpy as jnp
import numpy as np
from jax import lax
from jax.experimental import pallas as pl
from jax.experimental.pallas import tpu as pltpu

F32 = jnp.float32
BF16 = jnp.bfloat16

D_MODEL = 1024
D_FF = 2816
NORM_EPS = 1e-6
MIX_W = 512

RET_HEADS = 4
RET_DK = 128
RET_CHUNK = 128
ROPE_BASE = 10000.0
HEAD_NORM_EPS = 1e-6

SSM_HEADS = 8
SSM_HEAD_DIM = 64
SSM_GROUPS = 2
SSM_STATE = 64
SSM_CONV = 4
SSM_CHUNK = 128
SSM_XBC = 768
SSM_NORM_EPS = 1e-5

LRU_BLOCKS = 4
LRU_BLOCK = 128
LRU_CONV = 4
LRU_C = 8.0

RWKV_HEADS = 8
RWKV_HEAD_DIM = 64
DECAY_LORA = 64
ICL_LORA = 64
GATE_LORA = 160
RWKV_LN_EPS = 64e-5
RWKV_CHUNK = 64

LANES = 128
SUBLANES = 8
VMEM_LIMIT = 56 * 1024 * 1024

FFN_TM = 512
FFN_TF = 256
MXU_DIM = 256
FFN_TK = (0, 6 * MXU_DIM, D_FF)
MIX_TM = 256
PROJ_TN = 512

AB_Q, AB_K, AB_V, AB_G, AB_Z, AB_XBC, AB_DT = 0, 512, 1024, 1536, 2048, 2560, 3328
CD_XB, CD_GB, CD_PR, CD_LORA = 0, 512, 1024, 2560
CD_PR_COLS = 2048


def _bdot(a, b):
    return jnp.dot(a.astype(BF16), b.astype(BF16), preferred_element_type=F32)


def _bdot_nt(a, b):
    return lax.dot_general(a.astype(BF16), b.astype(BF16), (((1,), (1,)), ((), ())),
                           preferred_element_type=F32)


def _bdot_tn(a, b):
    return lax.dot_general(a.astype(BF16), b.astype(BF16), (((0,), (0,)), ((), ())),
                           preferred_element_type=F32)


def _split(x, pieces):
    out = []
    for _ in range(pieces - 1):
        p = x.astype(BF16)
        out.append(p)
        x = x - p.astype(F32)
    out.append(x.astype(BF16))
    return out


def _sel_dot_lhs(x, sel_stack, pieces):
    return jnp.dot(jnp.concatenate(_split(x, pieces), axis=1), sel_stack, preferred_element_type=F32)


def _sel_dot_rhs(sel, x, pieces):
    n = x.shape[1]
    res = jnp.dot(sel, jnp.concatenate(_split(x, pieces), axis=1), preferred_element_type=F32)
    out = res[:, 0:n]
    for p in range(1, pieces):
        out = out + res[:, p * n:(p + 1) * n]
    return out


def _rms_norm(x, g, eps):
    return x * lax.rsqrt(jnp.mean(x * x, axis=-1, keepdims=True) + eps) * g


def _silu(x):
    return x * jax.nn.sigmoid(x)


def _softplus(x):
    return jnp.maximum(x, 0.0) + jnp.log1p(jnp.exp(-jnp.abs(x)))


def _sqrt_nonneg(y):
    return y * lax.rsqrt(jnp.maximum(y, jnp.finfo(jnp.float32).tiny))


def _gelu_tanh(x):
    c = math.sqrt(2.0 / math.pi)
    return 0.5 * x * (1.0 + jnp.tanh(c * (x + 0.044715 * (x * x * x))))


def _causal_conv(tail_ref, x, w_ref, b_ref, tm, width):
    tail = tail_ref[...]
    row = lax.broadcasted_iota(jnp.int32, (SUBLANES, 1), 0)
    acc = b_ref[...] + w_ref[width - 1:width, :] * x
    for s in range(1, width):
        delayed = pltpu.roll(x, s, axis=0)
        head = jnp.where(row < s, pltpu.roll(tail, s, axis=0), delayed[0:SUBLANES])
        delayed = jnp.concatenate([head, delayed[SUBLANES:]], axis=0)
        acc = acc + w_ref[width - 1 - s:width - s, :] * delayed
    tail_ref[...] = x[tm - SUBLANES:tm]
    return acc


def _ffn_body(x_ref, g_ref, wg_ref, wu_ref, wd_ref, fin_ref, o_ref, act_sc, *, final_norm):
    x = x_ref[...]
    h = _rms_norm(x, g_ref[...], NORM_EPS).astype(BF16)
    for j in range(D_FF // FFN_TF):
        cols = slice(j * FFN_TF, (j + 1) * FFN_TF)
        gate = jnp.dot(h, wg_ref[:, cols].astype(BF16), preferred_element_type=F32)
        up = jnp.dot(h, wu_ref[:, cols].astype(BF16), preferred_element_type=F32)
        act_sc[:, cols] = (_silu(gate) * up).astype(BF16)
    down = None
    for lo, hi in zip(FFN_TK[:-1], FFN_TK[1:]):
        part = jnp.dot(act_sc[:, lo:hi], wd_ref[lo:hi, :].astype(BF16), preferred_element_type=F32)
        down = part if down is None else down + part
    y = x + 0.5 * down
    if final_norm:
        y = _rms_norm(y, fin_ref[...], NORM_EPS)
    o_ref[...] = y


def _resident_spec(shape):
    nd = len(shape)
    return pl.BlockSpec(shape, lambda i, _nd=nd: (0,) * _nd, pipeline_mode=pl.Buffered(1))


def _next_tile_spec(tm, d, n_tiles):
    return pl.BlockSpec((tm, d), lambda i: (jnp.minimum(2 * i + 2, n_tiles - 1), 0))


def _layer_spec(shape, layer):
    nd = len(shape)
    return pl.BlockSpec((None,) + tuple(shape), lambda i, _nd=nd: (layer,) + (0,) * _nd,
                        pipeline_mode=pl.Buffered(1))


def _ffn(x, norm_g, wg, wu, wd, layer, fin_g, *, final_norm):
    t_len, d = x.shape
    tm = min(FFN_TM, t_len)
    return pl.pallas_call(
        functools.partial(_ffn_body, final_norm=final_norm),
        grid=(t_len // tm,),
        in_specs=[
            pl.BlockSpec((tm, d), lambda i: (i, 0)),
            _resident_spec((1, d)),
            _layer_spec((d, D_FF), layer),
            _layer_spec((d, D_FF), layer),
            _layer_spec((D_FF, d), layer),
            _resident_spec((1, d)),
        ],
        out_specs=pl.BlockSpec((tm, d), lambda i: (i, 0)),
        out_shape=jax.ShapeDtypeStruct((t_len, d), F32),
        scratch_shapes=[pltpu.VMEM((tm, D_FF), BF16)],
        compiler_params=pltpu.CompilerParams(
            dimension_semantics=("arbitrary",), vmem_limit_bytes=VMEM_LIMIT),
        name="ffn_final" if final_norm else "ffn",
    )(x, norm_g.reshape(1, d), wg, wu, wd, fin_g.reshape(1, d))


def _cast_weight(dst_sc, src_ref):
    for lo in range(0, src_ref.shape[1], MXU_DIM):
        hi = min(lo + MXU_DIM, src_ref.shape[1])
        dst_sc[:, lo:hi] = src_ref[:, lo:hi].astype(BF16)


def _projection_steps(x, p_ref, g_ref, w_sc, tail_ref):
    h = _rms_norm(x, g_ref[...], NORM_EPS).astype(BF16)
    main = w_sc.shape[1]

    def section(lo, hi):
        p_ref[:, lo:hi] = jnp.dot(h, w_sc[:, lo:hi], preferred_element_type=F32)

    def tail():
        p_ref[:, main:main + tail_ref.shape[1]] = jnp.dot(h, tail_ref[...], preferred_element_type=F32)

    steps = [functools.partial(section, lo, min(lo + PROJ_TN, main)) for lo in range(0, main, PROJ_TN)]
    return steps + [tail]


def _run_next(steps):
    if steps:
        steps.pop(0)()


def _run_all(steps):
    while steps:
        _run_next(steps)


def _ab_mix(p_ref, x, rota, rotb_ref, rdec_ref, rqfs_ref, rkte_ref, rcd_ref,
            cw_ref, cb_ref, dtb_ref, alog_ref, dexp_ref, sng_ref, exp_ref, tri_ref,
            sret_sc, sssm_sc, ext_sc, woutb_sc, tm, side):
    def proj(lo, width):
        return p_ref[:, lo:lo + width]

    cos_a, sin_a = rota[0:1, :], rota[1:2, :]
    sin_a_sgn, cos_a_sgn = rota[2:3, :], rota[3:4, :]
    cos_b, sin_b = rotb_ref[:, 0:RET_DK], rotb_ref[:, RET_DK:2 * RET_DK]
    cos2 = cos_a * cos_b - sin_a * sin_b
    sin2 = sin_a_sgn * cos_b + cos_a_sgn * sin_b

    q_all = proj(AB_Q, MIX_W)
    k_all = proj(AB_K, MIX_W)
    v_all = proj(AB_V, MIX_W)
    gate_ret = _silu(proj(AB_G, MIX_W))
    scale = RET_DK ** -0.5
    c_len = RET_CHUNK
    ret_rows = []
    for c in range(tm // c_len):
        rs = slice(c * c_len, (c + 1) * c_len)
        heads = []
        for hh in range(RET_HEADS):
            ls = slice(hh * RET_DK, (hh + 1) * RET_DK)
            qh = q_all[rs, ls] * cos2[rs] + pltpu.roll(q_all[rs, ls], RET_DK // 2, axis=1) * sin2[rs]
            kh = (k_all[rs, ls] * cos2[rs] + pltpu.roll(k_all[rs, ls], RET_DK // 2, axis=1) * sin2[rs]) * scale
            vh = v_all[rs, ls]
            s_prev = sret_sc[hh]
            scores = _bdot_nt(qh, kh) * rdec_ref[hh]
            o = _bdot(jnp.concatenate([scores, qh * rqfs_ref[:, ls]], axis=1),
                      jnp.concatenate([vh, s_prev], axis=0))
            sret_sc[hh] = s_prev * rcd_ref[:, ls] + _bdot_tn(kh * rkte_ref[:, ls], vh)
            o = o * lax.rsqrt(jnp.mean(o * o, axis=-1, keepdims=True) + HEAD_NORM_EPS)
            heads.append(o)
            _run_next(side)
        ret_rows.append(jnp.concatenate(heads, axis=1))
    y_ret = gate_ret * jnp.concatenate(ret_rows, axis=0)

    z = proj(AB_Z, MIX_W)
    xbc = _silu(_causal_conv(ext_sc, proj(AB_XBC, SSM_XBC), cw_ref, cb_ref, tm, SSM_CONV))
    xs = xbc[:, :MIX_W]
    bm = xbc[:, MIX_W:MIX_W + LANES]
    cm = xbc[:, MIX_W + LANES:MIX_W + 2 * LANES]
    lane = lax.broadcasted_iota(jnp.int32, (1, LANES), 1)
    dt = _softplus(proj(AB_DT, LANES) + dtb_ref[...])
    a_neg = jnp.where(lane < SSM_HEADS, -jnp.exp(alog_ref[...]), 0.0)
    adt = dt * a_neg
    xdt = xs * _sel_dot_lhs(dt, exp_ref[0:2 * LANES, :], 2)
    acs_all = _sel_dot_rhs(tri_ref[...], adt, 3)
    acs_e_all = _sel_dot_lhs(acs_all, exp_ref[...], 3)
    l_len = SSM_CHUNK
    row = lax.broadcasted_iota(jnp.int32, (l_len, l_len), 0)
    col = lax.broadcasted_iota(jnp.int32, (l_len, l_len), 1)
    causal = row >= col
    first_half = lax.broadcasted_iota(jnp.int32, (1, LANES), 1) < SSM_HEAD_DIM
    grp_mask = [first_half, jnp.logical_not(first_half)]
    srow = lax.broadcasted_iota(jnp.int32, (LANES, MIX_W), 0)
    scol = lax.broadcasted_iota(jnp.int32, (LANES, MIX_W), 1)
    state_mask = (srow // SSM_STATE) == (scol // (MIX_W // SSM_GROUPS))
    ssm_rows = []
    for c in range(tm // l_len):
        rs = slice(c * l_len, (c + 1) * l_len)
        acs = acs_all[rs]
        acs_e = acs_e_all[rs]
        acs_t = acs.T
        last_e = acs_e[l_len - 1:l_len, :]
        bm_c = bm[rs]
        cm_c = cm[rs]
        s_prev = sssm_sc[...]
        y_off = _bdot(cm_c, s_prev) * jnp.exp(acs_e)
        cb = [_bdot_nt(jnp.where(grp_mask[g], cm_c, 0.0), bm_c) for g in range(SSM_GROUPS)]
        tiles = []
        for j in range(MIX_W // LANES):
            xt = xdt[rs, j * LANES:(j + 1) * LANES]
            outs = []
            for hh in (2 * j, 2 * j + 1):
                seg = acs[:, hh:hh + 1] - acs_t[hh:hh + 1, :]
                dec = jnp.where(causal, jnp.exp(jnp.where(causal, seg, 0.0)), 0.0)
                outs.append(_bdot(cb[hh // (SSM_HEADS // SSM_GROUPS)] * dec, xt))
            tiles.append(jnp.where(first_half, outs[0], outs[1]))
        y_diag = jnp.concatenate(tiles, axis=1)
        upd = _bdot_tn(bm_c, xdt[rs] * jnp.exp(last_e - acs_e))
        sssm_sc[...] = s_prev * jnp.exp(last_e) + jnp.where(state_mask, upd, 0.0)
        y = y_diag + y_off + dexp_ref[...] * xs[rs]
        y = y * _silu(z[rs])
        halves = []
        for g in range(SSM_GROUPS):
            yg = y[:, g * (MIX_W // SSM_GROUPS):(g + 1) * (MIX_W // SSM_GROUPS)]
            halves.append(yg * lax.rsqrt(jnp.mean(yg * yg, axis=-1, keepdims=True) + SSM_NORM_EPS))
        ssm_rows.append(jnp.concatenate(halves, axis=1) * sng_ref[...])
    y_ssm = jnp.concatenate(ssm_rows, axis=0)

    _run_all(side)
    return x + _bdot(jnp.concatenate([y_ret, y_ssm], axis=1), woutb_sc[...])


def _ab_body(x_ref, xn_ref, rota_ref, rotb_ref, g_ref, win_ref, wdt_ref, rdec_ref, rqfs_ref, rkte_ref, rcd_ref,
             cw_ref, cb_ref, dtb_ref, alog_ref, dexp_ref, sng_ref, exp_ref, tri_ref, wout_ref,
             o_ref, sret_sc, sssm_sc, ext_sc, winb_sc, woutb_sc, pa_sc, pb_sc, *, tm):
    steps = functools.partial(_projection_steps, g_ref=g_ref, w_sc=winb_sc, tail_ref=wdt_ref)

    @pl.when(pl.program_id(0) == 0)
    def _():
        sret_sc[...] = jnp.zeros_like(sret_sc)
        sssm_sc[...] = jnp.zeros_like(sssm_sc)
        ext_sc[...] = jnp.zeros_like(ext_sc)
        _cast_weight(winb_sc, win_ref)
        _cast_weight(woutb_sc, wout_ref)
        _run_all(steps(x_ref[0:tm, :], pa_sc))

    mix = functools.partial(
        _ab_mix, rotb_ref=rotb_ref, rdec_ref=rdec_ref, rqfs_ref=rqfs_ref, rkte_ref=rkte_ref, rcd_ref=rcd_ref,
        cw_ref=cw_ref, cb_ref=cb_ref, dtb_ref=dtb_ref, alog_ref=alog_ref, dexp_ref=dexp_ref, sng_ref=sng_ref,
        exp_ref=exp_ref, tri_ref=tri_ref, sret_sc=sret_sc, sssm_sc=sssm_sc, ext_sc=ext_sc, woutb_sc=woutb_sc,
        tm=tm)
    o_ref[0:tm, :] = mix(pa_sc, x_ref[0:tm, :], rota_ref[0], side=steps(x_ref[tm:2 * tm, :], pb_sc))
    o_ref[tm:2 * tm, :] = mix(pb_sc, x_ref[tm:2 * tm, :], rota_ref[1], side=steps(xn_ref[...], pa_sc))


def _rotary_tables(n_tiles, tm):
    inv_freq = ROPE_BASE ** (-np.arange(0, RET_DK, 2, dtype=np.float64) / RET_DK)
    two = lambda t: np.concatenate([t, t], axis=-1)
    sign = np.concatenate([-np.ones(RET_DK // 2), np.ones(RET_DK // 2)])
    ang_a = (np.arange(n_tiles, dtype=np.float64) * tm)[:, None] * inv_freq[None, :]
    ang_b = np.arange(tm, dtype=np.float64)[:, None] * inv_freq[None, :]
    cos_a, sin_a = two(np.cos(ang_a)), two(np.sin(ang_a))
    rot_a = np.zeros((n_tiles, SUBLANES, RET_DK))
    rot_a[:, 0], rot_a[:, 1], rot_a[:, 2], rot_a[:, 3] = cos_a, sin_a, sign * sin_a, sign * cos_a
    rot_b = np.concatenate([two(np.cos(ang_b)), two(np.sin(ang_b))], axis=1)
    return jnp.asarray(rot_a, F32), jnp.asarray(rot_b, F32)


def _retention_tables():
    c = RET_CHUNK
    log_g = np.log1p(-(2.0 ** (-5.0 - np.arange(RET_HEADS, dtype=np.float64))))
    pos = np.arange(c, dtype=np.float64)
    rel = pos[:, None] - pos[None, :]
    rdec = np.where(rel >= 0, np.exp(np.maximum(rel, 0.0)[None] * log_g[:, None, None]), 0.0)
    per_head = lambda t: np.repeat(t, RET_DK, axis=1)
    rkte = per_head(np.exp((c - 1 - pos)[:, None] * log_g[None, :]))
    rqfs = per_head(np.exp((pos + 1.0)[:, None] * log_g[None, :]))
    rcd = per_head(np.exp(c * log_g)[None, :])
    return tuple(jnp.asarray(t, F32) for t in (rdec, rqfs, rkte, rcd))


def _mix_ab(x, norm_g, w_in_all, w_out_all, j, conv_w, conv_b, dt_bias, a_log, d_skip, norm_ssm):
    t_len, d = x.shape
    tm = min(MIX_TM, t_len)
    rot_a, rot_b = _rotary_tables(t_len // tm, tm)
    rdec, rqfs, rkte, rcd = _retention_tables()

    w_dt = jnp.pad(w_in_all[j][:, AB_DT:], ((0, 0), (0, LANES - SSM_HEADS))).astype(BF16)
    pad_row = lambda v: jnp.pad(v.astype(F32), (0, LANES - v.shape[0])).reshape(1, LANES)
    expand = (jnp.arange(LANES)[:, None] == (jnp.arange(MIX_W)[None, :] // SSM_HEAD_DIM)).astype(BF16)
    expand = jnp.concatenate([expand] * 3, axis=0)
    t_idx = jnp.arange(tm)
    tri = (((t_idx[:, None] // SSM_CHUNK) == (t_idx[None, :] // SSM_CHUNK)) &
           (t_idx[:, None] >= t_idx[None, :])).astype(BF16)

    operands = [
        (x, pl.BlockSpec((2 * tm, d), lambda i: (i, 0))),
        (x, _next_tile_spec(tm, d, t_len // tm)),
        (rot_a, pl.BlockSpec((2, SUBLANES, RET_DK), lambda i: (i, 0, 0))),
        (rot_b, None),
        (norm_g.reshape(1, d), None),
        (w_in_all, _layer_spec((d, AB_DT), j)),
        (w_dt, None),
        (rdec, None), (rqfs, None), (rkte, None), (rcd, None),
        (conv_w, None), (conv_b.reshape(1, SSM_XBC), None),
        (pad_row(dt_bias), None), (pad_row(a_log), None),
        (jnp.repeat(d_skip.astype(F32), SSM_HEAD_DIM).reshape(1, MIX_W), None),
        (norm_ssm.reshape(1, MIX_W), None),
        (expand, None), (tri, None),
        (w_out_all, _layer_spec((2 * MIX_W, d), j)),
    ]
    args = [a for a, _ in operands]
    specs = [s if s is not None else _resident_spec(a.shape) for a, s in operands]
    return pl.pallas_call(
        functools.partial(_ab_body, tm=tm),
        grid=(t_len // (2 * tm),),
        in_specs=specs,
        out_specs=pl.BlockSpec((2 * tm, d), lambda i: (i, 0)),
        out_shape=jax.ShapeDtypeStruct((t_len, d), F32),
        scratch_shapes=[
            pltpu.VMEM((RET_HEADS, RET_DK, RET_DK), F32),
            pltpu.VMEM((LANES, MIX_W), F32),
            pltpu.VMEM((SUBLANES, SSM_XBC), F32),
            pltpu.VMEM((d, AB_DT), BF16),
            pltpu.VMEM((2 * MIX_W, d), BF16),
            pltpu.VMEM((tm, AB_DT + LANES), F32),
            pltpu.VMEM((tm, AB_DT + LANES), F32),
        ],
        compiler_params=pltpu.CompilerParams(
            dimension_semantics=("arbitrary",), vmem_limit_bytes=VMEM_LIMIT),
        name="mix_ret_ssd",
    )(*args)


def _head_sum(x, ones_blk):
    tiles = [_sel_dot_lhs(x[:, j * LANES:(j + 1) * LANES], ones_blk, 2) for j in range(x.shape[1] // LANES)]
    return jnp.concatenate(tiles, axis=1)


def _cd_mix(p_ref, x, lcw_ref, lcb_ref, wa_ref, ba_ref, wx_ref, bx_ref, lam_ref,
            mu_ref, w0_ref, w2_ref, a0_ref, a2_ref, g2_ref, kk_ref, ka_ref, rk_ref, lng_ref, lnb_ref,
            ones_ref, sel_ref, ext_sc, hcar_sc, pcar_sc, s_sc, woutb_sc, tm, side):
    def proj(lo, width):
        return p_ref[:, lo:lo + width]

    rows = lax.broadcasted_iota(jnp.int32, (tm, 1), 0)

    xc = _causal_conv(ext_sc, proj(CD_XB, MIX_W), lcw_ref, lcb_ref, tm, LRU_CONV)
    r_parts, i_parts = [], []
    for n in range(LRU_BLOCKS):
        ls = slice(n * LRU_BLOCK, (n + 1) * LRU_BLOCK)
        r_parts.append(jax.nn.sigmoid(_bdot(xc[:, ls], wa_ref[n]) + ba_ref[:, ls]))
        i_parts.append(jax.nn.sigmoid(_bdot(xc[:, ls], wx_ref[n]) + bx_ref[:, ls]))
    r_gate = jnp.concatenate(r_parts, axis=1)
    i_gate = jnp.concatenate(i_parts, axis=1)
    log_a = -LRU_C * r_gate * _softplus(-lam_ref[...])
    a_cum = jnp.exp(log_a)
    hs = _sqrt_nonneg(-jnp.tanh(log_a) * (1.0 + a_cum * a_cum)) * (i_gate * xc)
    in_group = rows % SUBLANES
    shift = 1
    while shift < SUBLANES:
        valid = in_group >= shift
        a_prev = jnp.where(valid, pltpu.roll(a_cum, shift, axis=0), 1.0)
        h_prev = jnp.where(valid, pltpu.roll(hs, shift, axis=0), 0.0)
        hs = hs + a_cum * h_prev
        a_cum = a_cum * a_prev
        shift *= 2
    carry = hcar_sc[...]
    groups = []
    for gi in range(tm // SUBLANES):
        grp = slice(gi * SUBLANES, (gi + 1) * SUBLANES)
        hg = hs[grp] + a_cum[grp] * carry
        groups.append(hg)
        carry = hg[SUBLANES - 1:SUBLANES, :]
    hs = jnp.concatenate(groups, axis=0)
    hcar_sc[...] = carry
    y_lru = hs * _gelu_tanh(proj(CD_GB, MIX_W))

    pr = proj(CD_PR, CD_PR_COLS)
    prev = pltpu.roll(pr, 1, axis=0)
    row8 = lax.broadcasted_iota(jnp.int32, (SUBLANES, 1), 0)
    prev = jnp.concatenate([jnp.where(row8 == 0, pcar_sc[...], prev[0:SUBLANES]), prev[SUBLANES:]], axis=0)
    pcar_sc[...] = pr[tm - 1:tm, :]
    ps = pr + (prev - pr) * mu_ref[...]
    r = ps[:, 0:MIX_W]
    k = ps[:, MIX_W:2 * MIX_W]
    v = ps[:, 2 * MIX_W:3 * MIX_W]
    wl = ps[:, 3 * MIX_W:3 * MIX_W + LANES]
    al = ps[:, 3 * MIX_W + LANES:3 * MIX_W + 2 * LANES]
    gl = ps[:, 3 * MIX_W + 2 * LANES:3 * MIX_W + 4 * LANES]
    log_w = -math.exp(-0.5) * jax.nn.sigmoid(w0_ref[...] + _bdot(jnp.tanh(wl), w2_ref[...]))
    a = jax.nn.sigmoid(a0_ref[...] + _bdot(al, a2_ref[...]))
    g = _bdot(jax.nn.sigmoid(gl), g2_ref[...])
    ones_blk = ones_ref[...]
    kk = k * kk_ref[...]
    kk = kk * lax.rsqrt(jnp.maximum(_head_sum(kk * kk, ones_blk), 1e-24))
    k = k * (1.0 + (a - 1.0) * ka_ref[...])
    c_len = RWKV_CHUNK
    pair = 2 * c_len
    n_chunks = tm // c_len
    n_pairs = MIX_W // LANES

    res = _sel_dot_rhs(sel_ref[...], log_w, 3)
    cs = res[0:tm]
    tot = res[tm:2 * tm]
    e_neg = jnp.exp(-cs)
    e_end = jnp.exp(tot - cs)
    w_end = jnp.exp(tot)
    nb = -(kk * a)
    bt = kk * jnp.exp(cs - log_w)
    rt = r * jnp.exp(cs)
    ab = nb * e_neg
    kb = k * e_neg
    ae = nb * e_end
    ke = k * e_end

    lane = lax.broadcasted_iota(jnp.int32, (1, LANES), 1)
    m0 = lane < RWKV_HEAD_DIM
    trow = lax.broadcasted_iota(jnp.int32, (c_len, LANES), 0)
    scol = lax.broadcasted_iota(jnp.int32, (c_len, LANES), 1) % c_len
    strict = trow > scol
    incl = trow >= scol
    eye = (trow == scol).astype(F32)

    def merge_mask(s):
        return ((trow // s) == (scol // s) + 1) & ((trow // s) % 2 == 1)

    def blockdiag(t):
        return jnp.concatenate([jnp.where(m0, t, 0.0), jnp.where(m0, 0.0, t)], axis=0)

    def tile(t, c, j):
        return t[c * c_len:(c + 1) * c_len, j * LANES:(j + 1) * LANES]

    units = [(c, j) for c in range(n_chunks) for j in range(n_pairs)]
    vbd = {u: blockdiag(tile(v, *u)).astype(BF16) for u in units}
    n_mat, lhs_x, lhs_y, lhs_z, w_col = {}, {}, {}, {}, {}
    for u in units:
        bt_u = tile(bt, *u).astype(BF16)
        rt_u = tile(rt, *u).astype(BF16)
        gram = _bdot_nt(jnp.concatenate([bt_u, rt_u], axis=0),
                        jnp.concatenate([blockdiag(tile(ab, *u)), blockdiag(tile(kb, *u))], axis=0))
        n_mat[u] = jnp.where(strict, gram[0:c_len, 0:LANES], 0.0)
        lhs_x[u] = jnp.concatenate(
            [bt_u, jnp.where(strict, gram[0:c_len, LANES:2 * LANES], 0.0).astype(BF16)], axis=1)
        lhs_y[u] = jnp.concatenate(
            [rt_u, jnp.where(incl, gram[c_len:pair, 0:LANES], 0.0).astype(BF16),
             jnp.where(incl, gram[c_len:pair, LANES:2 * LANES], 0.0).astype(BF16)], axis=1)
        lhs_z[u] = jnp.concatenate(
            [blockdiag(tile(ae, *u)).T.astype(BF16), blockdiag(tile(ke, *u)).T.astype(BF16)], axis=1)
        w_row = tile(w_end, *u)
        w_col[u] = jnp.concatenate([w_row, w_row], axis=0).T

    t_inv = {u: eye + jnp.where(merge_mask(1), n_mat[u], 0.0) for u in units}
    s = 2
    while s < c_len:
        mask = merge_mask(s)
        half = {u: _bdot(t_inv[u], blockdiag(jnp.where(mask, n_mat[u], 0.0))) for u in units}
        t_inv = {u: t_inv[u] + _bdot(half[u], blockdiag(t_inv[u])) for u in units}
        s *= 2
    t_inv = {u: t_inv[u].astype(BF16) for u in units}

    y_rows = []
    for c in range(n_chunks):
        z_prev = [s_sc[j] for j in range(n_pairs)]
        zb = [z.astype(BF16) for z in z_prev]
        xs = [jnp.dot(lhs_x[(c, j)], jnp.concatenate([zb[j], vbd[(c, j)]], axis=0),
                      preferred_element_type=F32) for j in range(n_pairs)]
        _run_next(side)
        us = [jnp.dot(t_inv[(c, j)], blockdiag(xs[j]).astype(BF16), preferred_element_type=F32)
              for j in range(n_pairs)]
        _run_next(side)
        us = [blockdiag(u).astype(BF16) for u in us]
        tiles = []
        for j in range(n_pairs):
            tiles.append(jnp.dot(lhs_y[(c, j)], jnp.concatenate([zb[j], us[j], vbd[(c, j)]], axis=0),
                                 preferred_element_type=F32))
            s_sc[j] = z_prev[j] * w_col[(c, j)] + jnp.dot(
                lhs_z[(c, j)], jnp.concatenate([us[j], vbd[(c, j)]], axis=0), preferred_element_type=F32)
        y_rows.append(jnp.concatenate(tiles, axis=1))
    y = jnp.concatenate(y_rows, axis=0)
    inv_n = 1.0 / RWKV_HEAD_DIM
    mean = _head_sum(y, ones_blk) * inv_n
    yc = y - mean
    var = _head_sum(yc * yc, ones_blk) * inv_n
    y = yc * lax.rsqrt(var + RWKV_LN_EPS) * lng_ref[...] + lnb_ref[...]
    y = y + _head_sum(r * k * rk_ref[...], ones_blk) * v
    y_rwkv = y * g

    _run_all(side)
    return x + _bdot(jnp.concatenate([y_lru, y_rwkv], axis=1), woutb_sc[...])


def _cd_body(x_ref, xn_ref, g_ref, win_ref, lcw_ref, lcb_ref, wa_ref, ba_ref, wx_ref, bx_ref, lam_ref,
             mu_ref, w0_ref, w2_ref, a0_ref, a2_ref, g2_ref, kk_ref, ka_ref, rk_ref, lng_ref, lnb_ref,
             ones_ref, sel_ref, wlora_ref, wout_ref, o_ref,
             ext_sc, hcar_sc, pcar_sc, s_sc, winb_sc, woutb_sc, pa_sc, pb_sc, *, tm):
    steps = functools.partial(_projection_steps, g_ref=g_ref, w_sc=winb_sc, tail_ref=wlora_ref)

    @pl.when(pl.program_id(0) == 0)
    def _():
        ext_sc[...] = jnp.zeros_like(ext_sc)
        hcar_sc[...] = jnp.zeros_like(hcar_sc)
        pcar_sc[...] = jnp.zeros_like(pcar_sc)
        s_sc[...] = jnp.zeros_like(s_sc)
        _cast_weight(winb_sc, win_ref)
        _cast_weight(woutb_sc, wout_ref)
        _run_all(steps(x_ref[0:tm, :], pa_sc))

    mix = functools.partial(
        _cd_mix, lcw_ref=lcw_ref, lcb_ref=lcb_ref, wa_ref=wa_ref, ba_ref=ba_ref, wx_ref=wx_ref, bx_ref=bx_ref,
        lam_ref=lam_ref, mu_ref=mu_ref, w0_ref=w0_ref, w2_ref=w2_ref, a0_ref=a0_ref, a2_ref=a2_ref, g2_ref=g2_ref,
        kk_ref=kk_ref, ka_ref=ka_ref, rk_ref=rk_ref, lng_ref=lng_ref, lnb_ref=lnb_ref, ones_ref=ones_ref,
        sel_ref=sel_ref, ext_sc=ext_sc, hcar_sc=hcar_sc, pcar_sc=pcar_sc, s_sc=s_sc, woutb_sc=woutb_sc, tm=tm)
    o_ref[0:tm, :] = mix(pa_sc, x_ref[0:tm, :], side=steps(x_ref[tm:2 * tm, :], pb_sc))
    o_ref[tm:2 * tm, :] = mix(pb_sc, x_ref[tm:2 * tm, :], side=steps(xn_ref[...], pa_sc))


def _mix_cd(x, norm_g, w_in_all, w_out_all, j, lru_conv_w, lru_conv_b, lru_wa, lru_ba, lru_wx, lru_bx,
            lru_lambda, mu, w0, w2, a0, a2, g2, k_k, k_a, r_k, ln_g, ln_b):
    t_len, d = x.shape
    tm = min(MIX_TM, t_len)

    def pad_lora(t, lo, width, padded):
        return jnp.pad(t[..., lo:lo + width], [(0, 0)] * (t.ndim - 1) + [(0, padded - width)])

    def lora_cols(t, base):
        return jnp.concatenate([
            pad_lora(t, base, DECAY_LORA, LANES),
            pad_lora(t, base + DECAY_LORA, ICL_LORA, LANES),
            pad_lora(t, base + DECAY_LORA + ICL_LORA, GATE_LORA, 2 * LANES)], axis=-1)

    w_lora = lora_cols(w_in_all[j], CD_LORA).astype(BF16)
    mu_row = mu.reshape(1, -1)
    mu_pad = jnp.concatenate([mu_row[:, :3 * MIX_W], lora_cols(mu_row, 3 * MIX_W)], axis=-1)
    pad_rows = lambda t, n: jnp.pad(t, ((0, n - t.shape[0]), (0, 0))).astype(BF16)
    row = lambda t: t.reshape(1, -1).astype(F32)
    ones_blk = ((jnp.arange(LANES)[:, None] // RWKV_HEAD_DIM) ==
                (jnp.arange(LANES)[None, :] // RWKV_HEAD_DIM)).astype(BF16)
    ones_blk = jnp.concatenate([ones_blk] * 2, axis=0)
    t_idx = jnp.arange(tm)
    same_chunk = (t_idx[:, None] // RWKV_CHUNK) == (t_idx[None, :] // RWKV_CHUNK)
    sel = jnp.concatenate([same_chunk & (t_idx[:, None] >= t_idx[None, :]), same_chunk], axis=0).astype(BF16)

    operands = [
        (x, pl.BlockSpec((2 * tm, d), lambda i: (i, 0))),
        (x, _next_tile_spec(tm, d, t_len // tm)),
        (row(norm_g), None),
        (w_in_all, _layer_spec((d, CD_LORA), j)),
        (lru_conv_w, None), (row(lru_conv_b), None),
        (lru_wa.astype(BF16), None), (row(lru_ba), None),
        (lru_wx.astype(BF16), None), (row(lru_bx), None),
        (row(lru_lambda), None),
        (mu_pad, None),
        (row(w0), None), (pad_rows(w2, LANES), None),
        (row(a0), None), (pad_rows(a2, LANES), None),
        (pad_rows(g2, 2 * LANES), None),
        (row(k_k), None), (row(k_a), None), (row(r_k), None), (row(ln_g), None), (row(ln_b), None),
        (ones_blk, None), (sel, None),
        (w_lora, None),
        (w_out_all, _layer_spec((2 * MIX_W, d), j)),
    ]
    args = [a for a, _ in operands]
    specs = [s if s is not None else _resident_spec(a.shape) for a, s in operands]
    return pl.pallas_call(
        functools.partial(_cd_body, tm=tm),
        grid=(t_len // (2 * tm),),
        in_specs=specs,
        out_specs=pl.BlockSpec((2 * tm, d), lambda i: (i, 0)),
        out_shape=jax.ShapeDtypeStruct((t_len, d), F32),
        scratch_shapes=[
            pltpu.VMEM((SUBLANES, MIX_W), F32),
            pltpu.VMEM((1, MIX_W), F32),
            pltpu.VMEM((1, CD_PR_COLS), F32),
            pltpu.VMEM((MIX_W // LANES, LANES, LANES), F32),
            pltpu.VMEM((d, CD_LORA), BF16),
            pltpu.VMEM((2 * MIX_W, d), BF16),
            pltpu.VMEM((tm, CD_PR + CD_PR_COLS), F32),
            pltpu.VMEM((tm, CD_PR + CD_PR_COLS), F32),
        ],
        compiler_params=pltpu.CompilerParams(
            dimension_semantics=("arbitrary",), vmem_limit_bytes=VMEM_LIMIT),
        name="mix_lru_rwkv",
    )(*args)


def kernel(x, ffn1_norm, ffn1_wg, ffn1_wu, ffn1_wd, mix_norm, ffn2_norm, ffn2_wg, ffn2_wu, ffn2_wd,
           ab_w_in, ab_w_out, ssm_conv_w, ssm_conv_b, ssm_dt_bias, ssm_a_log, ssm_d, ssm_norm,
           cd_w_in, cd_w_out, lru_conv_w, lru_conv_b, lru_wa, lru_ba, lru_wx, lru_bx, lru_lambda,
           rwkv_mu, rwkv_w0, rwkv_w2, rwkv_a0, rwkv_a2, rwkv_g2, rwkv_kk, rwkv_ka, rwkv_rk,
           rwkv_ln_g, rwkv_ln_b, final_norm):
    bsz, t_len, d = x.shape
    depth = ffn1_norm.shape[0]
    outs = []
    for b in range(bsz):
        xb = x.reshape(t_len, d) if bsz == 1 else x[b]
        for layer in range(depth):
            j = layer // 2
            xb = _ffn(xb, ffn1_norm[layer], ffn1_wg, ffn1_wu, ffn1_wd, layer, final_norm, final_norm=False)
            if layer % 2 == 0:
                xb = _mix_ab(xb, mix_norm[layer], ab_w_in, ab_w_out, j, ssm_conv_w[j], ssm_conv_b[j],
                             ssm_dt_bias[j], ssm_a_log[j], ssm_d[j], ssm_norm[j])
            else:
                xb = _mix_cd(xb, mix_norm[layer], cd_w_in, cd_w_out, j, lru_conv_w[j], lru_conv_b[j], lru_wa[j],
                             lru_ba[j], lru_wx[j], lru_bx[j], lru_lambda[j], rwkv_mu[j], rwkv_w0[j],
                             rwkv_w2[j], rwkv_a0[j], rwkv_a2[j], rwkv_g2[j], rwkv_kk[j], rwkv_ka[j],
                             rwkv_rk[j].reshape(-1), rwkv_ln_g[j], rwkv_ln_b[j])
            xb = _ffn(xb, ffn2_norm[layer], ffn2_wg, ffn2_wu, ffn2_wd, layer, final_norm,
                      final_norm=(layer == depth - 1))
        outs.append(xb)
    return outs[0].reshape(1, t_len, d) if bsz == 1 else jnp.stack(outs, axis=0)
```

```python
import functools
import math

import jax
import jax.numpy as jnp
import numpy as np
from jax import lax
from jax.experimental import pallas as pl
from jax.experimental.pallas import tpu as pltpu

F32 = jnp.float32
BF16 = jnp.bfloat16

D_MODEL = 1024
D_FF = 2816
NORM_EPS = 1e-6
MIX_W = 512

RET_HEADS = 4
RET_DK = 128
RET_CHUNK = 128
ROPE_BASE = 10000.0
HEAD_NORM_EPS = 1e-6

SSM_HEADS = 8
SSM_HEAD_DIM = 64
SSM_GROUPS = 2
SSM_STATE = 64
SSM_CONV = 4
SSM_CHUNK = 128
SSM_XBC = 768
SSM_NORM_EPS = 1e-5

LRU_BLOCKS = 4
LRU_BLOCK = 128
LRU_CONV = 4
LRU_C = 8.0

RWKV_HEADS = 8
RWKV_HEAD_DIM = 64
DECAY_LORA = 64
ICL_LORA = 64
GATE_LORA = 160
RWKV_LN_EPS = 64e-5
RWKV_CHUNK = 64

LANES = 128
SUBLANES = 8
VMEM_LIMIT = 56 * 1024 * 1024

FFN_TM = 512
FFN_TF = 256
MXU_DIM = 256
FFN_TK = (0, 6 * MXU_DIM, D_FF)
MIX_TM = 256
PROJ_TN = 512

AB_Q, AB_K, AB_V, AB_G, AB_Z, AB_XBC, AB_DT = 0, 512, 1024, 1536, 2048, 2560, 3328
CD_XB, CD_GB, CD_PR, CD_LORA = 0, 512, 1024, 2560
CD_PR_COLS = 2048


def _bdot(a, b):
    return jnp.dot(a.astype(BF16), b.astype(BF16), preferred_element_type=F32)


def _bdot_nt(a, b):
    return lax.dot_general(a.astype(BF16), b.astype(BF16), (((1,), (1,)), ((), ())),
                           preferred_element_type=F32)


def _bdot_tn(a, b):
    return lax.dot_general(a.astype(BF16), b.astype(BF16), (((0,), (0,)), ((), ())),
                           preferred_element_type=F32)


def _split(x, pieces):
    out = []
    for _ in range(pieces - 1):
        p = x.astype(BF16)
        out.append(p)
        x = x - p.astype(F32)
    out.append(x.astype(BF16))
    return out


def _sel_dot_lhs(x, sel_stack, pieces):
    return jnp.dot(jnp.concatenate(_split(x, pieces), axis=1), sel_stack, preferred_element_type=F32)


def _sel_dot_rhs(sel, x, pieces):
    n = x.shape[1]
    res = jnp.dot(sel, jnp.concatenate(_split(x, pieces), axis=1), preferred_element_type=F32)
    out = res[:, 0:n]
    for p in range(1, pieces):
        out = out + res[:, p * n:(p + 1) * n]
    return out


def _rms_norm(x, g, eps):
    return x * lax.rsqrt(jnp.mean(x * x, axis=-1, keepdims=True) + eps) * g


def _silu(x):
    return x * jax.nn.sigmoid(x)


def _softplus(x):
    return jnp.maximum(x, 0.0) + jnp.log1p(jnp.exp(-jnp.abs(x)))


def _sqrt_nonneg(y):
    return y * lax.rsqrt(jnp.maximum(y, jnp.finfo(jnp.float32).tiny))


def _gelu_tanh(x):
    c = math.sqrt(2.0 / math.pi)
    return 0.5 * x * (1.0 + jnp.tanh(c * (x + 0.044715 * (x * x * x))))


def _causal_conv(tail_ref, x, w_ref, b_ref, tm, width):
    tail = tail_ref[...]
    row = lax.broadcasted_iota(jnp.int32, (SUBLANES, 1), 0)
    acc = b_ref[...] + w_ref[width - 1:width, :] * x
    for s in range(1, width):
        delayed = pltpu.roll(x, s, axis=0)
        head = jnp.where(row < s, pltpu.roll(tail, s, axis=0), delayed[0:SUBLANES])
        delayed = jnp.concatenate([head, delayed[SUBLANES:]], axis=0)
        acc = acc + w_ref[width - 1 - s:width - s, :] * delayed
    tail_ref[...] = x[tm - SUBLANES:tm]
    return acc


def _ffn_body(x_ref, g_ref, wg_ref, wu_ref, wd_ref, fin_ref, o_ref, act_sc, *, final_norm):
    x = x_ref[...]
    h = _rms_norm(x, g_ref[...], NORM_EPS).astype(BF16)
    for j in range(D_FF // FFN_TF):
        cols = slice(j * FFN_TF, (j + 1) * FFN_TF)
        gate = jnp.dot(h, wg_ref[:, cols].astype(BF16), preferred_element_type=F32)
        up = jnp.dot(h, wu_ref[:, cols].astype(BF16), preferred_element_type=F32)
        act_sc[:, cols] = (_silu(gate) * up).astype(BF16)
    down = None
    for lo, hi in zip(FFN_TK[:-1], FFN_TK[1:]):
        part = jnp.dot(act_sc[:, lo:hi], wd_ref[lo:hi, :].astype(BF16), preferred_element_type=F32)
        down = part if down is None else down + part
    y = x + 0.5 * down
    if final_norm:
        y = _rms_norm(y, fin_ref[...], NORM_EPS)
    o_ref[...] = y


def _resident_spec(shape):
    nd = len(shape)
    return pl.BlockSpec(shape, lambda i, _nd=nd: (0,) * _nd, pipeline_mode=pl.Buffered(1))


def _next_tile_spec(tm, d, n_tiles):
    return pl.BlockSpec((tm, d), lambda i: (jnp.minimum(2 * i + 2, n_tiles - 1), 0))


def _layer_spec(shape, layer):
    nd = len(shape)
    return pl.BlockSpec((None,) + tuple(shape), lambda i, _nd=nd: (layer,) + (0,) * _nd,
                        pipeline_mode=pl.Buffered(1))


def _ffn(x, norm_g, wg, wu, wd, layer, fin_g, *, final_norm):
    t_len, d = x.shape
    tm = min(FFN_TM, t_len)
    return pl.pallas_call(
        functools.partial(_ffn_body, final_norm=final_norm),
        grid=(t_len // tm,),
        in_specs=[
            pl.BlockSpec((tm, d), lambda i: (i, 0)),
            _resident_spec((1, d)),
            _layer_spec((d, D_FF), layer),
            _layer_spec((d, D_FF), layer),
            _layer_spec((D_FF, d), layer),
            _resident_spec((1, d)),
        ],
        out_specs=pl.BlockSpec((tm, d), lambda i: (i, 0)),
        out_shape=jax.ShapeDtypeStruct((t_len, d), F32),
        scratch_shapes=[pltpu.VMEM((tm, D_FF), BF16)],
        compiler_params=pltpu.CompilerParams(
            dimension_semantics=("arbitrary",), vmem_limit_bytes=VMEM_LIMIT),
        name="ffn_final" if final_norm else "ffn",
    )(x, norm_g.reshape(1, d), wg, wu, wd, fin_g.reshape(1, d))


def _cast_weight(dst_sc, src_ref):
    for lo in range(0, src_ref.shape[1], MXU_DIM):
        hi = min(lo + MXU_DIM, src_ref.shape[1])
        dst_sc[:, lo:hi] = src_ref[:, lo:hi].astype(BF16)


def _projection_steps(x, p_ref, g_ref, w_sc, tail_ref):
    h = _rms_norm(x, g_ref[...], NORM_EPS).astype(BF16)
    main = w_sc.shape[1]

    def section(lo, hi):
        p_ref[:, lo:hi] = jnp.dot(h, w_sc[:, lo:hi], preferred_element_type=F32)

    def tail():
        p_ref[:, main:main + tail_ref.shape[1]] = jnp.dot(h, tail_ref[...], preferred_element_type=F32)

    steps = [functools.partial(section, lo, min(lo + PROJ_TN, main)) for lo in range(0, main, PROJ_TN)]
    return steps + [tail]


def _run_next(steps):
    if steps:
        steps.pop(0)()


def _run_all(steps):
    while steps:
        _run_next(steps)


def _ab_mix(p_ref, x, rota, rotb_ref, rdec_ref, rqfs_ref, rkte_ref, rcd_ref,
            cw_ref, cb_ref, dtb_ref, alog_ref, dexp_ref, sng_ref, exp_ref, tri_ref,
            sret_sc, sssm_sc, ext_sc, woutb_sc, tm, side):
    def proj(lo, width):
        return p_ref[:, lo:lo + width]

    cos_a, sin_a = rota[0:1, :], rota[1:2, :]
    sin_a_sgn, cos_a_sgn = rota[2:3, :], rota[3:4, :]
    cos_b, sin_b = rotb_ref[:, 0:RET_DK], rotb_ref[:, RET_DK:2 * RET_DK]
    cos2 = cos_a * cos_b - sin_a * sin_b
    sin2 = sin_a_sgn * cos_b + cos_a_sgn * sin_b

    q_all = proj(AB_Q, MIX_W)
    k_all = proj(AB_K, MIX_W)
    v_all = proj(AB_V, MIX_W)
    gate_ret = _silu(proj(AB_G, MIX_W))
    scale = RET_DK ** -0.5
    c_len = RET_CHUNK
    n_chunks = tm // c_len
    cells = [(c, hh) for c in range(n_chunks) for hh in range(RET_HEADS)]
    rows_of = lambda c: slice(c * c_len, (c + 1) * c_len)
    lanes_of = lambda hh: slice(hh * RET_DK, (hh + 1) * RET_DK)

    def rotate(t, c, hh):
        t = t[rows_of(c), lanes_of(hh)]
        return t * cos2[rows_of(c)] + pltpu.roll(t, RET_DK // 2, axis=1) * sin2[rows_of(c)]

    qh = {u: rotate(q_all, *u) for u in cells}
    kh = {u: rotate(k_all, *u) * scale for u in cells}
    vh = {(c, hh): v_all[rows_of(c), lanes_of(hh)].astype(BF16) for c, hh in cells}
    scores = {(c, hh): _bdot_nt(qh[(c, hh)], kh[(c, hh)]) * rdec_ref[hh] for c, hh in cells}
    kv = {(c, hh): _bdot_tn(kh[(c, hh)] * rkte_ref[:, lanes_of(hh)], vh[(c, hh)]) for c, hh in cells}
    state = {}
    for hh in range(RET_HEADS):
        s_cur = sret_sc[hh]
        for c in range(n_chunks):
            state[(c, hh)] = s_cur
            s_cur = s_cur * rcd_ref[:, lanes_of(hh)] + kv[(c, hh)]
        sret_sc[hh] = s_cur
    outs = {(c, hh): _bdot(jnp.concatenate([scores[(c, hh)], qh[(c, hh)] * rqfs_ref[:, lanes_of(hh)]], axis=1),
                           jnp.concatenate([vh[(c, hh)], state[(c, hh)].astype(BF16)], axis=0))
            for c, hh in cells}
    outs = {u: o * lax.rsqrt(jnp.mean(o * o, axis=-1, keepdims=True) + HEAD_NORM_EPS) for u, o in outs.items()}
    y_ret = gate_ret * jnp.concatenate(
        [jnp.concatenate([outs[(c, hh)] for hh in range(RET_HEADS)], axis=1) for c in range(n_chunks)], axis=0)

    z = proj(AB_Z, MIX_W)
    xbc = _silu(_causal_conv(ext_sc, proj(AB_XBC, SSM_XBC), cw_ref, cb_ref, tm, SSM_CONV))
    xs = xbc[:, :MIX_W]
    bm = xbc[:, MIX_W:MIX_W + LANES]
    cm = xbc[:, MIX_W + LANES:MIX_W + 2 * LANES]
    lane = lax.broadcasted_iota(jnp.int32, (1, LANES), 1)
    dt = _softplus(proj(AB_DT, LANES) + dtb_ref[...])
    a_neg = jnp.where(lane < SSM_HEADS, -jnp.exp(alog_ref[...]), 0.0)
    adt = dt * a_neg
    xdt = xs * _sel_dot_lhs(dt, exp_ref[0:2 * LANES, :], 2)
    acs_all = _sel_dot_rhs(tri_ref[...], adt, 3)
    acs_e_all = _sel_dot_lhs(acs_all, exp_ref[...], 3)
    l_len = SSM_CHUNK
    row = lax.broadcasted_iota(jnp.int32, (l_len, l_len), 0)
    col = lax.broadcasted_iota(jnp.int32, (l_len, l_len), 1)
    causal = row >= col
    first_half = lax.broadcasted_iota(jnp.int32, (1, LANES), 1) < SSM_HEAD_DIM
    grp_mask = [first_half, jnp.logical_not(first_half)]
    srow = lax.broadcasted_iota(jnp.int32, (LANES, MIX_W), 0)
    scol = lax.broadcasted_iota(jnp.int32, (LANES, MIX_W), 1)
    state_mask = (srow // SSM_STATE) == (scol // (MIX_W // SSM_GROUPS))
    s_cur = sssm_sc[...]
    s_enter = []
    for c in range(tm // l_len):
        rs = slice(c * l_len, (c + 1) * l_len)
        last_e = acs_e_all[(c + 1) * l_len - 1:(c + 1) * l_len, :]
        upd = _bdot_tn(bm[rs], xdt[rs] * jnp.exp(last_e - acs_e_all[rs]))
        s_enter.append(s_cur)
        s_cur = s_cur * jnp.exp(last_e) + jnp.where(state_mask, upd, 0.0)
    sssm_sc[...] = s_cur

    ssm_rows = []
    for c in range(tm // l_len):
        rs = slice(c * l_len, (c + 1) * l_len)
        acs = acs_all[rs]
        acs_e = acs_e_all[rs]
        acs_t = acs.T
        bm_c = bm[rs]
        cm_c = cm[rs]
        y_off = _bdot(cm_c, s_enter[c]) * jnp.exp(acs_e)
        cb = [_bdot_nt(jnp.where(grp_mask[g], cm_c, 0.0), bm_c) for g in range(SSM_GROUPS)]
        tiles = []
        for j in range(MIX_W // LANES):
            xt = xdt[rs, j * LANES:(j + 1) * LANES]
            outs = []
            for hh in (2 * j, 2 * j + 1):
                seg = acs[:, hh:hh + 1] - acs_t[hh:hh + 1, :]
                dec = jnp.where(causal, jnp.exp(jnp.where(causal, seg, 0.0)), 0.0)
                outs.append(_bdot(cb[hh // (SSM_HEADS // SSM_GROUPS)] * dec, xt))
            tiles.append(jnp.where(first_half, outs[0], outs[1]))
            _run_next(side)
        y_diag = jnp.concatenate(tiles, axis=1)
        y = y_diag + y_off + dexp_ref[...] * xs[rs]
        y = y * _silu(z[rs])
        halves = []
        for g in range(SSM_GROUPS):
            yg = y[:, g * (MIX_W // SSM_GROUPS):(g + 1) * (MIX_W // SSM_GROUPS)]
            halves.append(yg * lax.rsqrt(jnp.mean(yg * yg, axis=-1, keepdims=True) + SSM_NORM_EPS))
        ssm_rows.append(jnp.concatenate(halves, axis=1) * sng_ref[...])
    y_ssm = jnp.concatenate(ssm_rows, axis=0)

    _run_all(side)
    return x + _bdot(jnp.concatenate([y_ret, y_ssm], axis=1), woutb_sc[...])


def _ab_body(x_ref, xn_ref, rota_ref, rotb_ref, g_ref, win_ref, wdt_ref, rdec_ref, rqfs_ref, rkte_ref, rcd_ref,
             cw_ref, cb_ref, dtb_ref, alog_ref, dexp_ref, sng_ref, exp_ref, tri_ref, wout_ref,
             o_ref, sret_sc, sssm_sc, ext_sc, winb_sc, woutb_sc, pa_sc, pb_sc, *, tm):
    steps = functools.partial(_projection_steps, g_ref=g_ref, w_sc=winb_sc, tail_ref=wdt_ref)

    @pl.when(pl.program_id(0) == 0)
    def _():
        sret_sc[...] = jnp.zeros_like(sret_sc)
        sssm_sc[...] = jnp.zeros_like(sssm_sc)
        ext_sc[...] = jnp.zeros_like(ext_sc)
        _cast_weight(winb_sc, win_ref)
        _cast_weight(woutb_sc, wout_ref)
        _run_all(steps(x_ref[0:tm, :], pa_sc))

    mix = functools.partial(
        _ab_mix, rotb_ref=rotb_ref, rdec_ref=rdec_ref, rqfs_ref=rqfs_ref, rkte_ref=rkte_ref, rcd_ref=rcd_ref,
        cw_ref=cw_ref, cb_ref=cb_ref, dtb_ref=dtb_ref, alog_ref=alog_ref, dexp_ref=dexp_ref, sng_ref=sng_ref,
        exp_ref=exp_ref, tri_ref=tri_ref, sret_sc=sret_sc, sssm_sc=sssm_sc, ext_sc=ext_sc, woutb_sc=woutb_sc,
        tm=tm)
    o_ref[0:tm, :] = mix(pa_sc, x_ref[0:tm, :], rota_ref[0], side=steps(x_ref[tm:2 * tm, :], pb_sc))
    o_ref[tm:2 * tm, :] = mix(pb_sc, x_ref[tm:2 * tm, :], rota_ref[1], side=steps(xn_ref[...], pa_sc))


def _rotary_tables(n_tiles, tm):
    inv_freq = ROPE_BASE ** (-np.arange(0, RET_DK, 2, dtype=np.float64) / RET_DK)
    two = lambda t: np.concatenate([t, t], axis=-1)
    sign = np.concatenate([-np.ones(RET_DK // 2), np.ones(RET_DK // 2)])
    ang_a = (np.arange(n_tiles, dtype=np.float64) * tm)[:, None] * inv_freq[None, :]
    ang_b = np.arange(tm, dtype=np.float64)[:, None] * inv_freq[None, :]
    cos_a, sin_a = two(np.cos(ang_a)), two(np.sin(ang_a))
    rot_a = np.zeros((n_tiles, SUBLANES, RET_DK))
    rot_a[:, 0], rot_a[:, 1], rot_a[:, 2], rot_a[:, 3] = cos_a, sin_a, sign * sin_a, sign * cos_a
    rot_b = np.concatenate([two(np.cos(ang_b)), two(np.sin(ang_b))], axis=1)
    return jnp.asarray(rot_a, F32), jnp.asarray(rot_b, F32)


def _retention_tables():
    c = RET_CHUNK
    log_g = np.log1p(-(2.0 ** (-5.0 - np.arange(RET_HEADS, dtype=np.float64))))
    pos = np.arange(c, dtype=np.float64)
    rel = pos[:, None] - pos[None, :]
    rdec = np.where(rel >= 0, np.exp(np.maximum(rel, 0.0)[None] * log_g[:, None, None]), 0.0)
    per_head = lambda t: np.repeat(t, RET_DK, axis=1)
    rkte = per_head(np.exp((c - 1 - pos)[:, None] * log_g[None, :]))
    rqfs = per_head(np.exp((pos + 1.0)[:, None] * log_g[None, :]))
    rcd = per_head(np.exp(c * log_g)[None, :])
    return tuple(jnp.asarray(t, F32) for t in (rdec, rqfs, rkte, rcd))


def _mix_ab(x, norm_g, w_in_all, w_out_all, j, conv_w, conv_b, dt_bias, a_log, d_skip, norm_ssm):
    t_len, d = x.shape
    tm = min(MIX_TM, t_len)
    rot_a, rot_b = _rotary_tables(t_len // tm, tm)
    rdec, rqfs, rkte, rcd = _retention_tables()

    w_dt = jnp.pad(w_in_all[j][:, AB_DT:], ((0, 0), (0, LANES - SSM_HEADS))).astype(BF16)
    pad_row = lambda v: jnp.pad(v.astype(F32), (0, LANES - v.shape[0])).reshape(1, LANES)
    expand = (jnp.arange(LANES)[:, None] == (jnp.arange(MIX_W)[None, :] // SSM_HEAD_DIM)).astype(BF16)
    expand = jnp.concatenate([expand] * 3, axis=0)
    t_idx = jnp.arange(tm)
    tri = (((t_idx[:, None] // SSM_CHUNK) == (t_idx[None, :] // SSM_CHUNK)) &
           (t_idx[:, None] >= t_idx[None, :])).astype(BF16)

    operands = [
        (x, pl.BlockSpec((2 * tm, d), lambda i: (i, 0))),
        (x, _next_tile_spec(tm, d, t_len // tm)),
        (rot_a, pl.BlockSpec((2, SUBLANES, RET_DK), lambda i: (i, 0, 0))),
        (rot_b, None),
        (norm_g.reshape(1, d), None),
        (w_in_all, _layer_spec((d, AB_DT), j)),
        (w_dt, None),
        (rdec, None), (rqfs, None), (rkte, None), (rcd, None),
        (conv_w, None), (conv_b.reshape(1, SSM_XBC), None),
        (pad_row(dt_bias), None), (pad_row(a_log), None),
        (jnp.repeat(d_skip.astype(F32), SSM_HEAD_DIM).reshape(1, MIX_W), None),
        (norm_ssm.reshape(1, MIX_W), None),
        (expand, None), (tri, None),
        (w_out_all, _layer_spec((2 * MIX_W, d), j)),
    ]
    args = [a for a, _ in operands]
    specs = [s if s is not None else _resident_spec(a.shape) for a, s in operands]
    return pl.pallas_call(
        functools.partial(_ab_body, tm=tm),
        grid=(t_len // (2 * tm),),
        in_specs=specs,
        out_specs=pl.BlockSpec((2 * tm, d), lambda i: (i, 0)),
        out_shape=jax.ShapeDtypeStruct((t_len, d), F32),
        scratch_shapes=[
            pltpu.VMEM((RET_HEADS, RET_DK, RET_DK), F32),
            pltpu.VMEM((LANES, MIX_W), F32),
            pltpu.VMEM((SUBLANES, SSM_XBC), F32),
            pltpu.VMEM((d, AB_DT), BF16),
            pltpu.VMEM((2 * MIX_W, d), BF16),
            pltpu.VMEM((tm, AB_DT + LANES), F32),
            pltpu.VMEM((tm, AB_DT + LANES), F32),
        ],
        compiler_params=pltpu.CompilerParams(
            dimension_semantics=("arbitrary",), vmem_limit_bytes=VMEM_LIMIT),
        name="mix_ret_ssd",
    )(*args)


def _head_sum(x, ones_blk):
    tiles = [_sel_dot_lhs(x[:, j * LANES:(j + 1) * LANES], ones_blk, 2) for j in range(x.shape[1] // LANES)]
    return jnp.concatenate(tiles, axis=1)


def _cd_mix(p_ref, x, lcw_ref, lcb_ref, wa_ref, ba_ref, wx_ref, bx_ref, lam_ref,
            mu_ref, w0_ref, w2_ref, a0_ref, a2_ref, g2_ref, kk_ref, ka_ref, rk_ref, lng_ref, lnb_ref,
            ones_ref, sel_ref, ext_sc, hcar_sc, pcar_sc, s_sc, woutb_sc, tm, side):
    def proj(lo, width):
        return p_ref[:, lo:lo + width]

    rows = lax.broadcasted_iota(jnp.int32, (tm, 1), 0)

    xc = _causal_conv(ext_sc, proj(CD_XB, MIX_W), lcw_ref, lcb_ref, tm, LRU_CONV)
    r_parts, i_parts = [], []
    for n in range(LRU_BLOCKS):
        ls = slice(n * LRU_BLOCK, (n + 1) * LRU_BLOCK)
        r_parts.append(jax.nn.sigmoid(_bdot(xc[:, ls], wa_ref[n]) + ba_ref[:, ls]))
        i_parts.append(jax.nn.sigmoid(_bdot(xc[:, ls], wx_ref[n]) + bx_ref[:, ls]))
    r_gate = jnp.concatenate(r_parts, axis=1)
    i_gate = jnp.concatenate(i_parts, axis=1)
    log_a = -LRU_C * r_gate * _softplus(-lam_ref[...])
    a_cum = jnp.exp(log_a)
    hs = _sqrt_nonneg(-jnp.tanh(log_a) * (1.0 + a_cum * a_cum)) * (i_gate * xc)
    in_group = rows % SUBLANES
    shift = 1
    while shift < SUBLANES:
        valid = in_group >= shift
        a_prev = jnp.where(valid, pltpu.roll(a_cum, shift, axis=0), 1.0)
        h_prev = jnp.where(valid, pltpu.roll(hs, shift, axis=0), 0.0)
        hs = hs + a_cum * h_prev
        a_cum = a_cum * a_prev
        shift *= 2
    carry = hcar_sc[...]
    groups = []
    for gi in range(tm // SUBLANES):
        grp = slice(gi * SUBLANES, (gi + 1) * SUBLANES)
        hg = hs[grp] + a_cum[grp] * carry
        groups.append(hg)
        carry = hg[SUBLANES - 1:SUBLANES, :]
    hs = jnp.concatenate(groups, axis=0)
    hcar_sc[...] = carry
    y_lru = hs * _gelu_tanh(proj(CD_GB, MIX_W))

    pr = proj(CD_PR, CD_PR_COLS)
    prev = pltpu.roll(pr, 1, axis=0)
    row8 = lax.broadcasted_iota(jnp.int32, (SUBLANES, 1), 0)
    prev = jnp.concatenate([jnp.where(row8 == 0, pcar_sc[...], prev[0:SUBLANES]), prev[SUBLANES:]], axis=0)
    pcar_sc[...] = pr[tm - 1:tm, :]
    ps = pr + (prev - pr) * mu_ref[...]
    r = ps[:, 0:MIX_W]
    k = ps[:, MIX_W:2 * MIX_W]
    v = ps[:, 2 * MIX_W:3 * MIX_W]
    wl = ps[:, 3 * MIX_W:3 * MIX_W + LANES]
    al = ps[:, 3 * MIX_W + LANES:3 * MIX_W + 2 * LANES]
    gl = ps[:, 3 * MIX_W + 2 * LANES:3 * MIX_W + 4 * LANES]
    log_w = -math.exp(-0.5) * jax.nn.sigmoid(w0_ref[...] + _bdot(jnp.tanh(wl), w2_ref[...]))
    a = jax.nn.sigmoid(a0_ref[...] + _bdot(al, a2_ref[...]))
    g = _bdot(jax.nn.sigmoid(gl), g2_ref[...])
    ones_blk = ones_ref[...]
    kk = k * kk_ref[...]
    kk = kk * lax.rsqrt(jnp.maximum(_head_sum(kk * kk, ones_blk), 1e-24))
    k = k * (1.0 + (a - 1.0) * ka_ref[...])
    c_len = RWKV_CHUNK
    pair = 2 * c_len
    n_chunks = tm // c_len
    n_pairs = MIX_W // LANES

    res = _sel_dot_rhs(sel_ref[...], log_w, 3)
    cs = res[0:tm]
    tot = res[tm:2 * tm]
    e_neg = jnp.exp(-cs)
    e_end = jnp.exp(tot - cs)
    w_end = jnp.exp(tot)
    nb = -(kk * a)
    bt = kk * jnp.exp(cs - log_w)
    rt = r * jnp.exp(cs)
    ab = nb * e_neg
    kb = k * e_neg
    ae = nb * e_end
    ke = k * e_end

    lane = lax.broadcasted_iota(jnp.int32, (1, LANES), 1)
    m0 = lane < RWKV_HEAD_DIM
    trow = lax.broadcasted_iota(jnp.int32, (c_len, LANES), 0)
    scol = lax.broadcasted_iota(jnp.int32, (c_len, LANES), 1) % c_len
    strict = trow > scol
    incl = trow >= scol
    eye = (trow == scol).astype(F32)

    def merge_mask(s):
        return ((trow // s) == (scol // s) + 1) & ((trow // s) % 2 == 1)

    def blockdiag(t):
        return jnp.concatenate([jnp.where(m0, t, 0.0), jnp.where(m0, 0.0, t)], axis=0)

    def tile(t, c, j):
        return t[c * c_len:(c + 1) * c_len, j * LANES:(j + 1) * LANES]

    units = [(c, j) for c in range(n_chunks) for j in range(n_pairs)]
    vbd = {u: blockdiag(tile(v, *u)).astype(BF16) for u in units}
    n_mat, lhs_x, lhs_y, lhs_z, w_col = {}, {}, {}, {}, {}
    for u in units:
        bt_u = tile(bt, *u).astype(BF16)
        rt_u = tile(rt, *u).astype(BF16)
        gram = _bdot_nt(jnp.concatenate([bt_u, rt_u], axis=0),
                        jnp.concatenate([blockdiag(tile(ab, *u)), blockdiag(tile(kb, *u))], axis=0))
        n_mat[u] = jnp.where(strict, gram[0:c_len, 0:LANES], 0.0)
        lhs_x[u] = jnp.concatenate(
            [bt_u, jnp.where(strict, gram[0:c_len, LANES:2 * LANES], 0.0).astype(BF16)], axis=1)
        lhs_y[u] = jnp.concatenate(
            [rt_u, jnp.where(incl, gram[c_len:pair, 0:LANES], 0.0).astype(BF16),
             jnp.where(incl, gram[c_len:pair, LANES:2 * LANES], 0.0).astype(BF16)], axis=1)
        lhs_z[u] = jnp.concatenate(
            [blockdiag(tile(ae, *u)).T.astype(BF16), blockdiag(tile(ke, *u)).T.astype(BF16)], axis=1)
        w_row = tile(w_end, *u)
        w_col[u] = jnp.concatenate([w_row, w_row], axis=0).T

    t_inv = {u: eye + jnp.where(merge_mask(1), n_mat[u], 0.0) for u in units}
    s = 2
    while s < c_len:
        mask = merge_mask(s)
        half = {u: _bdot(t_inv[u], blockdiag(jnp.where(mask, n_mat[u], 0.0))) for u in units}
        t_inv = {u: t_inv[u] + _bdot(half[u], blockdiag(t_inv[u])) for u in units}
        s *= 2
    t_inv = {u: t_inv[u].astype(BF16) for u in units}

    y_rows = []
    for c in range(n_chunks):
        z_prev = [s_sc[j] for j in range(n_pairs)]
        zb = [z.astype(BF16) for z in z_prev]
        xs = [jnp.dot(lhs_x[(c, j)], jnp.concatenate([zb[j], vbd[(c, j)]], axis=0),
                      preferred_element_type=F32) for j in range(n_pairs)]
        _run_next(side)
        us = [jnp.dot(t_inv[(c, j)], blockdiag(xs[j]).astype(BF16), preferred_element_type=F32)
              for j in range(n_pairs)]
        _run_next(side)
        us = [blockdiag(u).astype(BF16) for u in us]
        tiles = []
        for j in range(n_pairs):
            tiles.append(jnp.dot(lhs_y[(c, j)], jnp.concatenate([zb[j], us[j], vbd[(c, j)]], axis=0),
                                 preferred_element_type=F32))
            s_sc[j] = z_prev[j] * w_col[(c, j)] + jnp.dot(
                lhs_z[(c, j)], jnp.concatenate([us[j], vbd[(c, j)]], axis=0), preferred_element_type=F32)
        y_rows.append(jnp.concatenate(tiles, axis=1))
    y = jnp.concatenate(y_rows, axis=0)
    inv_n = 1.0 / RWKV_HEAD_DIM
    mean = _head_sum(y, ones_blk) * inv_n
    yc = y - mean
    var = _head_sum(yc * yc, ones_blk) * inv_n
    y = yc * lax.rsqrt(var + RWKV_LN_EPS) * lng_ref[...] + lnb_ref[...]
    y = y + _head_sum(r * k * rk_ref[...], ones_blk) * v
    y_rwkv = y * g

    _run_all(side)
    return x + _bdot(jnp.concatenate([y_lru, y_rwkv], axis=1), woutb_sc[...])


def _cd_body(x_ref, xn_ref, g_ref, win_ref, lcw_ref, lcb_ref, wa_ref, ba_ref, wx_ref, bx_ref, lam_ref,
             mu_ref, w0_ref, w2_ref, a0_ref, a2_ref, g2_ref, kk_ref, ka_ref, rk_ref, lng_ref, lnb_ref,
             ones_ref, sel_ref, wlora_ref, wout_ref, o_ref,
             ext_sc, hcar_sc, pcar_sc, s_sc, winb_sc, woutb_sc, pa_sc, pb_sc, *, tm):
    steps = functools.partial(_projection_steps, g_ref=g_ref, w_sc=winb_sc, tail_ref=wlora_ref)

    @pl.when(pl.program_id(0) == 0)
    def _():
        ext_sc[...] = jnp.zeros_like(ext_sc)
        hcar_sc[...] = jnp.zeros_like(hcar_sc)
        pcar_sc[...] = jnp.zeros_like(pcar_sc)
        s_sc[...] = jnp.zeros_like(s_sc)
        _cast_weight(winb_sc, win_ref)
        _cast_weight(woutb_sc, wout_ref)
        _run_all(steps(x_ref[0:tm, :], pa_sc))

    mix = functools.partial(
        _cd_mix, lcw_ref=lcw_ref, lcb_ref=lcb_ref, wa_ref=wa_ref, ba_ref=ba_ref, wx_ref=wx_ref, bx_ref=bx_ref,
        lam_ref=lam_ref, mu_ref=mu_ref, w0_ref=w0_ref, w2_ref=w2_ref, a0_ref=a0_ref, a2_ref=a2_ref, g2_ref=g2_ref,
        kk_ref=kk_ref, ka_ref=ka_ref, rk_ref=rk_ref, lng_ref=lng_ref, lnb_ref=lnb_ref, ones_ref=ones_ref,
        sel_ref=sel_ref, ext_sc=ext_sc, hcar_sc=hcar_sc, pcar_sc=pcar_sc, s_sc=s_sc, woutb_sc=woutb_sc, tm=tm)
    o_ref[0:tm, :] = mix(pa_sc, x_ref[0:tm, :], side=steps(x_ref[tm:2 * tm, :], pb_sc))
    o_ref[tm:2 * tm, :] = mix(pb_sc, x_ref[tm:2 * tm, :], side=steps(xn_ref[...], pa_sc))


def _mix_cd(x, norm_g, w_in_all, w_out_all, j, lru_conv_w, lru_conv_b, lru_wa, lru_ba, lru_wx, lru_bx,
            lru_lambda, mu, w0, w2, a0, a2, g2, k_k, k_a, r_k, ln_g, ln_b):
    t_len, d = x.shape
    tm = min(MIX_TM, t_len)

    def pad_lora(t, lo, width, padded):
        return jnp.pad(t[..., lo:lo + width], [(0, 0)] * (t.ndim - 1) + [(0, padded - width)])

    def lora_cols(t, base):
        return jnp.concatenate([
            pad_lora(t, base, DECAY_LORA, LANES),
            pad_lora(t, base + DECAY_LORA, ICL_LORA, LANES),
            pad_lora(t, base + DECAY_LORA + ICL_LORA, GATE_LORA, 2 * LANES)], axis=-1)

    w_lora = lora_cols(w_in_all[j], CD_LORA).astype(BF16)
    mu_row = mu.reshape(1, -1)
    mu_pad = jnp.concatenate([mu_row[:, :3 * MIX_W], lora_cols(mu_row, 3 * MIX_W)], axis=-1)
    pad_rows = lambda t, n: jnp.pad(t, ((0, n - t.shape[0]), (0, 0))).astype(BF16)
    row = lambda t: t.reshape(1, -1).astype(F32)
    ones_blk = ((jnp.arange(LANES)[:, None] // RWKV_HEAD_DIM) ==
                (jnp.arange(LANES)[None, :] // RWKV_HEAD_DIM)).astype(BF16)
    ones_blk = jnp.concatenate([ones_blk] * 2, axis=0)
    t_idx = jnp.arange(tm)
    same_chunk = (t_idx[:, None] // RWKV_CHUNK) == (t_idx[None, :] // RWKV_CHUNK)
    sel = jnp.concatenate([same_chunk & (t_idx[:, None] >= t_idx[None, :]), same_chunk], axis=0).astype(BF16)

    operands = [
        (x, pl.BlockSpec((2 * tm, d), lambda i: (i, 0))),
        (x, _next_tile_spec(tm, d, t_len // tm)),
        (row(norm_g), None),
        (w_in_all, _layer_spec((d, CD_LORA), j)),
        (lru_conv_w, None), (row(lru_conv_b), None),
        (lru_wa.astype(BF16), None), (row(lru_ba), None),
        (lru_wx.astype(BF16), None), (row(lru_bx), None),
        (row(lru_lambda), None),
        (mu_pad, None),
        (row(w0), None), (pad_rows(w2, LANES), None),
        (row(a0), None), (pad_rows(a2, LANES), None),
        (pad_rows(g2, 2 * LANES), None),
        (row(k_k), None), (row(k_a), None), (row(r_k), None), (row(ln_g), None), (row(ln_b), None),
        (ones_blk, None), (sel, None),
        (w_lora, None),
        (w_out_all, _layer_spec((2 * MIX_W, d), j)),
    ]
    args = [a for a, _ in operands]
    specs = [s if s is not None else _resident_spec(a.shape) for a, s in operands]
    return pl.pallas_call(
        functools.partial(_cd_body, tm=tm),
        grid=(t_len // (2 * tm),),
        in_specs=specs,
        out_specs=pl.BlockSpec((2 * tm, d), lambda i: (i, 0)),
        out_shape=jax.ShapeDtypeStruct((t_len, d), F32),
        scratch_shapes=[
            pltpu.VMEM((SUBLANES, MIX_W), F32),
            pltpu.VMEM((1, MIX_W), F32),
            pltpu.VMEM((1, CD_PR_COLS), F32),
            pltpu.VMEM((MIX_W // LANES, LANES, LANES), F32),
            pltpu.VMEM((d, CD_LORA), BF16),
            pltpu.VMEM((2 * MIX_W, d), BF16),
            pltpu.VMEM((tm, CD_PR + CD_PR_COLS), F32),
            pltpu.VMEM((tm, CD_PR + CD_PR_COLS), F32),
        ],
        compiler_params=pltpu.CompilerParams(
            dimension_semantics=("arbitrary",), vmem_limit_bytes=VMEM_LIMIT),
        name="mix_lru_rwkv",
    )(*args)


def kernel(x, ffn1_norm, ffn1_wg, ffn1_wu, ffn1_wd, mix_norm, ffn2_norm, ffn2_wg, ffn2_wu, ffn2_wd,
           ab_w_in, ab_w_out, ssm_conv_w, ssm_conv_b, ssm_dt_bias, ssm_a_log, ssm_d, ssm_norm,
           cd_w_in, cd_w_out, lru_conv_w, lru_conv_b, lru_wa, lru_ba, lru_wx, lru_bx, lru_lambda,
           rwkv_mu, rwkv_w0, rwkv_w2, rwkv_a0, rwkv_a2, rwkv_g2, rwkv_kk, rwkv_ka, rwkv_rk,
           rwkv_ln_g, rwkv_ln_b, final_norm):
    bsz, t_len, d = x.shape
    depth = ffn1_norm.shape[0]
    outs = []
    for b in range(bsz):
        xb = x.reshape(t_len, d) if bsz == 1 else x[b]
        for layer in range(depth):
            j = layer // 2
            xb = _ffn(xb, ffn1_norm[layer], ffn1_wg, ffn1_wu, ffn1_wd, layer, final_norm, final_norm=False)
            if layer % 2 == 0:
                xb = _mix_ab(xb, mix_norm[layer], ab_w_in, ab_w_out, j, ssm_conv_w[j], ssm_conv_b[j],
                             ssm_dt_bias[j], ssm_a_log[j], ssm_d[j], ssm_norm[j])
            else:
                xb = _mix_cd(xb, mix_norm[layer], cd_w_in, cd_w_out, j, lru_conv_w[j], lru_conv_b[j], lru_wa[j],
                             lru_ba[j], lru_wx[j], lru_bx[j], lru_lambda[j], rwkv_mu[j], rwkv_w0[j],
                             rwkv_w2[j], rwkv_a0[j], rwkv_a2[j], rwkv_g2[j], rwkv_kk[j], rwkv_ka[j],
                             rwkv_rk[j].reshape(-1), rwkv_ln_g[j], rwkv_ln_b[j])
            xb = _ffn(xb, ffn2_norm[layer], ffn2_wg, ffn2_wu, ffn2_wd, layer, final_norm,
                      final_norm=(layer == depth - 1))
        outs.append(xb)
    return outs[0].reshape(1, t_len, d) if bsz == 1 else jnp.stack(outs, axis=0)
```

```python
import functools
import math

import jax
import jax.numpy as jnp
import numpy as np
from jax import lax
from jax.experimental import pallas as pl
from jax.experimental.pallas import tpu as pltpu

F32 = jnp.float32
BF16 = jnp.bfloat16

D_MODEL = 1024
D_FF = 2816
NORM_EPS = 1e-6
MIX_W = 512

RET_HEADS = 4
RET_DK = 128
RET_CHUNK = 128
ROPE_BASE = 10000.0
HEAD_NORM_EPS = 1e-6

SSM_HEADS = 8
SSM_HEAD_DIM = 64
SSM_GROUPS = 2
SSM_STATE = 64
SSM_CONV = 4
SSM_CHUNK = 128
SSM_XBC = 768
SSM_NORM_EPS = 1e-5

LRU_BLOCKS = 4
LRU_BLOCK = 128
LRU_CONV = 4
LRU_C = 8.0

RWKV_HEADS = 8
RWKV_HEAD_DIM = 64
DECAY_LORA = 64
ICL_LORA = 64
GATE_LORA = 160
RWKV_LN_EPS = 64e-5
RWKV_CHUNK = 64

LANES = 128
SUBLANES = 8
VMEM_LIMIT = 56 * 1024 * 1024

FFN_TM = 512
FFN_TF = 256
MXU_DIM = 256
FFN_TK = (0, 6 * MXU_DIM, D_FF)
MIX_TM = 256
PROJ_TN = 256

AB_Q, AB_K, AB_V, AB_G, AB_Z, AB_XBC, AB_DT = 0, 512, 1024, 1536, 2048, 2560, 3328
CD_XB, CD_GB, CD_PR, CD_LORA = 0, 512, 1024, 2560
CD_PR_COLS = 2048


def _bdot(a, b):
    return jnp.dot(a.astype(BF16), b.astype(BF16), preferred_element_type=F32)


def _bdot_nt(a, b):
    return lax.dot_general(a.astype(BF16), b.astype(BF16), (((1,), (1,)), ((), ())),
                           preferred_element_type=F32)


def _bdot_tn(a, b):
    return lax.dot_general(a.astype(BF16), b.astype(BF16), (((0,), (0,)), ((), ())),
                           preferred_element_type=F32)


def _split(x, pieces):
    out = []
    for _ in range(pieces - 1):
        p = x.astype(BF16)
        out.append(p)
        x = x - p.astype(F32)
    out.append(x.astype(BF16))
    return out


def _sel_dot_lhs(x, sel_stack, pieces):
    return jnp.dot(jnp.concatenate(_split(x, pieces), axis=1), sel_stack, preferred_element_type=F32)


def _sel_dot_rhs(sel, x, pieces):
    n = x.shape[1]
    res = jnp.dot(sel, jnp.concatenate(_split(x, pieces), axis=1), preferred_element_type=F32)
    out = res[:, 0:n]
    for p in range(1, pieces):
        out = out + res[:, p * n:(p + 1) * n]
    return out


def _rms_norm(x, g, eps):
    return x * lax.rsqrt(jnp.mean(x * x, axis=-1, keepdims=True) + eps) * g


def _silu(x):
    return x * jax.nn.sigmoid(x)


def _softplus(x):
    return jnp.maximum(x, 0.0) + jnp.log1p(jnp.exp(-jnp.abs(x)))


def _sqrt_nonneg(y):
    return y * lax.rsqrt(jnp.maximum(y, jnp.finfo(jnp.float32).tiny))


def _gelu_tanh(x):
    c = math.sqrt(2.0 / math.pi)
    return 0.5 * x * (1.0 + jnp.tanh(c * (x + 0.044715 * (x * x * x))))


def _causal_conv(tail_ref, x, w_ref, b_ref, tm, width):
    tail = tail_ref[...]
    row = lax.broadcasted_iota(jnp.int32, (SUBLANES, 1), 0)
    acc = b_ref[...] + w_ref[width - 1:width, :] * x
    for s in range(1, width):
        delayed = pltpu.roll(x, s, axis=0)
        head = jnp.where(row < s, pltpu.roll(tail, s, axis=0), delayed[0:SUBLANES])
        delayed = jnp.concatenate([head, delayed[SUBLANES:]], axis=0)
        acc = acc + w_ref[width - 1 - s:width - s, :] * delayed
    tail_ref[...] = x[tm - SUBLANES:tm]
    return acc


def _ffn_body(x_ref, g_ref, wg_ref, wu_ref, wd_ref, fin_ref, o_ref, act_sc, *, final_norm):
    x = x_ref[...]
    h = _rms_norm(x, g_ref[...], NORM_EPS).astype(BF16)
    for j in range(D_FF // FFN_TF):
        cols = slice(j * FFN_TF, (j + 1) * FFN_TF)
        gate = jnp.dot(h, wg_ref[:, cols].astype(BF16), preferred_element_type=F32)
        up = jnp.dot(h, wu_ref[:, cols].astype(BF16), preferred_element_type=F32)
        act_sc[:, cols] = (_silu(gate) * up).astype(BF16)
    down = None
    for lo, hi in zip(FFN_TK[:-1], FFN_TK[1:]):
        part = jnp.dot(act_sc[:, lo:hi], wd_ref[lo:hi, :].astype(BF16), preferred_element_type=F32)
        down = part if down is None else down + part
    y = x + 0.5 * down
    if final_norm:
        y = _rms_norm(y, fin_ref[...], NORM_EPS)
    o_ref[...] = y


def _resident_spec(shape):
    nd = len(shape)
    return pl.BlockSpec(shape, lambda i, _nd=nd: (0,) * _nd, pipeline_mode=pl.Buffered(1))


def _next_tile_spec(tm, d, n_tiles):
    return pl.BlockSpec((tm, d), lambda i: (jnp.minimum(2 * i + 2, n_tiles - 1), 0))


def _layer_spec(shape, layer):
    nd = len(shape)
    return pl.BlockSpec((None,) + tuple(shape), lambda i, _nd=nd: (layer,) + (0,) * _nd,
                        pipeline_mode=pl.Buffered(1))


def _layer_operand(stack, layer):
    if stack.shape[0] == 1:
        return stack.reshape(stack.shape[1:]), _resident_spec(stack.shape[1:])
    return stack, _layer_spec(stack.shape[1:], layer)


def _ffn(x, norm_g, wg, wu, wd, layer, fin_g, *, final_norm):
    t_len, d = x.shape
    tm = min(FFN_TM, t_len)
    return pl.pallas_call(
        functools.partial(_ffn_body, final_norm=final_norm),
        grid=(t_len // tm,),
        in_specs=[
            pl.BlockSpec((tm, d), lambda i: (i, 0)),
            _resident_spec((1, d)),
            _layer_spec((d, D_FF), layer),
            _layer_spec((d, D_FF), layer),
            _layer_spec((D_FF, d), layer),
            _resident_spec((1, d)),
        ],
        out_specs=pl.BlockSpec((tm, d), lambda i: (i, 0)),
        out_shape=jax.ShapeDtypeStruct((t_len, d), F32),
        scratch_shapes=[pltpu.VMEM((tm, D_FF), BF16)],
        compiler_params=pltpu.CompilerParams(
            dimension_semantics=("arbitrary",), vmem_limit_bytes=VMEM_LIMIT),
        name="ffn_final" if final_norm else "ffn",
    )(x, norm_g.reshape(1, d), wg, wu, wd, fin_g.reshape(1, d))


def _cast_weight(dst_sc, src_ref):
    for lo in range(0, dst_sc.shape[1], MXU_DIM):
        hi = min(lo + MXU_DIM, dst_sc.shape[1])
        dst_sc[:, lo:hi] = src_ref[:, lo:hi].astype(BF16)


def _cast_weight_rows(dst_sc, src_ref):
    for lo in range(0, dst_sc.shape[0], MXU_DIM):
        hi = min(lo + MXU_DIM, dst_sc.shape[0])
        dst_sc[lo:hi, :] = src_ref[lo:hi, :].astype(BF16)


def _projection_steps(x, p_ref, g_ref, wt_sc, tail_sc):
    h = _rms_norm(x, g_ref[...], NORM_EPS).astype(BF16)
    main = wt_sc.shape[0]
    nt_dot = lambda w: lax.dot_general(h, w, (((1,), (1,)), ((), ())), preferred_element_type=F32)

    def section(lo, hi):
        p_ref[:, lo:hi] = nt_dot(wt_sc[lo:hi, :])

    def tail():
        p_ref[:, main:main + tail_sc.shape[0]] = nt_dot(tail_sc[...])

    steps = [functools.partial(section, lo, min(lo + PROJ_TN, main)) for lo in range(0, main, PROJ_TN)]
    return steps + [tail]


def _run_next(steps):
    if steps:
        steps.pop(0)()


def _run_all(steps):
    while steps:
        _run_next(steps)


def _ab_mix(p_ref, x, rota, rotb_ref, rdec_ref, rqfs_ref, rkte_ref, rcd_ref,
            cw_ref, cb_ref, dtb_ref, alog_ref, dexp_ref, sng_ref, exp_ref, tri_ref,
            sret_sc, sssm_sc, ext_sc, woutb_sc, tm, side):
    def proj(lo, width):
        return p_ref[:, lo:lo + width]

    cos_a, sin_a = rota[0:1, :], rota[1:2, :]
    sin_a_sgn, cos_a_sgn = rota[2:3, :], rota[3:4, :]
    cos_b, sin_b = rotb_ref[:, 0:RET_DK], rotb_ref[:, RET_DK:2 * RET_DK]
    cos2 = cos_a * cos_b - sin_a * sin_b
    sin2 = sin_a_sgn * cos_b + cos_a_sgn * sin_b

    q_all = proj(AB_Q, MIX_W)
    k_all = proj(AB_K, MIX_W)
    v_all = proj(AB_V, MIX_W)
    gate_ret = _silu(proj(AB_G, MIX_W))
    scale = RET_DK ** -0.5
    c_len = RET_CHUNK
    n_chunks = tm // c_len
    cells = [(c, hh) for c in range(n_chunks) for hh in range(RET_HEADS)]
    rows_of = lambda c: slice(c * c_len, (c + 1) * c_len)
    lanes_of = lambda hh: slice(hh * RET_DK, (hh + 1) * RET_DK)

    def rotate(t, c, hh):
        t = t[rows_of(c), lanes_of(hh)]
        return t * cos2[rows_of(c)] + pltpu.roll(t, RET_DK // 2, axis=1) * sin2[rows_of(c)]

    qh = {u: rotate(q_all, *u) for u in cells}
    kh = {u: rotate(k_all, *u) * scale for u in cells}
    vh = {(c, hh): v_all[rows_of(c), lanes_of(hh)].astype(BF16) for c, hh in cells}
    scores = {(c, hh): _bdot_nt(qh[(c, hh)], kh[(c, hh)]) * rdec_ref[hh] for c, hh in cells}
    kv = {(c, hh): _bdot_tn(kh[(c, hh)] * rkte_ref[:, lanes_of(hh)], vh[(c, hh)]) for c, hh in cells}
    state = {}
    for hh in range(RET_HEADS):
        s_cur = sret_sc[hh]
        for c in range(n_chunks):
            state[(c, hh)] = s_cur
            s_cur = s_cur * rcd_ref[:, lanes_of(hh)] + kv[(c, hh)]
        sret_sc[hh] = s_cur
    outs = {(c, hh): _bdot(jnp.concatenate([scores[(c, hh)], qh[(c, hh)] * rqfs_ref[:, lanes_of(hh)]], axis=1),
                           jnp.concatenate([vh[(c, hh)], state[(c, hh)].astype(BF16)], axis=0))
            for c, hh in cells}
    outs = {u: o * lax.rsqrt(jnp.mean(o * o, axis=-1, keepdims=True) + HEAD_NORM_EPS) for u, o in outs.items()}
    y_ret = gate_ret * jnp.concatenate(
        [jnp.concatenate([outs[(c, hh)] for hh in range(RET_HEADS)], axis=1) for c in range(n_chunks)], axis=0)

    z = proj(AB_Z, MIX_W)
    xbc = _silu(_causal_conv(ext_sc, proj(AB_XBC, SSM_XBC), cw_ref, cb_ref, tm, SSM_CONV))
    xs = xbc[:, :MIX_W]
    bm = xbc[:, MIX_W:MIX_W + LANES]
    cm = xbc[:, MIX_W + LANES:MIX_W + 2 * LANES]
    lane = lax.broadcasted_iota(jnp.int32, (1, LANES), 1)
    dt = _softplus(proj(AB_DT, LANES) + dtb_ref[...])
    a_neg = jnp.where(lane < SSM_HEADS, -jnp.exp(alog_ref[...]), 0.0)
    adt = dt * a_neg
    xdt = xs * _sel_dot_lhs(dt, exp_ref[0:2 * LANES, :], 2)
    acs_all = _sel_dot_rhs(tri_ref[...], adt, 3)
    acs_e_all = _sel_dot_lhs(acs_all, exp_ref[...], 3)
    l_len = SSM_CHUNK
    row = lax.broadcasted_iota(jnp.int32, (l_len, l_len), 0)
    col = lax.broadcasted_iota(jnp.int32, (l_len, l_len), 1)
    causal = row >= col
    first_half = lax.broadcasted_iota(jnp.int32, (1, LANES), 1) < SSM_HEAD_DIM
    grp_mask = [first_half, jnp.logical_not(first_half)]
    srow = lax.broadcasted_iota(jnp.int32, (LANES, MIX_W), 0)
    scol = lax.broadcasted_iota(jnp.int32, (LANES, MIX_W), 1)
    state_mask = (srow // SSM_STATE) == (scol // (MIX_W // SSM_GROUPS))
    s_cur = sssm_sc[...]
    s_enter = []
    for c in range(tm // l_len):
        rs = slice(c * l_len, (c + 1) * l_len)
        last_e = acs_e_all[(c + 1) * l_len - 1:(c + 1) * l_len, :]
        upd = _bdot_tn(bm[rs], xdt[rs] * jnp.exp(last_e - acs_e_all[rs]))
        s_enter.append(s_cur)
        s_cur = s_cur * jnp.exp(last_e) + jnp.where(state_mask, upd, 0.0)
    sssm_sc[...] = s_cur

    ssm_rows = []
    for c in range(tm // l_len):
        rs = slice(c * l_len, (c + 1) * l_len)
        acs = acs_all[rs]
        acs_e = acs_e_all[rs]
        acs_t = acs.T
        bm_c = bm[rs]
        cm_c = cm[rs]
        y_off = _bdot(cm_c, s_enter[c]) * jnp.exp(acs_e)
        cb = [_bdot_nt(jnp.where(grp_mask[g], cm_c, 0.0), bm_c) for g in range(SSM_GROUPS)]
        tiles = []
        for j in range(MIX_W // LANES):
            xt = xdt[rs, j * LANES:(j + 1) * LANES]
            outs = []
            for hh in (2 * j, 2 * j + 1):
                seg = acs[:, hh:hh + 1] - acs_t[hh:hh + 1, :]
                dec = jnp.where(causal, jnp.exp(jnp.where(causal, seg, 0.0)), 0.0)
                outs.append(_bdot(cb[hh // (SSM_HEADS // SSM_GROUPS)] * dec, xt))
                _run_next(side)
            tiles.append(jnp.where(first_half, outs[0], outs[1]))
        y_diag = jnp.concatenate(tiles, axis=1)
        y = y_diag + y_off + dexp_ref[...] * xs[rs]
        y = y * _silu(z[rs])
        halves = []
        for g in range(SSM_GROUPS):
            yg = y[:, g * (MIX_W // SSM_GROUPS):(g + 1) * (MIX_W // SSM_GROUPS)]
            halves.append(yg * lax.rsqrt(jnp.mean(yg * yg, axis=-1, keepdims=True) + SSM_NORM_EPS))
        ssm_rows.append(jnp.concatenate(halves, axis=1) * sng_ref[...])
    y_ssm = jnp.concatenate(ssm_rows, axis=0)

    _run_all(side)
    return x + _bdot(jnp.concatenate([y_ret, y_ssm], axis=1), woutb_sc[...])


def _ab_body(x_ref, xn_ref, rota_ref, rotb_ref, g_ref, win_ref, rdec_ref, rqfs_ref, rkte_ref, rcd_ref,
             cw_ref, cb_ref, dtb_ref, alog_ref, dexp_ref, sng_ref, exp_ref, tri_ref, wout_ref,
             o_ref, sret_sc, sssm_sc, ext_sc, winb_sc, wdt_sc, woutb_sc, pa_sc, pb_sc, *, tm):
    steps = functools.partial(_projection_steps, g_ref=g_ref, wt_sc=winb_sc, tail_sc=wdt_sc)

    @pl.when(pl.program_id(0) == 0)
    def _():
        sret_sc[...] = jnp.zeros_like(sret_sc)
        sssm_sc[...] = jnp.zeros_like(sssm_sc)
        ext_sc[...] = jnp.zeros_like(ext_sc)
        _cast_weight_rows(winb_sc, win_ref)
        _cast_weight(woutb_sc, wout_ref)
        wdt_sc[...] = jnp.zeros_like(wdt_sc)
        wdt_sc[0:SSM_HEADS, :] = win_ref[AB_DT:AB_DT + SSM_HEADS, :].astype(BF16)
        _run_all(steps(x_ref[0:tm, :], pa_sc))

    mix = functools.partial(
        _ab_mix, rotb_ref=rotb_ref, rdec_ref=rdec_ref, rqfs_ref=rqfs_ref, rkte_ref=rkte_ref, rcd_ref=rcd_ref,
        cw_ref=cw_ref, cb_ref=cb_ref, dtb_ref=dtb_ref, alog_ref=alog_ref, dexp_ref=dexp_ref, sng_ref=sng_ref,
        exp_ref=exp_ref, tri_ref=tri_ref, sret_sc=sret_sc, sssm_sc=sssm_sc, ext_sc=ext_sc, woutb_sc=woutb_sc,
        tm=tm)
    o_ref[0:tm, :] = mix(pa_sc, x_ref[0:tm, :], rota_ref[0], side=steps(x_ref[tm:2 * tm, :], pb_sc))
    o_ref[tm:2 * tm, :] = mix(pb_sc, x_ref[tm:2 * tm, :], rota_ref[1], side=steps(xn_ref[...], pa_sc))


def _rotary_tables(n_tiles, tm):
    inv_freq = ROPE_BASE ** (-np.arange(0, RET_DK, 2, dtype=np.float64) / RET_DK)
    two = lambda t: np.concatenate([t, t], axis=-1)
    sign = np.concatenate([-np.ones(RET_DK // 2), np.ones(RET_DK // 2)])
    ang_a = (np.arange(n_tiles, dtype=np.float64) * tm)[:, None] * inv_freq[None, :]
    ang_b = np.arange(tm, dtype=np.float64)[:, None] * inv_freq[None, :]
    cos_a, sin_a = two(np.cos(ang_a)), two(np.sin(ang_a))
    rot_a = np.zeros((n_tiles, SUBLANES, RET_DK))
    rot_a[:, 0], rot_a[:, 1], rot_a[:, 2], rot_a[:, 3] = cos_a, sin_a, sign * sin_a, sign * cos_a
    rot_b = np.concatenate([two(np.cos(ang_b)), two(np.sin(ang_b))], axis=1)
    return jnp.asarray(rot_a, F32), jnp.asarray(rot_b, F32)


def _retention_tables():
    c = RET_CHUNK
    log_g = np.log1p(-(2.0 ** (-5.0 - np.arange(RET_HEADS, dtype=np.float64))))
    pos = np.arange(c, dtype=np.float64)
    rel = pos[:, None] - pos[None, :]
    rdec = np.where(rel >= 0, np.exp(np.maximum(rel, 0.0)[None] * log_g[:, None, None]), 0.0)
    per_head = lambda t: np.repeat(t, RET_DK, axis=1)
    rkte = per_head(np.exp((c - 1 - pos)[:, None] * log_g[None, :]))
    rqfs = per_head(np.exp((pos + 1.0)[:, None] * log_g[None, :]))
    rcd = per_head(np.exp(c * log_g)[None, :])
    return tuple(jnp.asarray(t, F32) for t in (rdec, rqfs, rkte, rcd))


def _mix_ab(x, norm_g, w_in_all, w_out_all, j, conv_w, conv_b, dt_bias, a_log, d_skip, norm_ssm):
    t_len, d = x.shape
    tm = min(MIX_TM, t_len)
    rot_a, rot_b = _rotary_tables(t_len // tm, tm)
    rdec, rqfs, rkte, rcd = _retention_tables()

    pad_row = lambda v: jnp.pad(v.astype(F32), (0, LANES - v.shape[0])).reshape(1, LANES)
    expand = (jnp.arange(LANES)[:, None] == (jnp.arange(MIX_W)[None, :] // SSM_HEAD_DIM)).astype(BF16)
    expand = jnp.concatenate([expand] * 3, axis=0)
    t_idx = jnp.arange(tm)
    tri = (((t_idx[:, None] // SSM_CHUNK) == (t_idx[None, :] // SSM_CHUNK)) &
           (t_idx[:, None] >= t_idx[None, :])).astype(BF16)

    operands = [
        (x, pl.BlockSpec((2 * tm, d), lambda i: (i, 0))),
        (x, _next_tile_spec(tm, d, t_len // tm)),
        (rot_a, pl.BlockSpec((2, SUBLANES, RET_DK), lambda i: (i, 0, 0))),
        (rot_b, None),
        (norm_g.reshape(1, d), None),
        _layer_operand(jnp.swapaxes(w_in_all, 1, 2), j),
        (rdec, None), (rqfs, None), (rkte, None), (rcd, None),
        (conv_w, None), (conv_b.reshape(1, SSM_XBC), None),
        (pad_row(dt_bias), None), (pad_row(a_log), None),
        (jnp.repeat(d_skip.astype(F32), SSM_HEAD_DIM).reshape(1, MIX_W), None),
        (norm_ssm.reshape(1, MIX_W), None),
        (expand, None), (tri, None),
        _layer_operand(w_out_all, j),
    ]
    args = [a for a, _ in operands]
    specs = [s if s is not None else _resident_spec(a.shape) for a, s in operands]
    return pl.pallas_call(
        functools.partial(_ab_body, tm=tm),
        grid=(t_len // (2 * tm),),
        in_specs=specs,
        out_specs=pl.BlockSpec((2 * tm, d), lambda i: (i, 0)),
        out_shape=jax.ShapeDtypeStruct((t_len, d), F32),
        scratch_shapes=[
            pltpu.VMEM((RET_HEADS, RET_DK, RET_DK), F32),
            pltpu.VMEM((LANES, MIX_W), F32),
            pltpu.VMEM((SUBLANES, SSM_XBC), F32),
            pltpu.VMEM((AB_DT, d), BF16),
            pltpu.VMEM((LANES, d), BF16),
            pltpu.VMEM((2 * MIX_W, d), BF16),
            pltpu.VMEM((tm, AB_DT + LANES), F32),
            pltpu.VMEM((tm, AB_DT + LANES), F32),
        ],
        compiler_params=pltpu.CompilerParams(
            dimension_semantics=("arbitrary",), vmem_limit_bytes=VMEM_LIMIT),
        name="mix_ret_ssd",
    )(*args)


def _head_sum(x, ones_blk):
    tiles = [_sel_dot_lhs(x[:, j * LANES:(j + 1) * LANES], ones_blk, 2) for j in range(x.shape[1] // LANES)]
    return jnp.concatenate(tiles, axis=1)


def _cd_mix(p_ref, x, lcw_ref, lcb_ref, wa_ref, ba_ref, wx_ref, bx_ref, lam_ref,
            mu_ref, w0_ref, w2_ref, a0_ref, a2_ref, g2_ref, kk_ref, ka_ref, rk_ref, lng_ref, lnb_ref,
            ones_ref, sel_ref, ext_sc, hcar_sc, pcar_sc, s_sc, woutb_sc, tm, side):
    def proj(lo, width):
        return p_ref[:, lo:lo + width]

    rows = lax.broadcasted_iota(jnp.int32, (tm, 1), 0)

    xc = _causal_conv(ext_sc, proj(CD_XB, MIX_W), lcw_ref, lcb_ref, tm, LRU_CONV)
    _run_next(side)
    r_parts, i_parts = [], []
    for n in range(LRU_BLOCKS):
        ls = slice(n * LRU_BLOCK, (n + 1) * LRU_BLOCK)
        r_parts.append(jax.nn.sigmoid(_bdot(xc[:, ls], wa_ref[n]) + ba_ref[:, ls]))
        i_parts.append(jax.nn.sigmoid(_bdot(xc[:, ls], wx_ref[n]) + bx_ref[:, ls]))
    r_gate = jnp.concatenate(r_parts, axis=1)
    i_gate = jnp.concatenate(i_parts, axis=1)
    log_a = -LRU_C * r_gate * _softplus(-lam_ref[...])
    a_cum = jnp.exp(log_a)
    hs = _sqrt_nonneg(-jnp.tanh(log_a) * (1.0 + a_cum * a_cum)) * (i_gate * xc)
    _run_next(side)
    in_group = rows % SUBLANES
    shift = 1
    while shift < SUBLANES:
        valid = in_group >= shift
        a_prev = jnp.where(valid, pltpu.roll(a_cum, shift, axis=0), 1.0)
        h_prev = jnp.where(valid, pltpu.roll(hs, shift, axis=0), 0.0)
        hs = hs + a_cum * h_prev
        a_cum = a_cum * a_prev
        shift *= 2
    carry = hcar_sc[...]
    groups = []
    for gi in range(tm // SUBLANES):
        grp = slice(gi * SUBLANES, (gi + 1) * SUBLANES)
        hg = hs[grp] + a_cum[grp] * carry
        groups.append(hg)
        carry = hg[SUBLANES - 1:SUBLANES, :]
    hs = jnp.concatenate(groups, axis=0)
    hcar_sc[...] = carry
    _run_next(side)
    y_lru = hs * _gelu_tanh(proj(CD_GB, MIX_W))

    pr = proj(CD_PR, CD_PR_COLS)
    prev = pltpu.roll(pr, 1, axis=0)
    row8 = lax.broadcasted_iota(jnp.int32, (SUBLANES, 1), 0)
    prev = jnp.concatenate([jnp.where(row8 == 0, pcar_sc[...], prev[0:SUBLANES]), prev[SUBLANES:]], axis=0)
    pcar_sc[...] = pr[tm - 1:tm, :]
    ps = pr + (prev - pr) * mu_ref[...]
    _run_next(side)
    r = ps[:, 0:MIX_W]
    k = ps[:, MIX_W:2 * MIX_W]
    v = ps[:, 2 * MIX_W:3 * MIX_W]
    wl = ps[:, 3 * MIX_W:3 * MIX_W + LANES]
    al = ps[:, 3 * MIX_W + LANES:3 * MIX_W + 2 * LANES]
    gl = ps[:, 3 * MIX_W + 2 * LANES:3 * MIX_W + 4 * LANES]
    log_w = -math.exp(-0.5) * jax.nn.sigmoid(w0_ref[...] + _bdot(jnp.tanh(wl), w2_ref[...]))
    a = jax.nn.sigmoid(a0_ref[...] + _bdot(al, a2_ref[...]))
    g = _bdot(jax.nn.sigmoid(gl), g2_ref[...])
    _run_next(side)
    ones_blk = ones_ref[...]
    kk = k * kk_ref[...]
    kk = kk * lax.rsqrt(jnp.maximum(_head_sum(kk * kk, ones_blk), 1e-24))
    k = k * (1.0 + (a - 1.0) * ka_ref[...])
    c_len = RWKV_CHUNK
    pair = 2 * c_len
    n_chunks = tm // c_len
    n_pairs = MIX_W // LANES

    cs = _sel_dot_rhs(sel_ref[...], log_w, 3)
    last = [cs[(c + 1) * c_len - 1:(c + 1) * c_len, :] for c in range(n_chunks)]
    tot = jnp.concatenate([jnp.broadcast_to(t, (c_len, MIX_W)) for t in last], axis=0)
    e_neg = jnp.exp(-cs)
    e_end = jnp.exp(tot - cs)
    nb = -(kk * a)
    bt = kk * jnp.exp(cs - log_w)
    rt = r * jnp.exp(cs)
    ab = nb * e_neg
    kb = k * e_neg
    ae = nb * e_end
    ke = k * e_end

    lane = lax.broadcasted_iota(jnp.int32, (1, LANES), 1)
    m0 = lane < RWKV_HEAD_DIM
    trow = lax.broadcasted_iota(jnp.int32, (c_len, LANES), 0)
    scol = lax.broadcasted_iota(jnp.int32, (c_len, LANES), 1) % c_len
    strict = trow > scol
    incl = trow >= scol
    eye = (trow == scol).astype(F32)

    def merge_mask(s):
        return ((trow // s) == (scol // s) + 1) & ((trow // s) % 2 == 1)

    def blockdiag(t):
        return jnp.concatenate([jnp.where(m0, t, 0.0), jnp.where(m0, 0.0, t)], axis=0)

    def tile(t, c, j):
        return t[c * c_len:(c + 1) * c_len, j * LANES:(j + 1) * LANES]

    units = [(c, j) for c in range(n_chunks) for j in range(n_pairs)]
    vbd = {u: blockdiag(tile(v, *u)).astype(BF16) for u in units}
    n_mat, lhs_x, lhs_y, lhs_z, w_col = {}, {}, {}, {}, {}
    for u in units:
        bt_u = tile(bt, *u).astype(BF16)
        rt_u = tile(rt, *u).astype(BF16)
        gram = _bdot_nt(jnp.concatenate([bt_u, rt_u], axis=0),
                        jnp.concatenate([blockdiag(tile(ab, *u)), blockdiag(tile(kb, *u))], axis=0))
        n_mat[u] = jnp.where(strict, gram[0:c_len, 0:LANES], 0.0)
        lhs_x[u] = jnp.concatenate(
            [bt_u, jnp.where(strict, gram[0:c_len, LANES:2 * LANES], 0.0).astype(BF16)], axis=1)
        lhs_y[u] = jnp.concatenate(
            [rt_u, jnp.where(incl, gram[c_len:pair, 0:LANES], 0.0).astype(BF16),
             jnp.where(incl, gram[c_len:pair, LANES:2 * LANES], 0.0).astype(BF16)], axis=1)
        lhs_z[u] = jnp.concatenate(
            [blockdiag(tile(ae, *u)).T.astype(BF16), blockdiag(tile(ke, *u)).T.astype(BF16)], axis=1)
        c, j = u
        w_row = jnp.exp(last[c][:, j * LANES:(j + 1) * LANES])
        w_col[u] = jnp.broadcast_to(w_row, (pair, LANES)).T

    n_bd = {u: blockdiag(n_mat[u]).astype(BF16) for u in units}
    t_inv = {u: eye + jnp.where(merge_mask(1), n_mat[u], 0.0) for u in units}
    s = 2
    while s < c_len:
        mask = merge_mask(s)
        half = {u: jnp.dot(t_inv[u].astype(BF16), n_bd[u], preferred_element_type=F32) for u in units}
        t_inv = {u: t_inv[u] + jnp.where(mask, _bdot(half[u], blockdiag(t_inv[u])), 0.0) for u in units}
        s *= 2
    t_inv = {u: t_inv[u].astype(BF16) for u in units}

    y_rows = []
    for c in range(n_chunks):
        z_prev = [s_sc[j] for j in range(n_pairs)]
        zb = [z.astype(BF16) for z in z_prev]
        xs = [jnp.dot(lhs_x[(c, j)], jnp.concatenate([zb[j], vbd[(c, j)]], axis=0),
                      preferred_element_type=F32) for j in range(n_pairs)]
        _run_next(side)
        us = [jnp.dot(t_inv[(c, j)], blockdiag(xs[j]).astype(BF16), preferred_element_type=F32)
              for j in range(n_pairs)]
        _run_next(side)
        us = [blockdiag(u).astype(BF16) for u in us]
        tiles = []
        for j in range(n_pairs):
            tiles.append(jnp.dot(lhs_y[(c, j)], jnp.concatenate([zb[j], us[j], vbd[(c, j)]], axis=0),
                                 preferred_element_type=F32))
            s_sc[j] = z_prev[j] * w_col[(c, j)] + jnp.dot(
                lhs_z[(c, j)], jnp.concatenate([us[j], vbd[(c, j)]], axis=0), preferred_element_type=F32)
        y_rows.append(jnp.concatenate(tiles, axis=1))
    y = jnp.concatenate(y_rows, axis=0)
    inv_n = 1.0 / RWKV_HEAD_DIM
    mean = _head_sum(y, ones_blk) * inv_n
    yc = y - mean
    var = _head_sum(yc * yc, ones_blk) * inv_n
    y = yc * lax.rsqrt(var + RWKV_LN_EPS) * lng_ref[...] + lnb_ref[...]
    y = y + _head_sum(r * k * rk_ref[...], ones_blk) * v
    y_rwkv = y * g

    _run_all(side)
    return x + _bdot(jnp.concatenate([y_lru, y_rwkv], axis=1), woutb_sc[...])


def _cd_body(x_ref, xn_ref, g_ref, win_ref, lcw_ref, lcb_ref, wa_ref, ba_ref, wx_ref, bx_ref, lam_ref,
             mu_ref, w0_ref, w2_ref, a0_ref, a2_ref, g2_ref, kk_ref, ka_ref, rk_ref, lng_ref, lnb_ref,
             ones_ref, sel_ref, wout_ref, o_ref,
             ext_sc, hcar_sc, pcar_sc, s_sc, winb_sc, wlora_sc, woutb_sc, pa_sc, pb_sc, *, tm):
    steps = functools.partial(_projection_steps, g_ref=g_ref, wt_sc=winb_sc, tail_sc=wlora_sc)

    @pl.when(pl.program_id(0) == 0)
    def _():
        ext_sc[...] = jnp.zeros_like(ext_sc)
        hcar_sc[...] = jnp.zeros_like(hcar_sc)
        pcar_sc[...] = jnp.zeros_like(pcar_sc)
        s_sc[...] = jnp.zeros_like(s_sc)
        _cast_weight_rows(winb_sc, win_ref)
        _cast_weight(woutb_sc, wout_ref)
        wlora_sc[...] = jnp.zeros_like(wlora_sc)
        src = CD_LORA
        for dst, width in ((0, DECAY_LORA), (LANES, ICL_LORA), (2 * LANES, GATE_LORA)):
            wlora_sc[dst:dst + width, :] = win_ref[src:src + width, :].astype(BF16)
            src += width
        _run_all(steps(x_ref[0:tm, :], pa_sc))

    mix = functools.partial(
        _cd_mix, lcw_ref=lcw_ref, lcb_ref=lcb_ref, wa_ref=wa_ref, ba_ref=ba_ref, wx_ref=wx_ref, bx_ref=bx_ref,
        lam_ref=lam_ref, mu_ref=mu_ref, w0_ref=w0_ref, w2_ref=w2_ref, a0_ref=a0_ref, a2_ref=a2_ref, g2_ref=g2_ref,
        kk_ref=kk_ref, ka_ref=ka_ref, rk_ref=rk_ref, lng_ref=lng_ref, lnb_ref=lnb_ref, ones_ref=ones_ref,
        sel_ref=sel_ref, ext_sc=ext_sc, hcar_sc=hcar_sc, pcar_sc=pcar_sc, s_sc=s_sc, woutb_sc=woutb_sc, tm=tm)
    o_ref[0:tm, :] = mix(pa_sc, x_ref[0:tm, :], side=steps(x_ref[tm:2 * tm, :], pb_sc))
    o_ref[tm:2 * tm, :] = mix(pb_sc, x_ref[tm:2 * tm, :], side=steps(xn_ref[...], pa_sc))


def _mix_cd(x, norm_g, w_in_all, w_out_all, j, lru_conv_w, lru_conv_b, lru_wa, lru_ba, lru_wx, lru_bx,
            lru_lambda, mu, w0, w2, a0, a2, g2, k_k, k_a, r_k, ln_g, ln_b):
    t_len, d = x.shape
    tm = min(MIX_TM, t_len)

    def pad_lora(t, lo, width, padded):
        return jnp.pad(t[..., lo:lo + width], [(0, 0)] * (t.ndim - 1) + [(0, padded - width)])

    def lora_cols(t, base):
        return jnp.concatenate([
            pad_lora(t, base, DECAY_LORA, LANES),
            pad_lora(t, base + DECAY_LORA, ICL_LORA, LANES),
            pad_lora(t, base + DECAY_LORA + ICL_LORA, GATE_LORA, 2 * LANES)], axis=-1)

    mu_row = mu.reshape(1, -1)
    mu_pad = jnp.concatenate([mu_row[:, :3 * MIX_W], lora_cols(mu_row, 3 * MIX_W)], axis=-1)
    pad_rows = lambda t, n: jnp.pad(t, ((0, n - t.shape[0]), (0, 0))).astype(BF16)
    row = lambda t: t.reshape(1, -1).astype(F32)
    ones_blk = ((jnp.arange(LANES)[:, None] // RWKV_HEAD_DIM) ==
                (jnp.arange(LANES)[None, :] // RWKV_HEAD_DIM)).astype(BF16)
    ones_blk = jnp.concatenate([ones_blk] * 2, axis=0)
    t_idx = jnp.arange(tm)
    same_chunk = (t_idx[:, None] // RWKV_CHUNK) == (t_idx[None, :] // RWKV_CHUNK)
    sel = (same_chunk & (t_idx[:, None] >= t_idx[None, :])).astype(BF16)

    operands = [
        (x, pl.BlockSpec((2 * tm, d), lambda i: (i, 0))),
        (x, _next_tile_spec(tm, d, t_len // tm)),
        (row(norm_g), None),
        _layer_operand(jnp.swapaxes(w_in_all, 1, 2), j),
        (lru_conv_w, None), (row(lru_conv_b), None),
        (lru_wa.astype(BF16), None), (row(lru_ba), None),
        (lru_wx.astype(BF16), None), (row(lru_bx), None),
        (row(lru_lambda), None),
        (mu_pad, None),
        (row(w0), None), (pad_rows(w2, LANES), None),
        (row(a0), None), (pad_rows(a2, LANES), None),
        (pad_rows(g2, 2 * LANES), None),
        (row(k_k), None), (row(k_a), None), (row(r_k), None), (row(ln_g), None), (row(ln_b), None),
        (ones_blk, None), (sel, None),
        _layer_operand(w_out_all, j),
    ]
    args = [a for a, _ in operands]
    specs = [s if s is not None else _resident_spec(a.shape) for a, s in operands]
    return pl.pallas_call(
        functools.partial(_cd_body, tm=tm),
        grid=(t_len // (2 * tm),),
        in_specs=specs,
        out_specs=pl.BlockSpec((2 * tm, d), lambda i: (i, 0)),
        out_shape=jax.ShapeDtypeStruct((t_len, d), F32),
        scratch_shapes=[
            pltpu.VMEM((SUBLANES, MIX_W), F32),
            pltpu.VMEM((1, MIX_W), F32),
            pltpu.VMEM((1, CD_PR_COLS), F32),
            pltpu.VMEM((MIX_W // LANES, LANES, LANES), F32),
            pltpu.VMEM((CD_LORA, d), BF16),
            pltpu.VMEM((CD_PR + CD_PR_COLS - CD_LORA, d), BF16),
            pltpu.VMEM((2 * MIX_W, d), BF16),
            pltpu.VMEM((tm, CD_PR + CD_PR_COLS), F32),
            pltpu.VMEM((tm, CD_PR + CD_PR_COLS), F32),
        ],
        compiler_params=pltpu.CompilerParams(
            dimension_semantics=("arbitrary",), vmem_limit_bytes=VMEM_LIMIT),
        name="mix_lru_rwkv",
    )(*args)


def kernel(x, ffn1_norm, ffn1_wg, ffn1_wu, ffn1_wd, mix_norm, ffn2_norm, ffn2_wg, ffn2_wu, ffn2_wd,
           ab_w_in, ab_w_out, ssm_conv_w, ssm_conv_b, ssm_dt_bias, ssm_a_log, ssm_d, ssm_norm,
           cd_w_in, cd_w_out, lru_conv_w, lru_conv_b, lru_wa, lru_ba, lru_wx, lru_bx, lru_lambda,
           rwkv_mu, rwkv_w0, rwkv_w2, rwkv_a0, rwkv_a2, rwkv_g2, rwkv_kk, rwkv_ka, rwkv_rk,
           rwkv_ln_g, rwkv_ln_b, final_norm):
    bsz, t_len, d = x.shape
    depth = ffn1_norm.shape[0]
    outs = []
    for b in range(bsz):
        xb = x.reshape(t_len, d) if bsz == 1 else x[b]
        for layer in range(depth):
            j = layer // 2
            xb = _ffn(xb, ffn1_norm[layer], ffn1_wg, ffn1_wu, ffn1_wd, layer, final_norm, final_norm=False)
            if layer % 2 == 0:
                xb = _mix_ab(xb, mix_norm[layer], ab_w_in, ab_w_out, j, ssm_conv_w[j], ssm_conv_b[j],
                             ssm_dt_bias[j], ssm_a_log[j], ssm_d[j], ssm_norm[j])
            else:
                xb = _mix_cd(xb, mix_norm[layer], cd_w_in, cd_w_out, j, lru_conv_w[j], lru_conv_b[j], lru_wa[j],
                             lru_ba[j], lru_wx[j], lru_bx[j], lru_lambda[j], rwkv_mu[j], rwkv_w0[j],
                             rwkv_w2[j], rwkv_a0[j], rwkv_a2[j], rwkv_g2[j], rwkv_kk[j], rwkv_ka[j],
                             rwkv_rk[j].reshape(-1), rwkv_ln_g[j], rwkv_ln_b[j])
            xb = _ffn(xb, ffn2_norm[layer], ffn2_wg, ffn2_wu, ffn2_wd, layer, final_norm,
                      final_norm=(layer == depth - 1))
        outs.append(xb)
    return outs[0].reshape(1, t_len, d) if bsz == 1 else jnp.stack(outs, axis=0)
```

```python
import functools
import math

import jax
import jax.numpy as jnp
import numpy as np
from jax import lax
from jax.experimental import pallas as pl
from jax.experimental.pallas import tpu as pltpu

F32 = jnp.float32
BF16 = jnp.bfloat16

D_FF = 2816
NORM_EPS = 1e-6
MIX_W = 512

RET_HEADS = 4
RET_DK = 128
RET_CHUNK = 128
ROPE_BASE = 10000.0
HEAD_NORM_EPS = 1e-6

SSM_HEADS = 8
SSM_HEAD_DIM = 64
SSM_GROUPS = 2
SSM_STATE = 64
SSM_CONV = 4
SSM_CHUNK = 128
SSM_XBC = 768
SSM_NORM_EPS = 1e-5

LRU_BLOCKS = 4
LRU_BLOCK = 128
LRU_CONV = 4
LRU_C = 8.0

RWKV_HEAD_DIM = 64
DECAY_LORA = 64
ICL_LORA = 64
GATE_LORA = 160
RWKV_LN_EPS = 64e-5
RWKV_CHUNK = 64

LANES = 128
SUBLANES = 8
VMEM_LIMIT = 56 * 1024 * 1024

FFN_TM = 512
FFN_TF = 256
MXU_DIM = 256
FFN_TK = (0, 6 * MXU_DIM, D_FF)
MIX_TM = 256
PROJ_TN = 256

AB_Q, AB_K, AB_V, AB_G, AB_Z, AB_XBC, AB_DT = 0, 512, 1024, 1536, 2048, 2560, 3328
CD_XB, CD_GB, CD_PR, CD_LORA = 0, 512, 1024, 2560
CD_PR_COLS = 2048


def _bdot(a, b):
    return jnp.dot(a.astype(BF16), b.astype(BF16), preferred_element_type=F32)


def _bdot_nt(a, b):
    return lax.dot_general(a.astype(BF16), b.astype(BF16), (((1,), (1,)), ((), ())),
                           preferred_element_type=F32)


def _bdot_tn(a, b):
    return lax.dot_general(a.astype(BF16), b.astype(BF16), (((0,), (0,)), ((), ())),
                           preferred_element_type=F32)


def _split(x, pieces):
    out = []
    for _ in range(pieces - 1):
        p = x.astype(BF16)
        out.append(p)
        x = x - p.astype(F32)
    out.append(x.astype(BF16))
    return out


def _sel_dot_lhs(x, sel_stack, pieces):
    return jnp.dot(jnp.concatenate(_split(x, pieces), axis=1), sel_stack, preferred_element_type=F32)


def _sel_dot_rhs(sel, x, pieces):
    n = x.shape[1]
    res = jnp.dot(sel, jnp.concatenate(_split(x, pieces), axis=1), preferred_element_type=F32)
    out = res[:, 0:n]
    for p in range(1, pieces):
        out = out + res[:, p * n:(p + 1) * n]
    return out


def _rms_norm(x, g, eps):
    return x * lax.rsqrt(jnp.mean(x * x, axis=-1, keepdims=True) + eps) * g


def _silu(x):
    return x * jax.nn.sigmoid(x)


def _softplus(x):
    return jnp.maximum(x, 0.0) + jnp.log1p(jnp.exp(-jnp.abs(x)))


def _sqrt_nonneg(y):
    return y * lax.rsqrt(jnp.maximum(y, jnp.finfo(jnp.float32).tiny))


def _gelu_tanh(x):
    c = math.sqrt(2.0 / math.pi)
    return 0.5 * x * (1.0 + jnp.tanh(c * (x + 0.044715 * (x * x * x))))


def _causal_conv(tail_ref, x, w_ref, b_ref, tm, width):
    tail = tail_ref[...]
    row = lax.broadcasted_iota(jnp.int32, (SUBLANES, 1), 0)
    acc = b_ref[...] + w_ref[width - 1:width, :] * x
    for s in range(1, width):
        delayed = pltpu.roll(x, s, axis=0)
        head = jnp.where(row < s, pltpu.roll(tail, s, axis=0), delayed[0:SUBLANES])
        delayed = jnp.concatenate([head, delayed[SUBLANES:]], axis=0)
        acc = acc + w_ref[width - 1 - s:width - s, :] * delayed
    tail_ref[...] = x[tm - SUBLANES:tm]
    return acc


def _ffn_body(x_ref, g_ref, wg_ref, wu_ref, wd_ref, fin_ref, o_ref, act_sc, *, final_norm):
    x = x_ref[...]
    h = _rms_norm(x, g_ref[...], NORM_EPS).astype(BF16)
    for j in range(D_FF // FFN_TF):
        cols = slice(j * FFN_TF, (j + 1) * FFN_TF)
        gate = jnp.dot(h, wg_ref[:, cols].astype(BF16), preferred_element_type=F32)
        up = jnp.dot(h, wu_ref[:, cols].astype(BF16), preferred_element_type=F32)
        act_sc[:, cols] = (_silu(gate) * up).astype(BF16)
    down = None
    for lo, hi in zip(FFN_TK[:-1], FFN_TK[1:]):
        part = jnp.dot(act_sc[:, lo:hi], wd_ref[lo:hi, :].astype(BF16), preferred_element_type=F32)
        down = part if down is None else down + part
    y = x + 0.5 * down
    if final_norm:
        y = _rms_norm(y, fin_ref[...], NORM_EPS)
    o_ref[...] = y


def _resident_spec(shape):
    nd = len(shape)
    return pl.BlockSpec(shape, lambda i, _nd=nd: (0,) * _nd, pipeline_mode=pl.Buffered(1))


def _next_tile_spec(tm, d, n_tiles):
    return pl.BlockSpec((tm, d), lambda i: (jnp.minimum(2 * i + 2, n_tiles - 1), 0))


def _layer_spec(shape, layer):
    nd = len(shape)
    return pl.BlockSpec((None,) + tuple(shape), lambda i, _nd=nd: (layer,) + (0,) * _nd,
                        pipeline_mode=pl.Buffered(1))


def _layer_operand(stack, layer):
    if stack.shape[0] == 1:
        return stack.reshape(stack.shape[1:]), _resident_spec(stack.shape[1:])
    return stack, _layer_spec(stack.shape[1:], layer)


def _ffn(x, norm_g, wg, wu, wd, layer, fin_g, *, final_norm):
    t_len, d = x.shape
    tm = min(FFN_TM, t_len)
    return pl.pallas_call(
        functools.partial(_ffn_body, final_norm=final_norm),
        grid=(t_len // tm,),
        in_specs=[
            pl.BlockSpec((tm, d), lambda i: (i, 0)),
            _resident_spec((1, d)),
            _layer_spec((d, D_FF), layer),
            _layer_spec((d, D_FF), layer),
            _layer_spec((D_FF, d), layer),
            _resident_spec((1, d)),
        ],
        out_specs=pl.BlockSpec((tm, d), lambda i: (i, 0)),
        out_shape=jax.ShapeDtypeStruct((t_len, d), F32),
        scratch_shapes=[pltpu.VMEM((tm, D_FF), BF16)],
        compiler_params=pltpu.CompilerParams(
            dimension_semantics=("arbitrary",), vmem_limit_bytes=VMEM_LIMIT),
        name="ffn_final" if final_norm else "ffn",
    )(x, norm_g.reshape(1, d), wg, wu, wd, fin_g.reshape(1, d))


def _cast_weight(dst_sc, src_ref):
    for lo in range(0, dst_sc.shape[1], MXU_DIM):
        hi = min(lo + MXU_DIM, dst_sc.shape[1])
        dst_sc[:, lo:hi] = src_ref[:, lo:hi].astype(BF16)


def _cast_weight_rows(dst_sc, src_ref):
    for lo in range(0, dst_sc.shape[0], MXU_DIM):
        hi = min(lo + MXU_DIM, dst_sc.shape[0])
        dst_sc[lo:hi, :] = src_ref[lo:hi, :].astype(BF16)


def _projection_steps(x, p_ref, g_ref, wt_sc, tail_sc):
    h = _rms_norm(x, g_ref[...], NORM_EPS).astype(BF16)
    main = wt_sc.shape[0]
    nt_dot = lambda w: lax.dot_general(h, w, (((1,), (1,)), ((), ())), preferred_element_type=F32)

    def section(lo, hi):
        p_ref[:, lo:hi] = nt_dot(wt_sc[lo:hi, :])

    def tail():
        p_ref[:, main:main + tail_sc.shape[0]] = nt_dot(tail_sc[...])

    steps = [functools.partial(section, lo, min(lo + PROJ_TN, main)) for lo in range(0, main, PROJ_TN)]
    return steps + [tail]


def _run_next(steps):
    if steps:
        steps.pop(0)()


def _run_all(steps):
    while steps:
        _run_next(steps)


def _ab_mix(p_ref, x, rota, rotb_ref, rdec_ref, rqfs_ref, rkte_ref, rcd_ref,
            cw_ref, cb_ref, dtb_ref, alog_ref, dexp_ref, sng_ref, exp_ref, tri_ref,
            sret_sc, sssm_sc, ext_sc, woutb_sc, tm, side):
    def proj(lo, width):
        return p_ref[:, lo:lo + width]

    cos_a, sin_a = rota[0:1, :], rota[1:2, :]
    sin_a_sgn, cos_a_sgn = rota[2:3, :], rota[3:4, :]
    cos_b, sin_b = rotb_ref[:, 0:RET_DK], rotb_ref[:, RET_DK:2 * RET_DK]
    cos2 = cos_a * cos_b - sin_a * sin_b
    sin2 = sin_a_sgn * cos_b + cos_a_sgn * sin_b

    q_all = proj(AB_Q, MIX_W)
    k_all = proj(AB_K, MIX_W)
    v_all = proj(AB_V, MIX_W)
    gate_ret = _silu(proj(AB_G, MIX_W))
    scale = RET_DK ** -0.5
    c_len = RET_CHUNK
    n_chunks = tm // c_len
    cells = [(c, hh) for c in range(n_chunks) for hh in range(RET_HEADS)]
    rows_of = lambda c: slice(c * c_len, (c + 1) * c_len)
    lanes_of = lambda hh: slice(hh * RET_DK, (hh + 1) * RET_DK)

    def rotate(t, c, hh):
        t = t[rows_of(c), lanes_of(hh)]
        return t * cos2[rows_of(c)] + pltpu.roll(t, RET_DK // 2, axis=1) * sin2[rows_of(c)]

    qh = {u: rotate(q_all, *u) for u in cells}
    kh = {u: rotate(k_all, *u) * scale for u in cells}
    vh = {(c, hh): v_all[rows_of(c), lanes_of(hh)].astype(BF16) for c, hh in cells}
    scores = {(c, hh): _bdot_nt(qh[(c, hh)], kh[(c, hh)]) * rdec_ref[hh] for c, hh in cells}
    kv = {(c, hh): _bdot_tn(kh[(c, hh)] * rkte_ref[:, lanes_of(hh)], vh[(c, hh)]) for c, hh in cells}
    state = {}
    for hh in range(RET_HEADS):
        s_cur = sret_sc[hh]
        for c in range(n_chunks):
            state[(c, hh)] = s_cur
            s_cur = s_cur * rcd_ref[:, lanes_of(hh)] + kv[(c, hh)]
        sret_sc[hh] = s_cur
    outs = {(c, hh): _bdot(jnp.concatenate([scores[(c, hh)], qh[(c, hh)] * rqfs_ref[:, lanes_of(hh)]], axis=1),
                           jnp.concatenate([vh[(c, hh)], state[(c, hh)].astype(BF16)], axis=0))
            for c, hh in cells}
    outs = {u: o * lax.rsqrt(jnp.mean(o * o, axis=-1, keepdims=True) + HEAD_NORM_EPS) for u, o in outs.items()}
    y_ret = gate_ret * jnp.concatenate(
        [jnp.concatenate([outs[(c, hh)] for hh in range(RET_HEADS)], axis=1) for c in range(n_chunks)], axis=0)

    z = proj(AB_Z, MIX_W)
    xbc = _silu(_causal_conv(ext_sc, proj(AB_XBC, SSM_XBC), cw_ref, cb_ref, tm, SSM_CONV))
    xs = xbc[:, :MIX_W]
    bm = xbc[:, MIX_W:MIX_W + LANES]
    cm = xbc[:, MIX_W + LANES:MIX_W + 2 * LANES]
    lane = lax.broadcasted_iota(jnp.int32, (1, LANES), 1)
    dt = _softplus(proj(AB_DT, LANES) + dtb_ref[...])
    a_neg = jnp.where(lane < SSM_HEADS, -jnp.exp(alog_ref[...]), 0.0)
    adt = dt * a_neg
    xdt = xs * _sel_dot_lhs(dt, exp_ref[0:2 * LANES, :], 2)
    acs_all = _sel_dot_rhs(tri_ref[...], adt, 3)
    acs_e_all = _sel_dot_lhs(acs_all, exp_ref[...], 3)
    l_len = SSM_CHUNK
    row = lax.broadcasted_iota(jnp.int32, (l_len, l_len), 0)
    col = lax.broadcasted_iota(jnp.int32, (l_len, l_len), 1)
    causal = row >= col
    first_half = lax.broadcasted_iota(jnp.int32, (1, LANES), 1) < SSM_HEAD_DIM
    grp_mask = [first_half, jnp.logical_not(first_half)]
    srow = lax.broadcasted_iota(jnp.int32, (LANES, MIX_W), 0)
    scol = lax.broadcasted_iota(jnp.int32, (LANES, MIX_W), 1)
    state_mask = (srow // SSM_STATE) == (scol // (MIX_W // SSM_GROUPS))
    s_cur = sssm_sc[...]
    s_enter = []
    for c in range(tm // l_len):
        rs = slice(c * l_len, (c + 1) * l_len)
        last_e = acs_e_all[(c + 1) * l_len - 1:(c + 1) * l_len, :]
        upd = _bdot_tn(bm[rs], xdt[rs] * jnp.exp(last_e - acs_e_all[rs]))
        s_enter.append(s_cur)
        s_cur = s_cur * jnp.exp(last_e) + jnp.where(state_mask, upd, 0.0)
    sssm_sc[...] = s_cur

    ssm_rows = []
    for c in range(tm // l_len):
        rs = slice(c * l_len, (c + 1) * l_len)
        acs = acs_all[rs]
        acs_e = acs_e_all[rs]
        acs_t = acs.T
        bm_c = bm[rs]
        cm_c = cm[rs]
        y_off = _bdot(cm_c, s_enter[c]) * jnp.exp(acs_e)
        cb = [jnp.where(causal, _bdot_nt(jnp.where(grp_mask[g], cm_c, 0.0), bm_c), 0.0)
              for g in range(SSM_GROUPS)]
        tiles = []
        for j in range(MIX_W // LANES):
            xt = xdt[rs, j * LANES:(j + 1) * LANES]
            weights = []
            for hh in (2 * j, 2 * j + 1):
                seg = acs[:, hh:hh + 1] - acs_t[hh:hh + 1, :]
                weights.append((cb[hh // (SSM_HEADS // SSM_GROUPS)] * jnp.exp(jnp.minimum(seg, 0.0))).astype(BF16))
                _run_next(side)
            tiles.append(jnp.dot(
                jnp.concatenate(weights, axis=1),
                jnp.concatenate([jnp.where(first_half, xt, 0.0), jnp.where(first_half, 0.0, xt)], axis=0).astype(BF16),
                preferred_element_type=F32))
        y_diag = jnp.concatenate(tiles, axis=1)
        y = y_diag + y_off + dexp_ref[...] * xs[rs]
        y = y * _silu(z[rs])
        halves = []
        for g in range(SSM_GROUPS):
            yg = y[:, g * (MIX_W // SSM_GROUPS):(g + 1) * (MIX_W // SSM_GROUPS)]
            halves.append(yg * lax.rsqrt(jnp.mean(yg * yg, axis=-1, keepdims=True) + SSM_NORM_EPS))
        ssm_rows.append(jnp.concatenate(halves, axis=1) * sng_ref[...])
    y_ssm = jnp.concatenate(ssm_rows, axis=0)

    _run_all(side)
    return x + _bdot(jnp.concatenate([y_ret, y_ssm], axis=1), woutb_sc[...])


def _ab_body(x_ref, xn_ref, rota_ref, rotb_ref, g_ref, win_ref, rdec_ref, rqfs_ref, rkte_ref, rcd_ref,
             cw_ref, cb_ref, dtb_ref, alog_ref, dexp_ref, sng_ref, exp_ref, tri_ref, wout_ref,
             o_ref, sret_sc, sssm_sc, ext_sc, winb_sc, wdt_sc, woutb_sc, pa_sc, pb_sc, *, tm):
    steps = functools.partial(_projection_steps, g_ref=g_ref, wt_sc=winb_sc, tail_sc=wdt_sc)

    @pl.when(pl.program_id(0) == 0)
    def _():
        sret_sc[...] = jnp.zeros_like(sret_sc)
        sssm_sc[...] = jnp.zeros_like(sssm_sc)
        ext_sc[...] = jnp.zeros_like(ext_sc)
        _cast_weight_rows(winb_sc, win_ref)
        _cast_weight(woutb_sc, wout_ref)
        wdt_sc[...] = jnp.zeros_like(wdt_sc)
        wdt_sc[0:SSM_HEADS, :] = win_ref[AB_DT:AB_DT + SSM_HEADS, :].astype(BF16)
        _run_all(steps(x_ref[0:tm, :], pa_sc))

    mix = functools.partial(
        _ab_mix, rotb_ref=rotb_ref, rdec_ref=rdec_ref, rqfs_ref=rqfs_ref, rkte_ref=rkte_ref, rcd_ref=rcd_ref,
        cw_ref=cw_ref, cb_ref=cb_ref, dtb_ref=dtb_ref, alog_ref=alog_ref, dexp_ref=dexp_ref, sng_ref=sng_ref,
        exp_ref=exp_ref, tri_ref=tri_ref, sret_sc=sret_sc, sssm_sc=sssm_sc, ext_sc=ext_sc, woutb_sc=woutb_sc,
        tm=tm)
    o_ref[0:tm, :] = mix(pa_sc, x_ref[0:tm, :], rota_ref[0], side=steps(x_ref[tm:2 * tm, :], pb_sc))
    o_ref[tm:2 * tm, :] = mix(pb_sc, x_ref[tm:2 * tm, :], rota_ref[1], side=steps(xn_ref[...], pa_sc))


def _rotary_tables(n_tiles, tm):
    inv_freq = ROPE_BASE ** (-np.arange(0, RET_DK, 2, dtype=np.float64) / RET_DK)
    two = lambda t: np.concatenate([t, t], axis=-1)
    sign = np.concatenate([-np.ones(RET_DK // 2), np.ones(RET_DK // 2)])
    ang_a = (np.arange(n_tiles, dtype=np.float64) * tm)[:, None] * inv_freq[None, :]
    ang_b = np.arange(tm, dtype=np.float64)[:, None] * inv_freq[None, :]
    cos_a, sin_a = two(np.cos(ang_a)), two(np.sin(ang_a))
    rot_a = np.zeros((n_tiles, SUBLANES, RET_DK))
    rot_a[:, 0], rot_a[:, 1], rot_a[:, 2], rot_a[:, 3] = cos_a, sin_a, sign * sin_a, sign * cos_a
    rot_b = np.concatenate([two(np.cos(ang_b)), two(np.sin(ang_b))], axis=1)
    return jnp.asarray(rot_a, F32), jnp.asarray(rot_b, F32)


def _retention_tables():
    c = RET_CHUNK
    log_g = np.log1p(-(2.0 ** (-5.0 - np.arange(RET_HEADS, dtype=np.float64))))
    pos = np.arange(c, dtype=np.float64)
    rel = pos[:, None] - pos[None, :]
    rdec = np.where(rel >= 0, np.exp(np.maximum(rel, 0.0)[None] * log_g[:, None, None]), 0.0)
    per_head = lambda t: np.repeat(t, RET_DK, axis=1)
    rkte = per_head(np.exp((c - 1 - pos)[:, None] * log_g[None, :]))
    rqfs = per_head(np.exp((pos + 1.0)[:, None] * log_g[None, :]))
    rcd = per_head(np.exp(c * log_g)[None, :])
    return tuple(jnp.asarray(t, F32) for t in (rdec, rqfs, rkte, rcd))


def _mix_ab(x, norm_g, w_in_all, w_out_all, j, conv_w, conv_b, dt_bias, a_log, d_skip, norm_ssm):
    t_len, d = x.shape
    tm = min(MIX_TM, t_len)
    rot_a, rot_b = _rotary_tables(t_len // tm, tm)
    rdec, rqfs, rkte, rcd = _retention_tables()

    pad_row = lambda v: jnp.pad(v.astype(F32), (0, LANES - v.shape[0])).reshape(1, LANES)
    expand = (jnp.arange(LANES)[:, None] == (jnp.arange(MIX_W)[None, :] // SSM_HEAD_DIM)).astype(BF16)
    expand = jnp.concatenate([expand] * 3, axis=0)
    t_idx = jnp.arange(tm)
    tri = (((t_idx[:, None] // SSM_CHUNK) == (t_idx[None, :] // SSM_CHUNK)) &
           (t_idx[:, None] >= t_idx[None, :])).astype(BF16)

    operands = [
        (x, pl.BlockSpec((2 * tm, d), lambda i: (i, 0))),
        (x, _next_tile_spec(tm, d, t_len // tm)),
        (rot_a, pl.BlockSpec((2, SUBLANES, RET_DK), lambda i: (i, 0, 0))),
        (rot_b, None),
        (norm_g.reshape(1, d), None),
        _layer_operand(jnp.swapaxes(w_in_all, 1, 2), j),
        (rdec, None), (rqfs, None), (rkte, None), (rcd, None),
        (conv_w, None), (conv_b.reshape(1, SSM_XBC), None),
        (pad_row(dt_bias), None), (pad_row(a_log), None),
        (jnp.repeat(d_skip.astype(F32), SSM_HEAD_DIM).reshape(1, MIX_W), None),
        (norm_ssm.reshape(1, MIX_W), None),
        (expand, None), (tri, None),
        _layer_operand(w_out_all, j),
    ]
    args = [a for a, _ in operands]
    specs = [s if s is not None else _resident_spec(a.shape) for a, s in operands]
    return pl.pallas_call(
        functools.partial(_ab_body, tm=tm),
        grid=(t_len // (2 * tm),),
        in_specs=specs,
        out_specs=pl.BlockSpec((2 * tm, d), lambda i: (i, 0)),
        out_shape=jax.ShapeDtypeStruct((t_len, d), F32),
        scratch_shapes=[
            pltpu.VMEM((RET_HEADS, RET_DK, RET_DK), F32),
            pltpu.VMEM((LANES, MIX_W), F32),
            pltpu.VMEM((SUBLANES, SSM_XBC), F32),
            pltpu.VMEM((AB_DT, d), BF16),
            pltpu.VMEM((LANES, d), BF16),
            pltpu.VMEM((2 * MIX_W, d), BF16),
            pltpu.VMEM((tm, AB_DT + LANES), F32),
            pltpu.VMEM((tm, AB_DT + LANES), F32),
        ],
        compiler_params=pltpu.CompilerParams(
            dimension_semantics=("arbitrary",), vmem_limit_bytes=VMEM_LIMIT),
        name="mix_ret_ssd",
    )(*args)


def _head_sum(x, ones_blk):
    tiles = [_sel_dot_lhs(x[:, j * LANES:(j + 1) * LANES], ones_blk, 2) for j in range(x.shape[1] // LANES)]
    return jnp.concatenate(tiles, axis=1)


def _cd_mix(p_ref, x, lcw_ref, lcb_ref, wa_ref, ba_ref, wx_ref, bx_ref, lam_ref,
            mu_ref, w0_ref, w2_ref, a0_ref, a2_ref, g2_ref, kk_ref, ka_ref, rk_ref, lng_ref, lnb_ref,
            ones_ref, sel_ref, ext_sc, hcar_sc, pcar_sc, s_sc, woutb_sc, tm, side):
    def proj(lo, width):
        return p_ref[:, lo:lo + width]

    rows = lax.broadcasted_iota(jnp.int32, (tm, 1), 0)

    xc = _causal_conv(ext_sc, proj(CD_XB, MIX_W), lcw_ref, lcb_ref, tm, LRU_CONV)
    _run_next(side)
    r_parts, i_parts = [], []
    for n in range(LRU_BLOCKS):
        ls = slice(n * LRU_BLOCK, (n + 1) * LRU_BLOCK)
        r_parts.append(jax.nn.sigmoid(_bdot(xc[:, ls], wa_ref[n]) + ba_ref[:, ls]))
        i_parts.append(jax.nn.sigmoid(_bdot(xc[:, ls], wx_ref[n]) + bx_ref[:, ls]))
    r_gate = jnp.concatenate(r_parts, axis=1)
    i_gate = jnp.concatenate(i_parts, axis=1)
    log_a = -LRU_C * r_gate * _softplus(-lam_ref[...])
    a_cum = jnp.exp(log_a)
    hs = _sqrt_nonneg(-jnp.tanh(log_a) * (1.0 + a_cum * a_cum)) * (i_gate * xc)
    _run_next(side)
    in_group = rows % SUBLANES
    shift = 1
    while shift < SUBLANES:
        valid = in_group >= shift
        a_prev = jnp.where(valid, pltpu.roll(a_cum, shift, axis=0), 1.0)
        h_prev = jnp.where(valid, pltpu.roll(hs, shift, axis=0), 0.0)
        hs = hs + a_cum * h_prev
        a_cum = a_cum * a_prev
        shift *= 2
    carry = hcar_sc[...]
    groups = []
    for gi in range(tm // SUBLANES):
        grp = slice(gi * SUBLANES, (gi + 1) * SUBLANES)
        hg = hs[grp] + a_cum[grp] * carry
        groups.append(hg)
        carry = hg[SUBLANES - 1:SUBLANES, :]
    hs = jnp.concatenate(groups, axis=0)
    hcar_sc[...] = carry
    _run_next(side)
    y_lru = hs * _gelu_tanh(proj(CD_GB, MIX_W))

    pr = proj(CD_PR, CD_PR_COLS)
    prev = pltpu.roll(pr, 1, axis=0)
    row8 = lax.broadcasted_iota(jnp.int32, (SUBLANES, 1), 0)
    prev = jnp.concatenate([jnp.where(row8 == 0, pcar_sc[...], prev[0:SUBLANES]), prev[SUBLANES:]], axis=0)
    pcar_sc[...] = pr[tm - 1:tm, :]
    ps = pr + (prev - pr) * mu_ref[...]
    _run_next(side)
    r = ps[:, 0:MIX_W]
    k = ps[:, MIX_W:2 * MIX_W]
    v = ps[:, 2 * MIX_W:3 * MIX_W]
    wl = ps[:, 3 * MIX_W:3 * MIX_W + LANES]
    al = ps[:, 3 * MIX_W + LANES:3 * MIX_W + 2 * LANES]
    gl = ps[:, 3 * MIX_W + 2 * LANES:3 * MIX_W + 4 * LANES]
    log_w = -math.exp(-0.5) * jax.nn.sigmoid(w0_ref[...] + _bdot(jnp.tanh(wl), w2_ref[...]))
    a = jax.nn.sigmoid(a0_ref[...] + _bdot(al, a2_ref[...]))
    g = _bdot(jax.nn.sigmoid(gl), g2_ref[...])
    _run_next(side)
    ones_blk = ones_ref[...]
    kk = k * kk_ref[...]
    kk = kk * lax.rsqrt(jnp.maximum(_head_sum(kk * kk, ones_blk), 1e-24))
    k = k * (1.0 + (a - 1.0) * ka_ref[...])
    c_len = RWKV_CHUNK
    pair = 2 * c_len
    n_chunks = tm // c_len
    n_pairs = MIX_W // LANES

    cs = _sel_dot_rhs(sel_ref[...], log_w, 3)
    last = [cs[(c + 1) * c_len - 1:(c + 1) * c_len, :] for c in range(n_chunks)]
    tot = jnp.concatenate([jnp.broadcast_to(t, (c_len, MIX_W)) for t in last], axis=0)
    e_neg = jnp.exp(-cs)
    e_end = jnp.exp(tot - cs)
    nb = -(kk * a)
    bt = kk * jnp.exp(cs - log_w)
    rt = r * jnp.exp(cs)
    ab = nb * e_neg
    kb = k * e_neg
    ae = nb * e_end
    ke = k * e_end

    lane = lax.broadcasted_iota(jnp.int32, (1, LANES), 1)
    m0 = lane < RWKV_HEAD_DIM
    trow = lax.broadcasted_iota(jnp.int32, (c_len, LANES), 0)
    scol = lax.broadcasted_iota(jnp.int32, (c_len, LANES), 1) % c_len
    strict = trow > scol
    incl = trow >= scol
    eye = (trow == scol).astype(F32)

    def merge_mask(s):
        return ((trow // s) == (scol // s) + 1) & ((trow // s) % 2 == 1)

    def blockdiag(t):
        return jnp.concatenate([jnp.where(m0, t, 0.0), jnp.where(m0, 0.0, t)], axis=0)

    def tile(t, c, j):
        return t[c * c_len:(c + 1) * c_len, j * LANES:(j + 1) * LANES]

    units = [(c, j) for c in range(n_chunks) for j in range(n_pairs)]
    vbd = {u: blockdiag(tile(v, *u)).astype(BF16) for u in units}
    n_mat, lhs_x, lhs_y, lhs_z, w_col = {}, {}, {}, {}, {}
    for u in units:
        bt_u = tile(bt, *u).astype(BF16)
        rt_u = tile(rt, *u).astype(BF16)
        gram = _bdot_nt(jnp.concatenate([bt_u, rt_u], axis=0),
                        jnp.concatenate([blockdiag(tile(ab, *u)), blockdiag(tile(kb, *u))], axis=0))
        n_mat[u] = jnp.where(strict, gram[0:c_len, 0:LANES], 0.0)
        lhs_x[u] = jnp.concatenate(
            [bt_u, jnp.where(strict, gram[0:c_len, LANES:2 * LANES], 0.0).astype(BF16)], axis=1)
        lhs_y[u] = jnp.concatenate(
            [rt_u, jnp.where(incl, gram[c_len:pair, 0:LANES], 0.0).astype(BF16),
             jnp.where(incl, gram[c_len:pair, LANES:2 * LANES], 0.0).astype(BF16)], axis=1)
        lhs_z[u] = jnp.concatenate(
            [blockdiag(tile(ae, *u)).T.astype(BF16), blockdiag(tile(ke, *u)).T.astype(BF16)], axis=1)
        c, j = u
        w_row = jnp.exp(last[c][:, j * LANES:(j + 1) * LANES])
        w_col[u] = jnp.broadcast_to(w_row, (pair, LANES)).T

    n_bd = {u: blockdiag(n_mat[u]).astype(BF16) for u in units}
    t_inv = {u: eye + jnp.where(merge_mask(1), n_mat[u], 0.0) for u in units}
    s = 2
    while s < c_len:
        mask = merge_mask(s)
        half = {u: jnp.dot(t_inv[u].astype(BF16), n_bd[u], preferred_element_type=F32) for u in units}
        t_inv = {u: t_inv[u] + jnp.where(mask, _bdot(half[u], blockdiag(t_inv[u])), 0.0) for u in units}
        s *= 2
    t_inv = {u: t_inv[u].astype(BF16) for u in units}

    y_rows = []
    for c in range(n_chunks):
        z_prev = [s_sc[j] for j in range(n_pairs)]
        zb = [z.astype(BF16) for z in z_prev]
        xs = [jnp.dot(lhs_x[(c, j)], jnp.concatenate([zb[j], vbd[(c, j)]], axis=0),
                      preferred_element_type=F32) for j in range(n_pairs)]
        _run_next(side)
        us = [jnp.dot(t_inv[(c, j)], blockdiag(xs[j]).astype(BF16), preferred_element_type=F32)
              for j in range(n_pairs)]
        _run_next(side)
        us = [blockdiag(u).astype(BF16) for u in us]
        tiles = []
        for j in range(n_pairs):
            tiles.append(jnp.dot(lhs_y[(c, j)], jnp.concatenate([zb[j], us[j], vbd[(c, j)]], axis=0),
                                 preferred_element_type=F32))
            s_sc[j] = z_prev[j] * w_col[(c, j)] + jnp.dot(
                lhs_z[(c, j)], jnp.concatenate([us[j], vbd[(c, j)]], axis=0), preferred_element_type=F32)
        y_rows.append(jnp.concatenate(tiles, axis=1))
    y = jnp.concatenate(y_rows, axis=0)
    inv_n = 1.0 / RWKV_HEAD_DIM
    mean = _head_sum(y, ones_blk) * inv_n
    yc = y - mean
    var = _head_sum(yc * yc, ones_blk) * inv_n
    y = yc * lax.rsqrt(var + RWKV_LN_EPS) * lng_ref[...] + lnb_ref[...]
    y = y + _head_sum(r * k * rk_ref[...], ones_blk) * v
    y_rwkv = y * g

    _run_all(side)
    return x + _bdot(jnp.concatenate([y_lru, y_rwkv], axis=1), woutb_sc[...])


def _cd_body(x_ref, xn_ref, g_ref, win_ref, lcw_ref, lcb_ref, wa_ref, ba_ref, wx_ref, bx_ref, lam_ref,
             mu_ref, w0_ref, w2_ref, a0_ref, a2_ref, g2_ref, kk_ref, ka_ref, rk_ref, lng_ref, lnb_ref,
             ones_ref, sel_ref, wout_ref, o_ref,
             ext_sc, hcar_sc, pcar_sc, s_sc, winb_sc, wlora_sc, woutb_sc, pa_sc, pb_sc, *, tm):
    steps = functools.partial(_projection_steps, g_ref=g_ref, wt_sc=winb_sc, tail_sc=wlora_sc)

    @pl.when(pl.program_id(0) == 0)
    def _():
        ext_sc[...] = jnp.zeros_like(ext_sc)
        hcar_sc[...] = jnp.zeros_like(hcar_sc)
        pcar_sc[...] = jnp.zeros_like(pcar_sc)
        s_sc[...] = jnp.zeros_like(s_sc)
        _cast_weight_rows(winb_sc, win_ref)
        _cast_weight(woutb_sc, wout_ref)
        wlora_sc[...] = jnp.zeros_like(wlora_sc)
        src = CD_LORA
        for dst, width in ((0, DECAY_LORA), (LANES, ICL_LORA), (2 * LANES, GATE_LORA)):
            wlora_sc[dst:dst + width, :] = win_ref[src:src + width, :].astype(BF16)
            src += width
        _run_all(steps(x_ref[0:tm, :], pa_sc))

    mix = functools.partial(
        _cd_mix, lcw_ref=lcw_ref, lcb_ref=lcb_ref, wa_ref=wa_ref, ba_ref=ba_ref, wx_ref=wx_ref, bx_ref=bx_ref,
        lam_ref=lam_ref, mu_ref=mu_ref, w0_ref=w0_ref, w2_ref=w2_ref, a0_ref=a0_ref, a2_ref=a2_ref, g2_ref=g2_ref,
        kk_ref=kk_ref, ka_ref=ka_ref, rk_ref=rk_ref, lng_ref=lng_ref, lnb_ref=lnb_ref, ones_ref=ones_ref,
        sel_ref=sel_ref, ext_sc=ext_sc, hcar_sc=hcar_sc, pcar_sc=pcar_sc, s_sc=s_sc, woutb_sc=woutb_sc, tm=tm)
    o_ref[0:tm, :] = mix(pa_sc, x_ref[0:tm, :], side=steps(x_ref[tm:2 * tm, :], pb_sc))
    o_ref[tm:2 * tm, :] = mix(pb_sc, x_ref[tm:2 * tm, :], side=steps(xn_ref[...], pa_sc))


def _mix_cd(x, norm_g, w_in_all, w_out_all, j, lru_conv_w, lru_conv_b, lru_wa, lru_ba, lru_wx, lru_bx,
            lru_lambda, mu, w0, w2, a0, a2, g2, k_k, k_a, r_k, ln_g, ln_b):
    t_len, d = x.shape
    tm = min(MIX_TM, t_len)

    def pad_lora(t, lo, width, padded):
        return jnp.pad(t[..., lo:lo + width], [(0, 0)] * (t.ndim - 1) + [(0, padded - width)])

    def lora_cols(t, base):
        return jnp.concatenate([
            pad_lora(t, base, DECAY_LORA, LANES),
            pad_lora(t, base + DECAY_LORA, ICL_LORA, LANES),
            pad_lora(t, base + DECAY_LORA + ICL_LORA, GATE_LORA, 2 * LANES)], axis=-1)

    mu_row = mu.reshape(1, -1)
    mu_pad = jnp.concatenate([mu_row[:, :3 * MIX_W], lora_cols(mu_row, 3 * MIX_W)], axis=-1)
    pad_rows = lambda t, n: jnp.pad(t, ((0, n - t.shape[0]), (0, 0))).astype(BF16)
    row = lambda t: t.reshape(1, -1).astype(F32)
    ones_blk = ((jnp.arange(LANES)[:, None] // RWKV_HEAD_DIM) ==
                (jnp.arange(LANES)[None, :] // RWKV_HEAD_DIM)).astype(BF16)
    ones_blk = jnp.concatenate([ones_blk] * 2, axis=0)
    t_idx = jnp.arange(tm)
    same_chunk = (t_idx[:, None] // RWKV_CHUNK) == (t_idx[None, :] // RWKV_CHUNK)
    sel = (same_chunk & (t_idx[:, None] >= t_idx[None, :])).astype(BF16)

    operands = [
        (x, pl.BlockSpec((2 * tm, d), lambda i: (i, 0))),
        (x, _next_tile_spec(tm, d, t_len // tm)),
        (row(norm_g), None),
        _layer_operand(jnp.swapaxes(w_in_all, 1, 2), j),
        (lru_conv_w, None), (row(lru_conv_b), None),
        (lru_wa.astype(BF16), None), (row(lru_ba), None),
        (lru_wx.astype(BF16), None), (row(lru_bx), None),
        (row(lru_lambda), None),
        (mu_pad, None),
        (row(w0), None), (pad_rows(w2, LANES), None),
        (row(a0), None), (pad_rows(a2, LANES), None),
        (pad_rows(g2, 2 * LANES), None),
        (row(k_k), None), (row(k_a), None), (row(r_k), None), (row(ln_g), None), (row(ln_b), None),
        (ones_blk, None), (sel, None),
        _layer_operand(w_out_all, j),
    ]
    args = [a for a, _ in operands]
    specs = [s if s is not None else _resident_spec(a.shape) for a, s in operands]
    return pl.pallas_call(
        functools.partial(_cd_body, tm=tm),
        grid=(t_len // (2 * tm),),
        in_specs=specs,
        out_specs=pl.BlockSpec((2 * tm, d), lambda i: (i, 0)),
        out_shape=jax.ShapeDtypeStruct((t_len, d), F32),
        scratch_shapes=[
            pltpu.VMEM((SUBLANES, MIX_W), F32),
            pltpu.VMEM((1, MIX_W), F32),
            pltpu.VMEM((1, CD_PR_COLS), F32),
            pltpu.VMEM((MIX_W // LANES, LANES, LANES), F32),
            pltpu.VMEM((CD_LORA, d), BF16),
            pltpu.VMEM((CD_PR + CD_PR_COLS - CD_LORA, d), BF16),
            pltpu.VMEM((2 * MIX_W, d), BF16),
            pltpu.VMEM((tm, CD_PR + CD_PR_COLS), F32),
            pltpu.VMEM((tm, CD_PR + CD_PR_COLS), F32),
        ],
        compiler_params=pltpu.CompilerParams(
            dimension_semantics=("arbitrary",), vmem_limit_bytes=VMEM_LIMIT),
        name="mix_lru_rwkv",
    )(*args)


def kernel(x, ffn1_norm, ffn1_wg, ffn1_wu, ffn1_wd, mix_norm, ffn2_norm, ffn2_wg, ffn2_wu, ffn2_wd,
           ab_w_in, ab_w_out, ssm_conv_w, ssm_conv_b, ssm_dt_bias, ssm_a_log, ssm_d, ssm_norm,
           cd_w_in, cd_w_out, lru_conv_w, lru_conv_b, lru_wa, lru_ba, lru_wx, lru_bx, lru_lambda,
           rwkv_mu, rwkv_w0, rwkv_w2, rwkv_a0, rwkv_a2, rwkv_g2, rwkv_kk, rwkv_ka, rwkv_rk,
           rwkv_ln_g, rwkv_ln_b, final_norm):
    bsz, t_len, d = x.shape
    depth = ffn1_norm.shape[0]
    outs = []
    for b in range(bsz):
        xb = x.reshape(t_len, d) if bsz == 1 else x[b]
        for layer in range(depth):
            j = layer // 2
            xb = _ffn(xb, ffn1_norm[layer], ffn1_wg, ffn1_wu, ffn1_wd, layer, final_norm, final_norm=False)
            if layer % 2 == 0:
                xb = _mix_ab(xb, mix_norm[layer], ab_w_in, ab_w_out, j, ssm_conv_w[j], ssm_conv_b[j],
                             ssm_dt_bias[j], ssm_a_log[j], ssm_d[j], ssm_norm[j])
            else:
                xb = _mix_cd(xb, mix_norm[layer], cd_w_in, cd_w_out, j, lru_conv_w[j], lru_conv_b[j], lru_wa[j],
                             lru_ba[j], lru_wx[j], lru_bx[j], lru_lambda[j], rwkv_mu[j], rwkv_w0[j],
                             rwkv_w2[j], rwkv_a0[j], rwkv_a2[j], rwkv_g2[j], rwkv_kk[j], rwkv_ka[j],
                             rwkv_rk[j].reshape(-1), rwkv_ln_g[j], rwkv_ln_b[j])
            xb = _ffn(xb, ffn2_norm[layer], ffn2_wg, ffn2_wu, ffn2_wd, layer, final_norm,
                      final_norm=(layer == depth - 1))
        outs.append(xb)
    return outs[0].reshape(1, t_len, d) if bsz == 1 else jnp.stack(outs, axis=0)
```

```python
import functools
import math

import jax
import jax.numpy as jnp
import numpy as np
from jax import lax
from jax.experimental import pallas as pl
from jax.experimental.pallas import tpu as pltpu

F32 = jnp.float32
BF16 = jnp.bfloat16

D_FF = 2816
NORM_EPS = 1e-6
MIX_W = 512

RET_HEADS = 4
RET_DK = 128
RET_CHUNK = 128
ROPE_BASE = 10000.0
HEAD_NORM_EPS = 1e-6

SSM_HEADS = 8
SSM_HEAD_DIM = 64
SSM_GROUPS = 2
SSM_STATE = 64
SSM_CONV = 4
SSM_CHUNK = 128
SSM_XBC = 768
SSM_NORM_EPS = 1e-5

LRU_BLOCKS = 4
LRU_BLOCK = 128
LRU_CONV = 4
LRU_C = 8.0

RWKV_HEAD_DIM = 64
DECAY_LORA = 64
ICL_LORA = 64
GATE_LORA = 160
RWKV_LN_EPS = 64e-5
RWKV_CHUNK = 64

LANES = 128
SUBLANES = 8
VMEM_LIMIT = 56 * 1024 * 1024

FFN_TM = 512
FFN_TF = 256
MXU_DIM = 256
FFN_TK = (0, 6 * MXU_DIM, D_FF)
MIX_TM = 256
PROJ_TN = 256

AB_Q, AB_K, AB_V, AB_G, AB_Z, AB_XBC, AB_DT = 0, 512, 1024, 1536, 2048, 2560, 3328
CD_XB, CD_GB, CD_PR, CD_LORA = 0, 512, 1024, 2560
CD_PR_COLS = 2048


def _bdot(a, b):
    return jnp.dot(a.astype(BF16), b.astype(BF16), preferred_element_type=F32)


def _bdot_nt(a, b):
    return lax.dot_general(a.astype(BF16), b.astype(BF16), (((1,), (1,)), ((), ())),
                           preferred_element_type=F32)


def _bdot_tn(a, b):
    return lax.dot_general(a.astype(BF16), b.astype(BF16), (((0,), (0,)), ((), ())),
                           preferred_element_type=F32)


def _split(x, pieces):
    out = []
    for _ in range(pieces - 1):
        p = x.astype(BF16)
        out.append(p)
        x = x - p.astype(F32)
    out.append(x.astype(BF16))
    return out


def _sel_dot_lhs(x, sel_stack, pieces):
    return jnp.dot(jnp.concatenate(_split(x, pieces), axis=1), sel_stack, preferred_element_type=F32)


def _sel_dot_rhs(sel, x, pieces):
    n = x.shape[1]
    res = jnp.dot(sel, jnp.concatenate(_split(x, pieces), axis=1), preferred_element_type=F32)
    out = res[:, 0:n]
    for p in range(1, pieces):
        out = out + res[:, p * n:(p + 1) * n]
    return out


def _rms_norm(x, g, eps):
    return x * lax.rsqrt(jnp.mean(x * x, axis=-1, keepdims=True) + eps) * g


def _silu(x):
    return x * jax.nn.sigmoid(x)


def _softplus(x):
    return jnp.maximum(x, 0.0) + jnp.log1p(jnp.exp(-jnp.abs(x)))


def _sqrt_nonneg(y):
    return y * lax.rsqrt(jnp.maximum(y, jnp.finfo(jnp.float32).tiny))


def _gelu_tanh(x):
    c = math.sqrt(2.0 / math.pi)
    return 0.5 * x * (1.0 + jnp.tanh(c * (x + 0.044715 * (x * x * x))))


def _causal_conv(tail_ref, x, w_ref, b_ref, tm, width):
    tail = tail_ref[...]
    row = lax.broadcasted_iota(jnp.int32, (SUBLANES, 1), 0)
    acc = b_ref[...] + w_ref[width - 1:width, :] * x
    for s in range(1, width):
        delayed = pltpu.roll(x, s, axis=0)
        head = jnp.where(row < s, pltpu.roll(tail, s, axis=0), delayed[0:SUBLANES])
        delayed = jnp.concatenate([head, delayed[SUBLANES:]], axis=0)
        acc = acc + w_ref[width - 1 - s:width - s, :] * delayed
    tail_ref[...] = x[tm - SUBLANES:tm]
    return acc


def _ffn_body(x_ref, g_ref, wg_hbm, wu_hbm, wd_hbm, fin_ref, o_ref, act_sc, wg_sc, wu_sc, wd_sc, sem,
              *, final_norm, layer):
    n_sec = D_FF // FFN_TF

    def column_copy(k, hbm, sc, j):
        cols = slice(j * FFN_TF, (j + 1) * FFN_TF)
        return pltpu.make_async_copy(hbm.at[layer, :, cols], sc.at[:, cols], sem.at[k, j])

    def row_copy(j):
        rows = slice(j * FFN_TF, (j + 1) * FFN_TF)
        return pltpu.make_async_copy(wd_hbm.at[layer, rows, :], wd_sc.at[rows, :], sem.at[2, j])

    def step(first):
        if first:
            for j in range(n_sec):
                column_copy(0, wg_hbm, wg_sc, j).start()
                column_copy(1, wu_hbm, wu_sc, j).start()
            for j in range(n_sec):
                row_copy(j).start()
        x = x_ref[...]
        h = _rms_norm(x, g_ref[...], NORM_EPS).astype(BF16)
        for j in range(n_sec):
            cols = slice(j * FFN_TF, (j + 1) * FFN_TF)
            if first:
                column_copy(0, wg_hbm, wg_sc, j).wait()
                column_copy(1, wu_hbm, wu_sc, j).wait()
            gate = jnp.dot(h, wg_sc[:, cols].astype(BF16), preferred_element_type=F32)
            up = jnp.dot(h, wu_sc[:, cols].astype(BF16), preferred_element_type=F32)
            act_sc[:, cols] = (_silu(gate) * up).astype(BF16)
        down = None
        for lo, hi in zip(FFN_TK[:-1], FFN_TK[1:]):
            if first:
                for j in range(lo // FFN_TF, hi // FFN_TF):
                    row_copy(j).wait()
            part = jnp.dot(act_sc[:, lo:hi], wd_sc[lo:hi, :].astype(BF16), preferred_element_type=F32)
            down = part if down is None else down + part
        y = x + 0.5 * down
        if final_norm:
            y = _rms_norm(y, fin_ref[...], NORM_EPS)
        o_ref[...] = y

    pl.when(pl.program_id(0) == 0)(functools.partial(step, True))
    pl.when(pl.program_id(0) > 0)(functools.partial(step, False))


def _resident_spec(shape):
    nd = len(shape)
    return pl.BlockSpec(shape, lambda i, _nd=nd: (0,) * _nd, pipeline_mode=pl.Buffered(1))


def _next_tile_spec(tm, d, n_tiles):
    return pl.BlockSpec((tm, d), lambda i: (jnp.minimum(2 * i + 2, n_tiles - 1), 0))


def _layer_spec(shape, layer):
    nd = len(shape)
    return pl.BlockSpec((None,) + tuple(shape), lambda i, _nd=nd: (layer,) + (0,) * _nd,
                        pipeline_mode=pl.Buffered(1))


def _layer_operand(stack, layer):
    if stack.shape[0] == 1:
        return stack.reshape(stack.shape[1:]), _resident_spec(stack.shape[1:])
    return stack, _layer_spec(stack.shape[1:], layer)


def _ffn(x, norm_g, wg, wu, wd, layer, fin_g, *, final_norm):
    t_len, d = x.shape
    tm = min(FFN_TM, t_len)
    hbm = pl.BlockSpec(memory_space=pl.ANY)
    return pl.pallas_call(
        functools.partial(_ffn_body, final_norm=final_norm, layer=layer),
        grid=(t_len // tm,),
        in_specs=[
            pl.BlockSpec((tm, d), lambda i: (i, 0)),
            _resident_spec((1, d)),
            hbm, hbm, hbm,
            _resident_spec((1, d)),
        ],
        out_specs=pl.BlockSpec((tm, d), lambda i: (i, 0)),
        out_shape=jax.ShapeDtypeStruct((t_len, d), F32),
        scratch_shapes=[
            pltpu.VMEM((tm, D_FF), BF16),
            pltpu.VMEM((d, D_FF), F32), pltpu.VMEM((d, D_FF), F32), pltpu.VMEM((D_FF, d), F32),
            pltpu.SemaphoreType.DMA((3, D_FF // FFN_TF)),
        ],
        compiler_params=pltpu.CompilerParams(
            dimension_semantics=("arbitrary",), vmem_limit_bytes=VMEM_LIMIT),
        name="ffn_final" if final_norm else "ffn",
    )(x, norm_g.reshape(1, d), wg, wu, wd, fin_g.reshape(1, d))


def _cast_weight(dst_sc, src_ref):
    for lo in range(0, dst_sc.shape[1], MXU_DIM):
        hi = min(lo + MXU_DIM, dst_sc.shape[1])
        dst_sc[:, lo:hi] = src_ref[:, lo:hi].astype(BF16)


def _cast_weight_rows(dst_sc, src_ref):
    for lo in range(0, dst_sc.shape[0], MXU_DIM):
        hi = min(lo + MXU_DIM, dst_sc.shape[0])
        dst_sc[lo:hi, :] = src_ref[lo:hi, :].astype(BF16)


def _projection_steps(x, p_ref, g_ref, wt_sc, tail_sc):
    h = _rms_norm(x, g_ref[...], NORM_EPS).astype(BF16)
    main = wt_sc.shape[0]
    nt_dot = lambda w: lax.dot_general(h, w, (((1,), (1,)), ((), ())), preferred_element_type=F32)

    def section(lo, hi):
        p_ref[:, lo:hi] = nt_dot(wt_sc[lo:hi, :])

    def tail():
        p_ref[:, main:main + tail_sc.shape[0]] = nt_dot(tail_sc[...])

    steps = [functools.partial(section, lo, min(lo + PROJ_TN, main)) for lo in range(0, main, PROJ_TN)]
    return steps + [tail]


def _run_next(steps):
    if steps:
        steps.pop(0)()


def _run_all(steps):
    while steps:
        _run_next(steps)


def _ab_mix(p_ref, x, rota, rotb_ref, rdec_ref, rqfs_ref, rkte_ref, rcd_ref,
            cw_ref, cb_ref, dtb_ref, alog_ref, dexp_ref, sng_ref, exp_ref, tri_ref,
            sret_sc, sssm_sc, ext_sc, woutb_sc, tm, side):
    def proj(lo, width):
        return p_ref[:, lo:lo + width]

    cos_a, sin_a = rota[0:1, :], rota[1:2, :]
    sin_a_sgn, cos_a_sgn = rota[2:3, :], rota[3:4, :]
    cos_b, sin_b = rotb_ref[:, 0:RET_DK], rotb_ref[:, RET_DK:2 * RET_DK]
    cos2 = cos_a * cos_b - sin_a * sin_b
    sin2 = sin_a_sgn * cos_b + cos_a_sgn * sin_b

    q_all = proj(AB_Q, MIX_W)
    k_all = proj(AB_K, MIX_W)
    v_all = proj(AB_V, MIX_W)
    gate_ret = _silu(proj(AB_G, MIX_W))
    scale = RET_DK ** -0.5
    c_len = RET_CHUNK
    n_chunks = tm // c_len
    cells = [(c, hh) for c in range(n_chunks) for hh in range(RET_HEADS)]
    rows_of = lambda c: slice(c * c_len, (c + 1) * c_len)
    lanes_of = lambda hh: slice(hh * RET_DK, (hh + 1) * RET_DK)

    def rotate(t, c, hh):
        t = t[rows_of(c), lanes_of(hh)]
        return t * cos2[rows_of(c)] + pltpu.roll(t, RET_DK // 2, axis=1) * sin2[rows_of(c)]

    qh = {u: rotate(q_all, *u) for u in cells}
    kh = {u: rotate(k_all, *u) * scale for u in cells}
    vh = {(c, hh): v_all[rows_of(c), lanes_of(hh)].astype(BF16) for c, hh in cells}
    scores = {(c, hh): _bdot_nt(qh[(c, hh)], kh[(c, hh)]) * rdec_ref[hh] for c, hh in cells}
    kv = {(c, hh): _bdot_tn(kh[(c, hh)] * rkte_ref[:, lanes_of(hh)], vh[(c, hh)]) for c, hh in cells}
    state = {}
    for hh in range(RET_HEADS):
        s_cur = sret_sc[hh]
        for c in range(n_chunks):
            state[(c, hh)] = s_cur
            s_cur = s_cur * rcd_ref[:, lanes_of(hh)] + kv[(c, hh)]
        sret_sc[hh] = s_cur
    outs = {(c, hh): _bdot(jnp.concatenate([scores[(c, hh)], qh[(c, hh)] * rqfs_ref[:, lanes_of(hh)]], axis=1),
                           jnp.concatenate([vh[(c, hh)], state[(c, hh)].astype(BF16)], axis=0))
            for c, hh in cells}
    outs = {u: o * lax.rsqrt(jnp.mean(o * o, axis=-1, keepdims=True) + HEAD_NORM_EPS) for u, o in outs.items()}
    y_ret = gate_ret * jnp.concatenate(
        [jnp.concatenate([outs[(c, hh)] for hh in range(RET_HEADS)], axis=1) for c in range(n_chunks)], axis=0)

    z = proj(AB_Z, MIX_W)
    xbc = _silu(_causal_conv(ext_sc, proj(AB_XBC, SSM_XBC), cw_ref, cb_ref, tm, SSM_CONV))
    xs = xbc[:, :MIX_W]
    bm = xbc[:, MIX_W:MIX_W + LANES]
    cm = xbc[:, MIX_W + LANES:MIX_W + 2 * LANES]
    lane = lax.broadcasted_iota(jnp.int32, (1, LANES), 1)
    dt = _softplus(proj(AB_DT, LANES) + dtb_ref[...])
    a_neg = jnp.where(lane < SSM_HEADS, -jnp.exp(alog_ref[...]), 0.0)
    adt = dt * a_neg
    xdt = xs * _sel_dot_lhs(dt, exp_ref[0:2 * LANES, :], 2)
    acs_all = _sel_dot_rhs(tri_ref[...], adt, 3)
    acs_e_all = _sel_dot_lhs(acs_all, exp_ref[...], 3)
    l_len = SSM_CHUNK
    row = lax.broadcasted_iota(jnp.int32, (l_len, l_len), 0)
    col = lax.broadcasted_iota(jnp.int32, (l_len, l_len), 1)
    causal = row >= col
    first_half = lax.broadcasted_iota(jnp.int32, (1, LANES), 1) < SSM_HEAD_DIM
    grp_mask = [first_half, jnp.logical_not(first_half)]
    srow = lax.broadcasted_iota(jnp.int32, (LANES, MIX_W), 0)
    scol = lax.broadcasted_iota(jnp.int32, (LANES, MIX_W), 1)
    state_mask = (srow // SSM_STATE) == (scol // (MIX_W // SSM_GROUPS))
    s_cur = sssm_sc[...]
    s_enter = []
    for c in range(tm // l_len):
        rs = slice(c * l_len, (c + 1) * l_len)
        last_e = acs_e_all[(c + 1) * l_len - 1:(c + 1) * l_len, :]
        upd = _bdot_tn(bm[rs], xdt[rs] * jnp.exp(last_e - acs_e_all[rs]))
        s_enter.append(s_cur)
        s_cur = s_cur * jnp.exp(last_e) + jnp.where(state_mask, upd, 0.0)
    sssm_sc[...] = s_cur

    ssm_rows = []
    for c in range(tm // l_len):
        rs = slice(c * l_len, (c + 1) * l_len)
        acs = acs_all[rs]
        acs_e = acs_e_all[rs]
        acs_t = acs.T
        bm_c = bm[rs]
        cm_c = cm[rs]
        y_off = _bdot(cm_c, s_enter[c]) * jnp.exp(acs_e)
        cb = [jnp.where(causal, _bdot_nt(jnp.where(grp_mask[g], cm_c, 0.0), bm_c), 0.0)
              for g in range(SSM_GROUPS)]
        tiles = []
        for j in range(MIX_W // LANES):
            xt = xdt[rs, j * LANES:(j + 1) * LANES]
            weights = []
            for hh in (2 * j, 2 * j + 1):
                seg = acs[:, hh:hh + 1] - acs_t[hh:hh + 1, :]
                weights.append((cb[hh // (SSM_HEADS // SSM_GROUPS)] * jnp.exp(jnp.minimum(seg, 0.0))).astype(BF16))
                _run_next(side)
            tiles.append(jnp.dot(
                jnp.concatenate(weights, axis=1),
                jnp.concatenate([jnp.where(first_half, xt, 0.0), jnp.where(first_half, 0.0, xt)], axis=0).astype(BF16),
                preferred_element_type=F32))
        y_diag = jnp.concatenate(tiles, axis=1)
        y = y_diag + y_off + dexp_ref[...] * xs[rs]
        y = y * _silu(z[rs])
        halves = []
        for g in range(SSM_GROUPS):
            yg = y[:, g * (MIX_W // SSM_GROUPS):(g + 1) * (MIX_W // SSM_GROUPS)]
            halves.append(yg * lax.rsqrt(jnp.mean(yg * yg, axis=-1, keepdims=True) + SSM_NORM_EPS))
        ssm_rows.append(jnp.concatenate(halves, axis=1) * sng_ref[...])
    y_ssm = jnp.concatenate(ssm_rows, axis=0)

    _run_all(side)
    return x + _bdot(jnp.concatenate([y_ret, y_ssm], axis=1), woutb_sc[...])


def _ab_body(x_ref, xn_ref, rota_ref, rotb_ref, g_ref, win_ref, rdec_ref, rqfs_ref, rkte_ref, rcd_ref,
             cw_ref, cb_ref, dtb_ref, alog_ref, dexp_ref, sng_ref, exp_ref, tri_ref, wout_ref,
             o_ref, sret_sc, sssm_sc, ext_sc, winb_sc, wdt_sc, woutb_sc, pa_sc, pb_sc, *, tm):
    steps = functools.partial(_projection_steps, g_ref=g_ref, wt_sc=winb_sc, tail_sc=wdt_sc)

    @pl.when(pl.program_id(0) == 0)
    def _():
        sret_sc[...] = jnp.zeros_like(sret_sc)
        sssm_sc[...] = jnp.zeros_like(sssm_sc)
        ext_sc[...] = jnp.zeros_like(ext_sc)
        _cast_weight_rows(winb_sc, win_ref)
        _cast_weight(woutb_sc, wout_ref)
        wdt_sc[...] = jnp.zeros_like(wdt_sc)
        wdt_sc[0:SSM_HEADS, :] = win_ref[AB_DT:AB_DT + SSM_HEADS, :].astype(BF16)
        _run_all(steps(x_ref[0:tm, :], pa_sc))

    mix = functools.partial(
        _ab_mix, rotb_ref=rotb_ref, rdec_ref=rdec_ref, rqfs_ref=rqfs_ref, rkte_ref=rkte_ref, rcd_ref=rcd_ref,
        cw_ref=cw_ref, cb_ref=cb_ref, dtb_ref=dtb_ref, alog_ref=alog_ref, dexp_ref=dexp_ref, sng_ref=sng_ref,
        exp_ref=exp_ref, tri_ref=tri_ref, sret_sc=sret_sc, sssm_sc=sssm_sc, ext_sc=ext_sc, woutb_sc=woutb_sc,
        tm=tm)
    o_ref[0:tm, :] = mix(pa_sc, x_ref[0:tm, :], rota_ref[0], side=steps(x_ref[tm:2 * tm, :], pb_sc))
    o_ref[tm:2 * tm, :] = mix(pb_sc, x_ref[tm:2 * tm, :], rota_ref[1], side=steps(xn_ref[...], pa_sc))


def _rotary_tables(n_tiles, tm):
    inv_freq = ROPE_BASE ** (-np.arange(0, RET_DK, 2, dtype=np.float64) / RET_DK)
    two = lambda t: np.concatenate([t, t], axis=-1)
    sign = np.concatenate([-np.ones(RET_DK // 2), np.ones(RET_DK // 2)])
    ang_a = (np.arange(n_tiles, dtype=np.float64) * tm)[:, None] * inv_freq[None, :]
    ang_b = np.arange(tm, dtype=np.float64)[:, None] * inv_freq[None, :]
    cos_a, sin_a = two(np.cos(ang_a)), two(np.sin(ang_a))
    rot_a = np.zeros((n_tiles, SUBLANES, RET_DK))
    rot_a[:, 0], rot_a[:, 1], rot_a[:, 2], rot_a[:, 3] = cos_a, sin_a, sign * sin_a, sign * cos_a
    rot_b = np.concatenate([two(np.cos(ang_b)), two(np.sin(ang_b))], axis=1)
    return jnp.asarray(rot_a, F32), jnp.asarray(rot_b, F32)


def _retention_tables():
    c = RET_CHUNK
    log_g = np.log1p(-(2.0 ** (-5.0 - np.arange(RET_HEADS, dtype=np.float64))))
    pos = np.arange(c, dtype=np.float64)
    rel = pos[:, None] - pos[None, :]
    rdec = np.where(rel >= 0, np.exp(np.maximum(rel, 0.0)[None] * log_g[:, None, None]), 0.0)
    per_head = lambda t: np.repeat(t, RET_DK, axis=1)
    rkte = per_head(np.exp((c - 1 - pos)[:, None] * log_g[None, :]))
    rqfs = per_head(np.exp((pos + 1.0)[:, None] * log_g[None, :]))
    rcd = per_head(np.exp(c * log_g)[None, :])
    return tuple(jnp.asarray(t, F32) for t in (rdec, rqfs, rkte, rcd))


def _mix_ab(x, norm_g, w_in_all, w_out_all, j, conv_w, conv_b, dt_bias, a_log, d_skip, norm_ssm):
    t_len, d = x.shape
    tm = min(MIX_TM, t_len)
    rot_a, rot_b = _rotary_tables(t_len // tm, tm)
    rdec, rqfs, rkte, rcd = _retention_tables()

    pad_row = lambda v: jnp.pad(v.astype(F32), (0, LANES - v.shape[0])).reshape(1, LANES)
    expand = (jnp.arange(LANES)[:, None] == (jnp.arange(MIX_W)[None, :] // SSM_HEAD_DIM)).astype(BF16)
    expand = jnp.concatenate([expand] * 3, axis=0)
    t_idx = jnp.arange(tm)
    tri = (((t_idx[:, None] // SSM_CHUNK) == (t_idx[None, :] // SSM_CHUNK)) &
           (t_idx[:, None] >= t_idx[None, :])).astype(BF16)

    operands = [
        (x, pl.BlockSpec((2 * tm, d), lambda i: (i, 0))),
        (x, _next_tile_spec(tm, d, t_len // tm)),
        (rot_a, pl.BlockSpec((2, SUBLANES, RET_DK), lambda i: (i, 0, 0))),
        (rot_b, None),
        (norm_g.reshape(1, d), None),
        _layer_operand(jnp.swapaxes(w_in_all, 1, 2), j),
        (rdec, None), (rqfs, None), (rkte, None), (rcd, None),
        (conv_w, None), (conv_b.reshape(1, SSM_XBC), None),
        (pad_row(dt_bias), None), (pad_row(a_log), None),
        (jnp.repeat(d_skip.astype(F32), SSM_HEAD_DIM).reshape(1, MIX_W), None),
        (norm_ssm.reshape(1, MIX_W), None),
        (expand, None), (tri, None),
        _layer_operand(w_out_all, j),
    ]
    args = [a for a, _ in operands]
    specs = [s if s is not None else _resident_spec(a.shape) for a, s in operands]
    return pl.pallas_call(
        functools.partial(_ab_body, tm=tm),
        grid=(t_len // (2 * tm),),
        in_specs=specs,
        out_specs=pl.BlockSpec((2 * tm, d), lambda i: (i, 0)),
        out_shape=jax.ShapeDtypeStruct((t_len, d), F32),
        scratch_shapes=[
            pltpu.VMEM((RET_HEADS, RET_DK, RET_DK), F32),
            pltpu.VMEM((LANES, MIX_W), F32),
            pltpu.VMEM((SUBLANES, SSM_XBC), F32),
            pltpu.VMEM((AB_DT, d), BF16),
            pltpu.VMEM((LANES, d), BF16),
            pltpu.VMEM((2 * MIX_W, d), BF16),
            pltpu.VMEM((tm, AB_DT + LANES), F32),
            pltpu.VMEM((tm, AB_DT + LANES), F32),
        ],
        compiler_params=pltpu.CompilerParams(
            dimension_semantics=("arbitrary",), vmem_limit_bytes=VMEM_LIMIT),
        name="mix_ret_ssd",
    )(*args)


def _head_sum(x, ones_blk):
    tiles = [_sel_dot_lhs(x[:, j * LANES:(j + 1) * LANES], ones_blk, 2) for j in range(x.shape[1] // LANES)]
    return jnp.concatenate(tiles, axis=1)


def _cd_mix(p_ref, x, lcw_ref, lcb_ref, wa_ref, ba_ref, wx_ref, bx_ref, lam_ref,
            mu_ref, w0_ref, w2_ref, a0_ref, a2_ref, g2_ref, kk_ref, ka_ref, rk_ref, lng_ref, lnb_ref,
            ones_ref, sel_ref, ext_sc, hcar_sc, pcar_sc, s_sc, woutb_sc, tm, side):
    def proj(lo, width):
        return p_ref[:, lo:lo + width]

    rows = lax.broadcasted_iota(jnp.int32, (tm, 1), 0)

    xc = _causal_conv(ext_sc, proj(CD_XB, MIX_W), lcw_ref, lcb_ref, tm, LRU_CONV)
    _run_next(side)
    r_parts, i_parts = [], []
    for n in range(LRU_BLOCKS):
        ls = slice(n * LRU_BLOCK, (n + 1) * LRU_BLOCK)
        r_parts.append(jax.nn.sigmoid(_bdot(xc[:, ls], wa_ref[n]) + ba_ref[:, ls]))
        i_parts.append(jax.nn.sigmoid(_bdot(xc[:, ls], wx_ref[n]) + bx_ref[:, ls]))
    r_gate = jnp.concatenate(r_parts, axis=1)
    i_gate = jnp.concatenate(i_parts, axis=1)
    log_a = -LRU_C * r_gate * _softplus(-lam_ref[...])
    a_cum = jnp.exp(log_a)
    hs = _sqrt_nonneg(-jnp.tanh(log_a) * (1.0 + a_cum * a_cum)) * (i_gate * xc)
    _run_next(side)
    in_group = rows % SUBLANES
    shift = 1
    while shift < SUBLANES:
        valid = in_group >= shift
        a_prev = jnp.where(valid, pltpu.roll(a_cum, shift, axis=0), 1.0)
        h_prev = jnp.where(valid, pltpu.roll(hs, shift, axis=0), 0.0)
        hs = hs + a_cum * h_prev
        a_cum = a_cum * a_prev
        shift *= 2
    carry = hcar_sc[...]
    groups = []
    for gi in range(tm // SUBLANES):
        grp = slice(gi * SUBLANES, (gi + 1) * SUBLANES)
        hg = hs[grp] + a_cum[grp] * carry
        groups.append(hg)
        carry = hg[SUBLANES - 1:SUBLANES, :]
    hs = jnp.concatenate(groups, axis=0)
    hcar_sc[...] = carry
    _run_next(side)
    y_lru = hs * _gelu_tanh(proj(CD_GB, MIX_W))

    pr = proj(CD_PR, CD_PR_COLS)
    prev = pltpu.roll(pr, 1, axis=0)
    row8 = lax.broadcasted_iota(jnp.int32, (SUBLANES, 1), 0)
    prev = jnp.concatenate([jnp.where(row8 == 0, pcar_sc[...], prev[0:SUBLANES]), prev[SUBLANES:]], axis=0)
    pcar_sc[...] = pr[tm - 1:tm, :]
    ps = pr + (prev - pr) * mu_ref[...]
    _run_next(side)
    r = ps[:, 0:MIX_W]
    k = ps[:, MIX_W:2 * MIX_W]
    v = ps[:, 2 * MIX_W:3 * MIX_W]
    wl = ps[:, 3 * MIX_W:3 * MIX_W + LANES]
    al = ps[:, 3 * MIX_W + LANES:3 * MIX_W + 2 * LANES]
    gl = ps[:, 3 * MIX_W + 2 * LANES:3 * MIX_W + 4 * LANES]
    log_w = -math.exp(-0.5) * jax.nn.sigmoid(w0_ref[...] + _bdot(jnp.tanh(wl), w2_ref[...]))
    a = jax.nn.sigmoid(a0_ref[...] + _bdot(al, a2_ref[...]))
    g = _bdot(jax.nn.sigmoid(gl), g2_ref[...])
    _run_next(side)
    ones_blk = ones_ref[...]
    kk = k * kk_ref[...]
    kk = kk * lax.rsqrt(jnp.maximum(_head_sum(kk * kk, ones_blk), 1e-24))
    k = k * (1.0 + (a - 1.0) * ka_ref[...])
    c_len = RWKV_CHUNK
    pair = 2 * c_len
    n_chunks = tm // c_len
    n_pairs = MIX_W // LANES

    cs = _sel_dot_rhs(sel_ref[...], log_w, 3)
    last = [cs[(c + 1) * c_len - 1:(c + 1) * c_len, :] for c in range(n_chunks)]
    tot = jnp.concatenate([jnp.broadcast_to(t, (c_len, MIX_W)) for t in last], axis=0)
    e_neg = jnp.exp(-cs)
    e_end = jnp.exp(tot - cs)
    nb = -(kk * a)
    bt = kk * jnp.exp(cs - log_w)
    rt = r * jnp.exp(cs)
    ab = nb * e_neg
    kb = k * e_neg
    ae = nb * e_end
    ke = k * e_end

    lane = lax.broadcasted_iota(jnp.int32, (1, LANES), 1)
    m0 = lane < RWKV_HEAD_DIM
    trow = lax.broadcasted_iota(jnp.int32, (c_len, LANES), 0)
    scol = lax.broadcasted_iota(jnp.int32, (c_len, LANES), 1) % c_len
    strict = trow > scol
    incl = trow >= scol
    eye = (trow == scol).astype(F32)

    def merge_mask(s):
        return ((trow // s) == (scol // s) + 1) & ((trow // s) % 2 == 1)

    def blockdiag(t):
        return jnp.concatenate([jnp.where(m0, t, 0.0), jnp.where(m0, 0.0, t)], axis=0)

    def tile(t, c, j):
        return t[c * c_len:(c + 1) * c_len, j * LANES:(j + 1) * LANES]

    units = [(c, j) for c in range(n_chunks) for j in range(n_pairs)]
    vbd = {u: blockdiag(tile(v, *u)).astype(BF16) for u in units}
    n_mat, lhs_x, lhs_y, lhs_z, w_col = {}, {}, {}, {}, {}
    for u in units:
        bt_u = tile(bt, *u).astype(BF16)
        rt_u = tile(rt, *u).astype(BF16)
        gram = _bdot_nt(jnp.concatenate([bt_u, rt_u], axis=0),
                        jnp.concatenate([blockdiag(tile(ab, *u)), blockdiag(tile(kb, *u))], axis=0))
        n_mat[u] = jnp.where(strict, gram[0:c_len, 0:LANES], 0.0)
        lhs_x[u] = jnp.concatenate(
            [bt_u, jnp.where(strict, gram[0:c_len, LANES:2 * LANES], 0.0).astype(BF16)], axis=1)
        lhs_y[u] = jnp.concatenate(
            [rt_u, jnp.where(incl, gram[c_len:pair, 0:LANES], 0.0).astype(BF16),
             jnp.where(incl, gram[c_len:pair, LANES:2 * LANES], 0.0).astype(BF16)], axis=1)
        lhs_z[u] = jnp.concatenate(
            [blockdiag(tile(ae, *u)).T.astype(BF16), blockdiag(tile(ke, *u)).T.astype(BF16)], axis=1)
        c, j = u
        w_row = jnp.exp(last[c][:, j * LANES:(j + 1) * LANES])
        w_col[u] = jnp.broadcast_to(w_row, (pair, LANES)).T

    n_bd = {u: blockdiag(n_mat[u]).astype(BF16) for u in units}
    t_inv = {u: eye + jnp.where(merge_mask(1), n_mat[u], 0.0) for u in units}
    s = 2
    while s < c_len:
        mask = merge_mask(s)
        half = {u: jnp.dot(t_inv[u].astype(BF16), n_bd[u], preferred_element_type=F32) for u in units}
        t_inv = {u: t_inv[u] + jnp.where(mask, _bdot(half[u], blockdiag(t_inv[u])), 0.0) for u in units}
        s *= 2
    t_inv = {u: t_inv[u].astype(BF16) for u in units}

    y_rows = []
    for c in range(n_chunks):
        z_prev = [s_sc[j] for j in range(n_pairs)]
        zb = [z.astype(BF16) for z in z_prev]
        xs = [jnp.dot(lhs_x[(c, j)], jnp.concatenate([zb[j], vbd[(c, j)]], axis=0),
                      preferred_element_type=F32) for j in range(n_pairs)]
        _run_next(side)
        us = [jnp.dot(t_inv[(c, j)], blockdiag(xs[j]).astype(BF16), preferred_element_type=F32)
              for j in range(n_pairs)]
        _run_next(side)
        us = [blockdiag(u).astype(BF16) for u in us]
        tiles = []
        for j in range(n_pairs):
            tiles.append(jnp.dot(lhs_y[(c, j)], jnp.concatenate([zb[j], us[j], vbd[(c, j)]], axis=0),
                                 preferred_element_type=F32))
            s_sc[j] = z_prev[j] * w_col[(c, j)] + jnp.dot(
                lhs_z[(c, j)], jnp.concatenate([us[j], vbd[(c, j)]], axis=0), preferred_element_type=F32)
        y_rows.append(jnp.concatenate(tiles, axis=1))
    y = jnp.concatenate(y_rows, axis=0)
    inv_n = 1.0 / RWKV_HEAD_DIM
    mean = _head_sum(y, ones_blk) * inv_n
    yc = y - mean
    var = _head_sum(yc * yc, ones_blk) * inv_n
    y = yc * lax.rsqrt(var + RWKV_LN_EPS) * lng_ref[...] + lnb_ref[...]
    y = y + _head_sum(r * k * rk_ref[...], ones_blk) * v
    y_rwkv = y * g

    _run_all(side)
    return x + _bdot(jnp.concatenate([y_lru, y_rwkv], axis=1), woutb_sc[...])


def _cd_body(x_ref, xn_ref, g_ref, win_ref, lcw_ref, lcb_ref, wa_ref, ba_ref, wx_ref, bx_ref, lam_ref,
             mu_ref, w0_ref, w2_ref, a0_ref, a2_ref, g2_ref, kk_ref, ka_ref, rk_ref, lng_ref, lnb_ref,
             ones_ref, sel_ref, wout_ref, o_ref,
             ext_sc, hcar_sc, pcar_sc, s_sc, winb_sc, wlora_sc, woutb_sc, pa_sc, pb_sc, *, tm):
    steps = functools.partial(_projection_steps, g_ref=g_ref, wt_sc=winb_sc, tail_sc=wlora_sc)

    @pl.when(pl.program_id(0) == 0)
    def _():
        ext_sc[...] = jnp.zeros_like(ext_sc)
        hcar_sc[...] = jnp.zeros_like(hcar_sc)
        pcar_sc[...] = jnp.zeros_like(pcar_sc)
        s_sc[...] = jnp.zeros_like(s_sc)
        _cast_weight_rows(winb_sc, win_ref)
        _cast_weight(woutb_sc, wout_ref)
        wlora_sc[...] = jnp.zeros_like(wlora_sc)
        src = CD_LORA
        for dst, width in ((0, DECAY_LORA), (LANES, ICL_LORA), (2 * LANES, GATE_LORA)):
            wlora_sc[dst:dst + width, :] = win_ref[src:src + width, :].astype(BF16)
            src += width
        _run_all(steps(x_ref[0:tm, :], pa_sc))

    mix = functools.partial(
        _cd_mix, lcw_ref=lcw_ref, lcb_ref=lcb_ref, wa_ref=wa_ref, ba_ref=ba_ref, wx_ref=wx_ref, bx_ref=bx_ref,
        lam_ref=lam_ref, mu_ref=mu_ref, w0_ref=w0_ref, w2_ref=w2_ref, a0_ref=a0_ref, a2_ref=a2_ref, g2_ref=g2_ref,
        kk_ref=kk_ref, ka_ref=ka_ref, rk_ref=rk_ref, lng_ref=lng_ref, lnb_ref=lnb_ref, ones_ref=ones_ref,
        sel_ref=sel_ref, ext_sc=ext_sc, hcar_sc=hcar_sc, pcar_sc=pcar_sc, s_sc=s_sc, woutb_sc=woutb_sc, tm=tm)
    o_ref[0:tm, :] = mix(pa_sc, x_ref[0:tm, :], side=steps(x_ref[tm:2 * tm, :], pb_sc))
    o_ref[tm:2 * tm, :] = mix(pb_sc, x_ref[tm:2 * tm, :], side=steps(xn_ref[...], pa_sc))


def _mix_cd(x, norm_g, w_in_all, w_out_all, j, lru_conv_w, lru_conv_b, lru_wa, lru_ba, lru_wx, lru_bx,
            lru_lambda, mu, w0, w2, a0, a2, g2, k_k, k_a, r_k, ln_g, ln_b):
    t_len, d = x.shape
    tm = min(MIX_TM, t_len)

    def pad_lora(t, lo, width, padded):
        return jnp.pad(t[..., lo:lo + width], [(0, 0)] * (t.ndim - 1) + [(0, padded - width)])

    def lora_cols(t, base):
        return jnp.concatenate([
            pad_lora(t, base, DECAY_LORA, LANES),
            pad_lora(t, base + DECAY_LORA, ICL_LORA, LANES),
            pad_lora(t, base + DECAY_LORA + ICL_LORA, GATE_LORA, 2 * LANES)], axis=-1)

    mu_row = mu.reshape(1, -1)
    mu_pad = jnp.concatenate([mu_row[:, :3 * MIX_W], lora_cols(mu_row, 3 * MIX_W)], axis=-1)
    pad_rows = lambda t, n: jnp.pad(t, ((0, n - t.shape[0]), (0, 0))).astype(BF16)
    row = lambda t: t.reshape(1, -1).astype(F32)
    ones_blk = ((jnp.arange(LANES)[:, None] // RWKV_HEAD_DIM) ==
                (jnp.arange(LANES)[None, :] // RWKV_HEAD_DIM)).astype(BF16)
    ones_blk = jnp.concatenate([ones_blk] * 2, axis=0)
    t_idx = jnp.arange(tm)
    same_chunk = (t_idx[:, None] // RWKV_CHUNK) == (t_idx[None, :] // RWKV_CHUNK)
    sel = (same_chunk & (t_idx[:, None] >= t_idx[None, :])).astype(BF16)

    operands = [
        (x, pl.BlockSpec((2 * tm, d), lambda i: (i, 0))),
        (x, _next_tile_spec(tm, d, t_len // tm)),
        (row(norm_g), None),
        _layer_operand(jnp.swapaxes(w_in_all, 1, 2), j),
        (lru_conv_w, None), (row(lru_conv_b), None),
        (lru_wa.astype(BF16), None), (row(lru_ba), None),
        (lru_wx.astype(BF16), None), (row(lru_bx), None),
        (row(lru_lambda), None),
        (mu_pad, None),
        (row(w0), None), (pad_rows(w2, LANES), None),
        (row(a0), None), (pad_rows(a2, LANES), None),
        (pad_rows(g2, 2 * LANES), None),
        (row(k_k), None), (row(k_a), None), (row(r_k), None), (row(ln_g), None), (row(ln_b), None),
        (ones_blk, None), (sel, None),
        _layer_operand(w_out_all, j),
    ]
    args = [a for a, _ in operands]
    specs = [s if s is not None else _resident_spec(a.shape) for a, s in operands]
    return pl.pallas_call(
        functools.partial(_cd_body, tm=tm),
        grid=(t_len // (2 * tm),),
        in_specs=specs,
        out_specs=pl.BlockSpec((2 * tm, d), lambda i: (i, 0)),
        out_shape=jax.ShapeDtypeStruct((t_len, d), F32),
        scratch_shapes=[
            pltpu.VMEM((SUBLANES, MIX_W), F32),
            pltpu.VMEM((1, MIX_W), F32),
            pltpu.VMEM((1, CD_PR_COLS), F32),
            pltpu.VMEM((MIX_W // LANES, LANES, LANES), F32),
            pltpu.VMEM((CD_LORA, d), BF16),
            pltpu.VMEM((CD_PR + CD_PR_COLS - CD_LORA, d), BF16),
            pltpu.VMEM((2 * MIX_W, d), BF16),
            pltpu.VMEM((tm, CD_PR + CD_PR_COLS), F32),
            pltpu.VMEM((tm, CD_PR + CD_PR_COLS), F32),
        ],
        compiler_params=pltpu.CompilerParams(
            dimension_semantics=("arbitrary",), vmem_limit_bytes=VMEM_LIMIT),
        name="mix_lru_rwkv",
    )(*args)


def kernel(x, ffn1_norm, ffn1_wg, ffn1_wu, ffn1_wd, mix_norm, ffn2_norm, ffn2_wg, ffn2_wu, ffn2_wd,
           ab_w_in, ab_w_out, ssm_conv_w, ssm_conv_b, ssm_dt_bias, ssm_a_log, ssm_d, ssm_norm,
           cd_w_in, cd_w_out, lru_conv_w, lru_conv_b, lru_wa, lru_ba, lru_wx, lru_bx, lru_lambda,
           rwkv_mu, rwkv_w0, rwkv_w2, rwkv_a0, rwkv_a2, rwkv_g2, rwkv_kk, rwkv_ka, rwkv_rk,
           rwkv_ln_g, rwkv_ln_b, final_norm):
    bsz, t_len, d = x.shape
    depth = ffn1_norm.shape[0]
    outs = []
    for b in range(bsz):
        xb = x.reshape(t_len, d) if bsz == 1 else x[b]
        for layer in range(depth):
            j = layer // 2
            xb = _ffn(xb, ffn1_norm[layer], ffn1_wg, ffn1_wu, ffn1_wd, layer, final_norm, final_norm=False)
            if layer % 2 == 0:
                xb = _mix_ab(xb, mix_norm[layer], ab_w_in, ab_w_out, j, ssm_conv_w[j], ssm_conv_b[j],
                             ssm_dt_bias[j], ssm_a_log[j], ssm_d[j], ssm_norm[j])
            else:
                xb = _mix_cd(xb, mix_norm[layer], cd_w_in, cd_w_out, j, lru_conv_w[j], lru_conv_b[j], lru_wa[j],
                             lru_ba[j], lru_wx[j], lru_bx[j], lru_lambda[j], rwkv_mu[j], rwkv_w0[j],
                             rwkv_w2[j], rwkv_a0[j], rwkv_a2[j], rwkv_g2[j], rwkv_kk[j], rwkv_ka[j],
                             rwkv_rk[j].reshape(-1), rwkv_ln_g[j], rwkv_ln_b[j])
            xb = _ffn(xb, ffn2_norm[layer], ffn2_wg, ffn2_wu, ffn2_wd, layer, final_norm,
                      final_norm=(layer == depth - 1))
        outs.append(xb)
    return outs[0].reshape(1, t_len, d) if bsz == 1 else jnp.stack(outs, axis=0)
```

```python
import functools
import math

import jax
import jax.numpy as jnp
import numpy as np
from jax import lax
from jax.experimental import pallas as pl
from jax.experimental.pallas import tpu as pltpu

F32 = jnp.float32
BF16 = jnp.bfloat16

D_FF = 2816
NORM_EPS = 1e-6
MIX_W = 512

RET_HEADS = 4
RET_DK = 128
RET_CHUNK = 128
ROPE_BASE = 10000.0
HEAD_NORM_EPS = 1e-6

SSM_HEADS = 8
SSM_HEAD_DIM = 64
SSM_GROUPS = 2
SSM_STATE = 64
SSM_CONV = 4
SSM_CHUNK = 128
SSM_XBC = 768
SSM_NORM_EPS = 1e-5

LRU_BLOCKS = 4
LRU_BLOCK = 128
LRU_CONV = 4
LRU_C = 8.0

RWKV_HEAD_DIM = 64
DECAY_LORA = 64
ICL_LORA = 64
GATE_LORA = 160
RWKV_LN_EPS = 64e-5
RWKV_CHUNK = 64

LANES = 128
SUBLANES = 8
VMEM_LIMIT = 56 * 1024 * 1024

FFN_TM = 512
FFN_TF = 256
MXU_DIM = 256
FFN_TK = (0, 6 * MXU_DIM, D_FF)
FFN_DMA_DEPTH = 8
MIX_TM = 256
PROJ_TN = 256

AB_Q, AB_K, AB_V, AB_G, AB_Z, AB_XBC, AB_DT = 0, 512, 1024, 1536, 2048, 2560, 3328
CD_XB, CD_GB, CD_PR, CD_LORA = 0, 512, 1024, 2560
CD_PR_COLS = 2048


def _bdot(a, b):
    return jnp.dot(a.astype(BF16), b.astype(BF16), preferred_element_type=F32)


def _bdot_nt(a, b):
    return lax.dot_general(a.astype(BF16), b.astype(BF16), (((1,), (1,)), ((), ())),
                           preferred_element_type=F32)


def _bdot_tn(a, b):
    return lax.dot_general(a.astype(BF16), b.astype(BF16), (((0,), (0,)), ((), ())),
                           preferred_element_type=F32)


def _split(x, pieces):
    out = []
    for _ in range(pieces - 1):
        p = x.astype(BF16)
        out.append(p)
        x = x - p.astype(F32)
    out.append(x.astype(BF16))
    return out


def _sel_dot_lhs(x, sel_stack, pieces):
    return jnp.dot(jnp.concatenate(_split(x, pieces), axis=1), sel_stack, preferred_element_type=F32)


def _sel_dot_rhs(sel, x, pieces):
    n = x.shape[1]
    res = jnp.dot(sel, jnp.concatenate(_split(x, pieces), axis=1), preferred_element_type=F32)
    out = res[:, 0:n]
    for p in range(1, pieces):
        out = out + res[:, p * n:(p + 1) * n]
    return out


def _rms_norm(x, g, eps):
    return x * lax.rsqrt(jnp.mean(x * x, axis=-1, keepdims=True) + eps) * g


def _silu(x):
    return x * jax.nn.sigmoid(x)


def _softplus(x):
    return jnp.maximum(x, 0.0) + jnp.log1p(jnp.exp(-jnp.abs(x)))


def _sqrt_nonneg(y):
    return y * lax.rsqrt(jnp.maximum(y, jnp.finfo(jnp.float32).tiny))


def _gelu_tanh(x):
    c = math.sqrt(2.0 / math.pi)
    return 0.5 * x * (1.0 + jnp.tanh(c * (x + 0.044715 * (x * x * x))))


def _causal_conv(tail_ref, x, w_ref, b_ref, tm, width):
    tail = tail_ref[...]
    row = lax.broadcasted_iota(jnp.int32, (SUBLANES, 1), 0)
    acc = b_ref[...] + w_ref[width - 1:width, :] * x
    for s in range(1, width):
        delayed = pltpu.roll(x, s, axis=0)
        head = jnp.where(row < s, pltpu.roll(tail, s, axis=0), delayed[0:SUBLANES])
        delayed = jnp.concatenate([head, delayed[SUBLANES:]], axis=0)
        acc = acc + w_ref[width - 1 - s:width - s, :] * delayed
    tail_ref[...] = x[tm - SUBLANES:tm]
    return acc


def _ffn_body(x_ref, g_ref, wg_hbm, wu_hbm, wd_hbm, fin_ref, o_ref, act_sc, wg_sc, wu_sc, wd_sc, sem,
              *, final_norm, layer):
    n_sec = D_FF // FFN_TF

    def column_copy(k, hbm, sc, j):
        cols = slice(j * FFN_TF, (j + 1) * FFN_TF)
        return pltpu.make_async_copy(hbm.at[layer, :, cols], sc.at[:, cols], sem.at[k, j])

    def row_copy(j):
        rows = slice(j * FFN_TF, (j + 1) * FFN_TF)
        return pltpu.make_async_copy(wd_hbm.at[layer, rows, :], wd_sc.at[rows, :], sem.at[2, j])

    copies = [c for j in range(n_sec) for c in (functools.partial(column_copy, 0, wg_hbm, wg_sc, j),
                                                functools.partial(column_copy, 1, wu_hbm, wu_sc, j))]
    copies += [functools.partial(row_copy, j) for j in range(n_sec)]

    def wait_copies(lo, hi):
        for t in range(lo, hi):
            copies[t]().wait()
            if t + FFN_DMA_DEPTH < len(copies):
                copies[t + FFN_DMA_DEPTH]().start()

    def step(first):
        if first:
            for t in range(FFN_DMA_DEPTH):
                copies[t]().start()
        x = x_ref[...]
        h = _rms_norm(x, g_ref[...], NORM_EPS).astype(BF16)
        for j in range(n_sec):
            cols = slice(j * FFN_TF, (j + 1) * FFN_TF)
            if first:
                wait_copies(2 * j, 2 * j + 2)
            gate = jnp.dot(h, wg_sc[:, cols].astype(BF16), preferred_element_type=F32)
            up = jnp.dot(h, wu_sc[:, cols].astype(BF16), preferred_element_type=F32)
            act_sc[:, cols] = (_silu(gate) * up).astype(BF16)
        down = None
        for lo, hi in zip(FFN_TK[:-1], FFN_TK[1:]):
            if first:
                wait_copies(2 * n_sec + lo // FFN_TF, 2 * n_sec + hi // FFN_TF)
            part = jnp.dot(act_sc[:, lo:hi], wd_sc[lo:hi, :].astype(BF16), preferred_element_type=F32)
            down = part if down is None else down + part
        y = x + 0.5 * down
        if final_norm:
            y = _rms_norm(y, fin_ref[...], NORM_EPS)
        o_ref[...] = y

    pl.when(pl.program_id(0) == 0)(functools.partial(step, True))
    pl.when(pl.program_id(0) > 0)(functools.partial(step, False))


def _resident_spec(shape):
    nd = len(shape)
    return pl.BlockSpec(shape, lambda i, _nd=nd: (0,) * _nd, pipeline_mode=pl.Buffered(1))


def _next_tile_spec(tm, d, n_tiles):
    return pl.BlockSpec((tm, d), lambda i: (jnp.minimum(2 * i + 2, n_tiles - 1), 0))


def _layer_spec(shape, layer):
    nd = len(shape)
    return pl.BlockSpec((None,) + tuple(shape), lambda i, _nd=nd: (layer,) + (0,) * _nd,
                        pipeline_mode=pl.Buffered(1))


def _layer_operand(stack, layer):
    if stack.shape[0] == 1:
        return stack.reshape(stack.shape[1:]), _resident_spec(stack.shape[1:])
    return stack, _layer_spec(stack.shape[1:], layer)


def _ffn(x, norm_g, wg, wu, wd, layer, fin_g, *, final_norm):
    t_len, d = x.shape
    tm = min(FFN_TM, t_len)
    hbm = pl.BlockSpec(memory_space=pl.ANY)
    return pl.pallas_call(
        functools.partial(_ffn_body, final_norm=final_norm, layer=layer),
        grid=(t_len // tm,),
        in_specs=[
            pl.BlockSpec((tm, d), lambda i: (i, 0)),
            _resident_spec((1, d)),
            hbm, hbm, hbm,
            _resident_spec((1, d)),
        ],
        out_specs=pl.BlockSpec((tm, d), lambda i: (i, 0)),
        out_shape=jax.ShapeDtypeStruct((t_len, d), F32),
        scratch_shapes=[
            pltpu.VMEM((tm, D_FF), BF16),
            pltpu.VMEM((d, D_FF), F32), pltpu.VMEM((d, D_FF), F32), pltpu.VMEM((D_FF, d), F32),
            pltpu.SemaphoreType.DMA((3, D_FF // FFN_TF)),
        ],
        compiler_params=pltpu.CompilerParams(
            dimension_semantics=("arbitrary",), vmem_limit_bytes=VMEM_LIMIT),
        name="ffn_final" if final_norm else "ffn",
    )(x, norm_g.reshape(1, d), wg, wu, wd, fin_g.reshape(1, d))


def _cast_weight(dst_sc, src_ref):
    for lo in range(0, dst_sc.shape[1], MXU_DIM):
        hi = min(lo + MXU_DIM, dst_sc.shape[1])
        dst_sc[:, lo:hi] = src_ref[:, lo:hi].astype(BF16)


def _cast_weight_rows(dst_sc, src_ref):
    for lo in range(0, dst_sc.shape[0], MXU_DIM):
        hi = min(lo + MXU_DIM, dst_sc.shape[0])
        dst_sc[lo:hi, :] = src_ref[lo:hi, :].astype(BF16)


def _projection_steps(x, p_ref, g_ref, wt_sc, tail_sc):
    h = _rms_norm(x, g_ref[...], NORM_EPS).astype(BF16)
    main = wt_sc.shape[0]
    nt_dot = lambda w: lax.dot_general(h, w, (((1,), (1,)), ((), ())), preferred_element_type=F32)

    def section(lo, hi):
        p_ref[:, lo:hi] = nt_dot(wt_sc[lo:hi, :])

    def tail():
        p_ref[:, main:main + tail_sc.shape[0]] = nt_dot(tail_sc[...])

    steps = [functools.partial(section, lo, min(lo + PROJ_TN, main)) for lo in range(0, main, PROJ_TN)]
    return steps + [tail]


def _run_next(steps):
    if steps:
        steps.pop(0)()


def _run_all(steps):
    while steps:
        _run_next(steps)


def _ab_mix(p_ref, x, rota, rotb_ref, rdec_ref, rqfs_ref, rkte_ref, rcd_ref,
            cw_ref, cb_ref, dtb_ref, alog_ref, dexp_ref, sng_ref, exp_ref, tri_ref,
            sret_sc, sssm_sc, ext_sc, woutb_sc, tm, side):
    def proj(lo, width):
        return p_ref[:, lo:lo + width]

    cos_a, sin_a = rota[0:1, :], rota[1:2, :]
    sin_a_sgn, cos_a_sgn = rota[2:3, :], rota[3:4, :]
    cos_b, sin_b = rotb_ref[:, 0:RET_DK], rotb_ref[:, RET_DK:2 * RET_DK]
    cos2 = cos_a * cos_b - sin_a * sin_b
    sin2 = sin_a_sgn * cos_b + cos_a_sgn * sin_b

    q_all = proj(AB_Q, MIX_W)
    k_all = proj(AB_K, MIX_W)
    v_all = proj(AB_V, MIX_W)
    gate_ret = _silu(proj(AB_G, MIX_W))
    scale = RET_DK ** -0.5
    c_len = RET_CHUNK
    n_chunks = tm // c_len
    cells = [(c, hh) for c in range(n_chunks) for hh in range(RET_HEADS)]
    rows_of = lambda c: slice(c * c_len, (c + 1) * c_len)
    lanes_of = lambda hh: slice(hh * RET_DK, (hh + 1) * RET_DK)

    def rotate(t, c, hh):
        t = t[rows_of(c), lanes_of(hh)]
        return t * cos2[rows_of(c)] + pltpu.roll(t, RET_DK // 2, axis=1) * sin2[rows_of(c)]

    qh = {u: rotate(q_all, *u) for u in cells}
    kh = {u: rotate(k_all, *u) * scale for u in cells}
    vh = {(c, hh): v_all[rows_of(c), lanes_of(hh)].astype(BF16) for c, hh in cells}
    scores = {(c, hh): _bdot_nt(qh[(c, hh)], kh[(c, hh)]) * rdec_ref[hh] for c, hh in cells}
    kv = {(c, hh): _bdot_tn(kh[(c, hh)] * rkte_ref[:, lanes_of(hh)], vh[(c, hh)]) for c, hh in cells}
    state = {}
    for hh in range(RET_HEADS):
        s_cur = sret_sc[hh]
        for c in range(n_chunks):
            state[(c, hh)] = s_cur
            s_cur = s_cur * rcd_ref[:, lanes_of(hh)] + kv[(c, hh)]
        sret_sc[hh] = s_cur
    outs = {(c, hh): _bdot(jnp.concatenate([scores[(c, hh)], qh[(c, hh)] * rqfs_ref[:, lanes_of(hh)]], axis=1),
                           jnp.concatenate([vh[(c, hh)], state[(c, hh)].astype(BF16)], axis=0))
            for c, hh in cells}
    outs = {u: o * lax.rsqrt(jnp.mean(o * o, axis=-1, keepdims=True) + HEAD_NORM_EPS) for u, o in outs.items()}
    y_ret = gate_ret * jnp.concatenate(
        [jnp.concatenate([outs[(c, hh)] for hh in range(RET_HEADS)], axis=1) for c in range(n_chunks)], axis=0)

    z = proj(AB_Z, MIX_W)
    xbc = _silu(_causal_conv(ext_sc, proj(AB_XBC, SSM_XBC), cw_ref, cb_ref, tm, SSM_CONV))
    xs = xbc[:, :MIX_W]
    bm = xbc[:, MIX_W:MIX_W + LANES]
    cm = xbc[:, MIX_W + LANES:MIX_W + 2 * LANES]
    lane = lax.broadcasted_iota(jnp.int32, (1, LANES), 1)
    dt = _softplus(proj(AB_DT, LANES) + dtb_ref[...])
    a_neg = jnp.where(lane < SSM_HEADS, -jnp.exp(alog_ref[...]), 0.0)
    adt = dt * a_neg
    xdt = xs * _sel_dot_lhs(dt, exp_ref[0:2 * LANES, :], 2)
    acs_all = _sel_dot_rhs(tri_ref[...], adt, 3)
    acs_e_all = _sel_dot_lhs(acs_all, exp_ref[...], 3)
    l_len = SSM_CHUNK
    row = lax.broadcasted_iota(jnp.int32, (l_len, l_len), 0)
    col = lax.broadcasted_iota(jnp.int32, (l_len, l_len), 1)
    causal = row >= col
    first_half = lax.broadcasted_iota(jnp.int32, (1, LANES), 1) < SSM_HEAD_DIM
    grp_mask = [first_half, jnp.logical_not(first_half)]
    srow = lax.broadcasted_iota(jnp.int32, (LANES, MIX_W), 0)
    scol = lax.broadcasted_iota(jnp.int32, (LANES, MIX_W), 1)
    state_mask = (srow // SSM_STATE) == (scol // (MIX_W // SSM_GROUPS))
    s_cur = sssm_sc[...]
    s_enter = []
    for c in range(tm // l_len):
        rs = slice(c * l_len, (c + 1) * l_len)
        last_e = acs_e_all[(c + 1) * l_len - 1:(c + 1) * l_len, :]
        upd = _bdot_tn(bm[rs], xdt[rs] * jnp.exp(last_e - acs_e_all[rs]))
        s_enter.append(s_cur)
        s_cur = s_cur * jnp.exp(last_e) + jnp.where(state_mask, upd, 0.0)
    sssm_sc[...] = s_cur

    ssm_rows = []
    for c in range(tm // l_len):
        rs = slice(c * l_len, (c + 1) * l_len)
        acs = acs_all[rs]
        acs_e = acs_e_all[rs]
        acs_t = acs.T
        bm_c = bm[rs]
        cm_c = cm[rs]
        y_off = _bdot(cm_c, s_enter[c]) * jnp.exp(acs_e)
        cb = [jnp.where(causal, _bdot_nt(jnp.where(grp_mask[g], cm_c, 0.0), bm_c), 0.0)
              for g in range(SSM_GROUPS)]
        tiles = []
        for j in range(MIX_W // LANES):
            xt = xdt[rs, j * LANES:(j + 1) * LANES]
            weights = []
            for hh in (2 * j, 2 * j + 1):
                seg = acs[:, hh:hh + 1] - acs_t[hh:hh + 1, :]
                weights.append((cb[hh // (SSM_HEADS // SSM_GROUPS)] * jnp.exp(jnp.minimum(seg, 0.0))).astype(BF16))
                _run_next(side)
            tiles.append(jnp.dot(
                jnp.concatenate(weights, axis=1),
                jnp.concatenate([jnp.where(first_half, xt, 0.0), jnp.where(first_half, 0.0, xt)], axis=0).astype(BF16),
                preferred_element_type=F32))
        y_diag = jnp.concatenate(tiles, axis=1)
        y = y_diag + y_off + dexp_ref[...] * xs[rs]
        y = y * _silu(z[rs])
        halves = []
        for g in range(SSM_GROUPS):
            yg = y[:, g * (MIX_W // SSM_GROUPS):(g + 1) * (MIX_W // SSM_GROUPS)]
            halves.append(yg * lax.rsqrt(jnp.mean(yg * yg, axis=-1, keepdims=True) + SSM_NORM_EPS))
        ssm_rows.append(jnp.concatenate(halves, axis=1) * sng_ref[...])
    y_ssm = jnp.concatenate(ssm_rows, axis=0)

    _run_all(side)
    return x + _bdot(jnp.concatenate([y_ret, y_ssm], axis=1), woutb_sc[...])


def _ab_body(x_ref, xn_ref, rota_ref, rotb_ref, g_ref, win_ref, rdec_ref, rqfs_ref, rkte_ref, rcd_ref,
             cw_ref, cb_ref, dtb_ref, alog_ref, dexp_ref, sng_ref, exp_ref, tri_ref, wout_ref,
             o_ref, sret_sc, sssm_sc, ext_sc, winb_sc, wdt_sc, woutb_sc, pa_sc, pb_sc, *, tm):
    steps = functools.partial(_projection_steps, g_ref=g_ref, wt_sc=winb_sc, tail_sc=wdt_sc)

    @pl.when(pl.program_id(0) == 0)
    def _():
        sret_sc[...] = jnp.zeros_like(sret_sc)
        sssm_sc[...] = jnp.zeros_like(sssm_sc)
        ext_sc[...] = jnp.zeros_like(ext_sc)
        _cast_weight_rows(winb_sc, win_ref)
        _cast_weight(woutb_sc, wout_ref)
        wdt_sc[...] = jnp.zeros_like(wdt_sc)
        wdt_sc[0:SSM_HEADS, :] = win_ref[AB_DT:AB_DT + SSM_HEADS, :].astype(BF16)
        _run_all(steps(x_ref[0:tm, :], pa_sc))

    mix = functools.partial(
        _ab_mix, rotb_ref=rotb_ref, rdec_ref=rdec_ref, rqfs_ref=rqfs_ref, rkte_ref=rkte_ref, rcd_ref=rcd_ref,
        cw_ref=cw_ref, cb_ref=cb_ref, dtb_ref=dtb_ref, alog_ref=alog_ref, dexp_ref=dexp_ref, sng_ref=sng_ref,
        exp_ref=exp_ref, tri_ref=tri_ref, sret_sc=sret_sc, sssm_sc=sssm_sc, ext_sc=ext_sc, woutb_sc=woutb_sc,
        tm=tm)
    o_ref[0:tm, :] = mix(pa_sc, x_ref[0:tm, :], rota_ref[0], side=steps(x_ref[tm:2 * tm, :], pb_sc))
    o_ref[tm:2 * tm, :] = mix(pb_sc, x_ref[tm:2 * tm, :], rota_ref[1], side=steps(xn_ref[...], pa_sc))


def _rotary_tables(n_tiles, tm):
    inv_freq = ROPE_BASE ** (-np.arange(0, RET_DK, 2, dtype=np.float64) / RET_DK)
    two = lambda t: np.concatenate([t, t], axis=-1)
    sign = np.concatenate([-np.ones(RET_DK // 2), np.ones(RET_DK // 2)])
    ang_a = (np.arange(n_tiles, dtype=np.float64) * tm)[:, None] * inv_freq[None, :]
    ang_b = np.arange(tm, dtype=np.float64)[:, None] * inv_freq[None, :]
    cos_a, sin_a = two(np.cos(ang_a)), two(np.sin(ang_a))
    rot_a = np.zeros((n_tiles, SUBLANES, RET_DK))
    rot_a[:, 0], rot_a[:, 1], rot_a[:, 2], rot_a[:, 3] = cos_a, sin_a, sign * sin_a, sign * cos_a
    rot_b = np.concatenate([two(np.cos(ang_b)), two(np.sin(ang_b))], axis=1)
    return jnp.asarray(rot_a, F32), jnp.asarray(rot_b, F32)


def _retention_tables():
    c = RET_CHUNK
    log_g = np.log1p(-(2.0 ** (-5.0 - np.arange(RET_HEADS, dtype=np.float64))))
    pos = np.arange(c, dtype=np.float64)
    rel = pos[:, None] - pos[None, :]
    rdec = np.where(rel >= 0, np.exp(np.maximum(rel, 0.0)[None] * log_g[:, None, None]), 0.0)
    per_head = lambda t: np.repeat(t, RET_DK, axis=1)
    rkte = per_head(np.exp((c - 1 - pos)[:, None] * log_g[None, :]))
    rqfs = per_head(np.exp((pos + 1.0)[:, None] * log_g[None, :]))
    rcd = per_head(np.exp(c * log_g)[None, :])
    return tuple(jnp.asarray(t, F32) for t in (rdec, rqfs, rkte, rcd))


def _mix_ab(x, norm_g, w_in_all, w_out_all, j, conv_w, conv_b, dt_bias, a_log, d_skip, norm_ssm):
    t_len, d = x.shape
    tm = min(MIX_TM, t_len)
    rot_a, rot_b = _rotary_tables(t_len // tm, tm)
    rdec, rqfs, rkte, rcd = _retention_tables()

    pad_row = lambda v: jnp.pad(v.astype(F32), (0, LANES - v.shape[0])).reshape(1, LANES)
    expand = (jnp.arange(LANES)[:, None] == (jnp.arange(MIX_W)[None, :] // SSM_HEAD_DIM)).astype(BF16)
    expand = jnp.concatenate([expand] * 3, axis=0)
    t_idx = jnp.arange(tm)
    tri = (((t_idx[:, None] // SSM_CHUNK) == (t_idx[None, :] // SSM_CHUNK)) &
           (t_idx[:, None] >= t_idx[None, :])).astype(BF16)

    operands = [
        (x, pl.BlockSpec((2 * tm, d), lambda i: (i, 0))),
        (x, _next_tile_spec(tm, d, t_len // tm)),
        (rot_a, pl.BlockSpec((2, SUBLANES, RET_DK), lambda i: (i, 0, 0))),
        (rot_b, None),
        (norm_g.reshape(1, d), None),
        _layer_operand(jnp.swapaxes(w_in_all, 1, 2), j),
        (rdec, None), (rqfs, None), (rkte, None), (rcd, None),
        (conv_w, None), (conv_b.reshape(1, SSM_XBC), None),
        (pad_row(dt_bias), None), (pad_row(a_log), None),
        (jnp.repeat(d_skip.astype(F32), SSM_HEAD_DIM).reshape(1, MIX_W), None),
        (norm_ssm.reshape(1, MIX_W), None),
        (expand, None), (tri, None),
        _layer_operand(w_out_all, j),
    ]
    args = [a for a, _ in operands]
    specs = [s if s is not None else _resident_spec(a.shape) for a, s in operands]
    return pl.pallas_call(
        functools.partial(_ab_body, tm=tm),
        grid=(t_len // (2 * tm),),
        in_specs=specs,
        out_specs=pl.BlockSpec((2 * tm, d), lambda i: (i, 0)),
        out_shape=jax.ShapeDtypeStruct((t_len, d), F32),
        scratch_shapes=[
            pltpu.VMEM((RET_HEADS, RET_DK, RET_DK), F32),
            pltpu.VMEM((LANES, MIX_W), F32),
            pltpu.VMEM((SUBLANES, SSM_XBC), F32),
            pltpu.VMEM((AB_DT, d), BF16),
            pltpu.VMEM((LANES, d), BF16),
            pltpu.VMEM((2 * MIX_W, d), BF16),
            pltpu.VMEM((tm, AB_DT + LANES), F32),
            pltpu.VMEM((tm, AB_DT + LANES), F32),
        ],
        compiler_params=pltpu.CompilerParams(
            dimension_semantics=("arbitrary",), vmem_limit_bytes=VMEM_LIMIT),
        name="mix_ret_ssd",
    )(*args)


def _head_sum(x, ones_blk):
    tiles = [_sel_dot_lhs(x[:, j * LANES:(j + 1) * LANES], ones_blk, 2) for j in range(x.shape[1] // LANES)]
    return jnp.concatenate(tiles, axis=1)


def _cd_mix(p_ref, x, lcw_ref, lcb_ref, wa_ref, ba_ref, wx_ref, bx_ref, lam_ref,
            mu_ref, w0_ref, w2_ref, a0_ref, a2_ref, g2_ref, kk_ref, ka_ref, rk_ref, lng_ref, lnb_ref,
            ones_ref, sel_ref, ext_sc, hcar_sc, pcar_sc, s_sc, woutb_sc, tm, side):
    def proj(lo, width):
        return p_ref[:, lo:lo + width]

    rows = lax.broadcasted_iota(jnp.int32, (tm, 1), 0)

    xc = _causal_conv(ext_sc, proj(CD_XB, MIX_W), lcw_ref, lcb_ref, tm, LRU_CONV)
    _run_next(side)
    r_parts, i_parts = [], []
    for n in range(LRU_BLOCKS):
        ls = slice(n * LRU_BLOCK, (n + 1) * LRU_BLOCK)
        r_parts.append(jax.nn.sigmoid(_bdot(xc[:, ls], wa_ref[n]) + ba_ref[:, ls]))
        i_parts.append(jax.nn.sigmoid(_bdot(xc[:, ls], wx_ref[n]) + bx_ref[:, ls]))
    r_gate = jnp.concatenate(r_parts, axis=1)
    i_gate = jnp.concatenate(i_parts, axis=1)
    log_a = -LRU_C * r_gate * _softplus(-lam_ref[...])
    a_cum = jnp.exp(log_a)
    hs = _sqrt_nonneg(-jnp.tanh(log_a) * (1.0 + a_cum * a_cum)) * (i_gate * xc)
    _run_next(side)
    in_group = rows % SUBLANES
    shift = 1
    while shift < SUBLANES:
        valid = in_group >= shift
        a_prev = jnp.where(valid, pltpu.roll(a_cum, shift, axis=0), 1.0)
        h_prev = jnp.where(valid, pltpu.roll(hs, shift, axis=0), 0.0)
        hs = hs + a_cum * h_prev
        a_cum = a_cum * a_prev
        shift *= 2
    carry = hcar_sc[...]
    groups = []
    for gi in range(tm // SUBLANES):
        grp = slice(gi * SUBLANES, (gi + 1) * SUBLANES)
        hg = hs[grp] + a_cum[grp] * carry
        groups.append(hg)
        carry = hg[SUBLANES - 1:SUBLANES, :]
    hs = jnp.concatenate(groups, axis=0)
    hcar_sc[...] = carry
    _run_next(side)
    y_lru = hs * _gelu_tanh(proj(CD_GB, MIX_W))

    pr = proj(CD_PR, CD_PR_COLS)
    prev = pltpu.roll(pr, 1, axis=0)
    row8 = lax.broadcasted_iota(jnp.int32, (SUBLANES, 1), 0)
    prev = jnp.concatenate([jnp.where(row8 == 0, pcar_sc[...], prev[0:SUBLANES]), prev[SUBLANES:]], axis=0)
    pcar_sc[...] = pr[tm - 1:tm, :]
    ps = pr + (prev - pr) * mu_ref[...]
    _run_next(side)
    r = ps[:, 0:MIX_W]
    k = ps[:, MIX_W:2 * MIX_W]
    v = ps[:, 2 * MIX_W:3 * MIX_W]
    wl = ps[:, 3 * MIX_W:3 * MIX_W + LANES]
    al = ps[:, 3 * MIX_W + LANES:3 * MIX_W + 2 * LANES]
    gl = ps[:, 3 * MIX_W + 2 * LANES:3 * MIX_W + 4 * LANES]
    log_w = -math.exp(-0.5) * jax.nn.sigmoid(w0_ref[...] + _bdot(jnp.tanh(wl), w2_ref[...]))
    a = jax.nn.sigmoid(a0_ref[...] + _bdot(al, a2_ref[...]))
    g = _bdot(jax.nn.sigmoid(gl), g2_ref[...])
    _run_next(side)
    ones_blk = ones_ref[...]
    kk = k * kk_ref[...]
    kk = kk * lax.rsqrt(jnp.maximum(_head_sum(kk * kk, ones_blk), 1e-24))
    k = k * (1.0 + (a - 1.0) * ka_ref[...])
    c_len = RWKV_CHUNK
    pair = 2 * c_len
    n_chunks = tm // c_len
    n_pairs = MIX_W // LANES

    cs = _sel_dot_rhs(sel_ref[...], log_w, 3)
    last = [cs[(c + 1) * c_len - 1:(c + 1) * c_len, :] for c in range(n_chunks)]
    tot = jnp.concatenate([jnp.broadcast_to(t, (c_len, MIX_W)) for t in last], axis=0)
    e_neg = jnp.exp(-cs)
    e_end = jnp.exp(tot - cs)
    nb = -(kk * a)
    bt = kk * jnp.exp(cs - log_w)
    rt = r * jnp.exp(cs)
    ab = nb * e_neg
    kb = k * e_neg
    ae = nb * e_end
    ke = k * e_end

    lane = lax.broadcasted_iota(jnp.int32, (1, LANES), 1)
    m0 = lane < RWKV_HEAD_DIM
    trow = lax.broadcasted_iota(jnp.int32, (c_len, LANES), 0)
    scol = lax.broadcasted_iota(jnp.int32, (c_len, LANES), 1) % c_len
    strict = trow > scol
    incl = trow >= scol
    eye = (trow == scol).astype(F32)

    def merge_mask(s):
        return ((trow // s) == (scol // s) + 1) & ((trow // s) % 2 == 1)

    def blockdiag(t):
        return jnp.concatenate([jnp.where(m0, t, 0.0), jnp.where(m0, 0.0, t)], axis=0)

    def tile(t, c, j):
        return t[c * c_len:(c + 1) * c_len, j * LANES:(j + 1) * LANES]

    units = [(c, j) for c in range(n_chunks) for j in range(n_pairs)]
    vbd = {u: blockdiag(tile(v, *u)).astype(BF16) for u in units}
    n_mat, lhs_x, lhs_y, lhs_z, w_col = {}, {}, {}, {}, {}
    for u in units:
        bt_u = tile(bt, *u).astype(BF16)
        rt_u = tile(rt, *u).astype(BF16)
        gram = _bdot_nt(jnp.concatenate([bt_u, rt_u], axis=0),
                        jnp.concatenate([blockdiag(tile(ab, *u)), blockdiag(tile(kb, *u))], axis=0))
        n_mat[u] = jnp.where(strict, gram[0:c_len, 0:LANES], 0.0)
        lhs_x[u] = jnp.concatenate(
            [bt_u, jnp.where(strict, gram[0:c_len, LANES:2 * LANES], 0.0).astype(BF16)], axis=1)
        lhs_y[u] = jnp.concatenate(
            [rt_u, jnp.where(incl, gram[c_len:pair, 0:LANES], 0.0).astype(BF16),
             jnp.where(incl, gram[c_len:pair, LANES:2 * LANES], 0.0).astype(BF16)], axis=1)
        lhs_z[u] = jnp.concatenate(
            [blockdiag(tile(ae, *u)).T.astype(BF16), blockdiag(tile(ke, *u)).T.astype(BF16)], axis=1)
        c, j = u
        w_row = jnp.exp(last[c][:, j * LANES:(j + 1) * LANES])
        w_col[u] = jnp.broadcast_to(w_row, (pair, LANES)).T

    n_bd = {u: blockdiag(n_mat[u]).astype(BF16) for u in units}
    t_inv = {u: eye + jnp.where(merge_mask(1), n_mat[u], 0.0) for u in units}
    s = 2
    while s < c_len:
        mask = merge_mask(s)
        half = {u: jnp.dot(t_inv[u].astype(BF16), n_bd[u], preferred_element_type=F32) for u in units}
        t_inv = {u: t_inv[u] + jnp.where(mask, _bdot(half[u], blockdiag(t_inv[u])), 0.0) for u in units}
        s *= 2
    t_inv = {u: t_inv[u].astype(BF16) for u in units}

    y_rows = []
    for c in range(n_chunks):
        z_prev = [s_sc[j] for j in range(n_pairs)]
        zb = [z.astype(BF16) for z in z_prev]
        xs = [jnp.dot(lhs_x[(c, j)], jnp.concatenate([zb[j], vbd[(c, j)]], axis=0),
                      preferred_element_type=F32) for j in range(n_pairs)]
        _run_next(side)
        us = [jnp.dot(t_inv[(c, j)], blockdiag(xs[j]).astype(BF16), preferred_element_type=F32)
              for j in range(n_pairs)]
        _run_next(side)
        us = [blockdiag(u).astype(BF16) for u in us]
        tiles = []
        for j in range(n_pairs):
            tiles.append(jnp.dot(lhs_y[(c, j)], jnp.concatenate([zb[j], us[j], vbd[(c, j)]], axis=0),
                                 preferred_element_type=F32))
            s_sc[j] = z_prev[j] * w_col[(c, j)] + jnp.dot(
                lhs_z[(c, j)], jnp.concatenate([us[j], vbd[(c, j)]], axis=0), preferred_element_type=F32)
        y_rows.append(jnp.concatenate(tiles, axis=1))
    y = jnp.concatenate(y_rows, axis=0)
    inv_n = 1.0 / RWKV_HEAD_DIM
    mean = _head_sum(y, ones_blk) * inv_n
    yc = y - mean
    var = _head_sum(yc * yc, ones_blk) * inv_n
    y = yc * lax.rsqrt(var + RWKV_LN_EPS) * lng_ref[...] + lnb_ref[...]
    y = y + _head_sum(r * k * rk_ref[...], ones_blk) * v
    y_rwkv = y * g

    _run_all(side)
    return x + _bdot(jnp.concatenate([y_lru, y_rwkv], axis=1), woutb_sc[...])


def _cd_body(x_ref, xn_ref, g_ref, win_ref, lcw_ref, lcb_ref, wa_ref, ba_ref, wx_ref, bx_ref, lam_ref,
             mu_ref, w0_ref, w2_ref, a0_ref, a2_ref, g2_ref, kk_ref, ka_ref, rk_ref, lng_ref, lnb_ref,
             ones_ref, sel_ref, wout_ref, o_ref,
             ext_sc, hcar_sc, pcar_sc, s_sc, winb_sc, wlora_sc, woutb_sc, pa_sc, pb_sc, *, tm):
    steps = functools.partial(_projection_steps, g_ref=g_ref, wt_sc=winb_sc, tail_sc=wlora_sc)

    @pl.when(pl.program_id(0) == 0)
    def _():
        ext_sc[...] = jnp.zeros_like(ext_sc)
        hcar_sc[...] = jnp.zeros_like(hcar_sc)
        pcar_sc[...] = jnp.zeros_like(pcar_sc)
        s_sc[...] = jnp.zeros_like(s_sc)
        _cast_weight_rows(winb_sc, win_ref)
        _cast_weight(woutb_sc, wout_ref)
        wlora_sc[...] = jnp.zeros_like(wlora_sc)
        src = CD_LORA
        for dst, width in ((0, DECAY_LORA), (LANES, ICL_LORA), (2 * LANES, GATE_LORA)):
            wlora_sc[dst:dst + width, :] = win_ref[src:src + width, :].astype(BF16)
            src += width
        _run_all(steps(x_ref[0:tm, :], pa_sc))

    mix = functools.partial(
        _cd_mix, lcw_ref=lcw_ref, lcb_ref=lcb_ref, wa_ref=wa_ref, ba_ref=ba_ref, wx_ref=wx_ref, bx_ref=bx_ref,
        lam_ref=lam_ref, mu_ref=mu_ref, w0_ref=w0_ref, w2_ref=w2_ref, a0_ref=a0_ref, a2_ref=a2_ref, g2_ref=g2_ref,
        kk_ref=kk_ref, ka_ref=ka_ref, rk_ref=rk_ref, lng_ref=lng_ref, lnb_ref=lnb_ref, ones_ref=ones_ref,
        sel_ref=sel_ref, ext_sc=ext_sc, hcar_sc=hcar_sc, pcar_sc=pcar_sc, s_sc=s_sc, woutb_sc=woutb_sc, tm=tm)
    o_ref[0:tm, :] = mix(pa_sc, x_ref[0:tm, :], side=steps(x_ref[tm:2 * tm, :], pb_sc))
    o_ref[tm:2 * tm, :] = mix(pb_sc, x_ref[tm:2 * tm, :], side=steps(xn_ref[...], pa_sc))


def _mix_cd(x, norm_g, w_in_all, w_out_all, j, lru_conv_w, lru_conv_b, lru_wa, lru_ba, lru_wx, lru_bx,
            lru_lambda, mu, w0, w2, a0, a2, g2, k_k, k_a, r_k, ln_g, ln_b):
    t_len, d = x.shape
    tm = min(MIX_TM, t_len)

    def pad_lora(t, lo, width, padded):
        return jnp.pad(t[..., lo:lo + width], [(0, 0)] * (t.ndim - 1) + [(0, padded - width)])

    def lora_cols(t, base):
        return jnp.concatenate([
            pad_lora(t, base, DECAY_LORA, LANES),
            pad_lora(t, base + DECAY_LORA, ICL_LORA, LANES),
            pad_lora(t, base + DECAY_LORA + ICL_LORA, GATE_LORA, 2 * LANES)], axis=-1)

    mu_row = mu.reshape(1, -1)
    mu_pad = jnp.concatenate([mu_row[:, :3 * MIX_W], lora_cols(mu_row, 3 * MIX_W)], axis=-1)
    pad_rows = lambda t, n: jnp.pad(t, ((0, n - t.shape[0]), (0, 0))).astype(BF16)
    row = lambda t: t.reshape(1, -1).astype(F32)
    ones_blk = ((jnp.arange(LANES)[:, None] // RWKV_HEAD_DIM) ==
                (jnp.arange(LANES)[None, :] // RWKV_HEAD_DIM)).astype(BF16)
    ones_blk = jnp.concatenate([ones_blk] * 2, axis=0)
    t_idx = jnp.arange(tm)
    same_chunk = (t_idx[:, None] // RWKV_CHUNK) == (t_idx[None, :] // RWKV_CHUNK)
    sel = (same_chunk & (t_idx[:, None] >= t_idx[None, :])).astype(BF16)

    operands = [
        (x, pl.BlockSpec((2 * tm, d), lambda i: (i, 0))),
        (x, _next_tile_spec(tm, d, t_len // tm)),
        (row(norm_g), None),
        _layer_operand(jnp.swapaxes(w_in_all, 1, 2), j),
        (lru_conv_w, None), (row(lru_conv_b), None),
        (lru_wa.astype(BF16), None), (row(lru_ba), None),
        (lru_wx.astype(BF16), None), (row(lru_bx), None),
        (row(lru_lambda), None),
        (mu_pad, None),
        (row(w0), None), (pad_rows(w2, LANES), None),
        (row(a0), None), (pad_rows(a2, LANES), None),
        (pad_rows(g2, 2 * LANES), None),
        (row(k_k), None), (row(k_a), None), (row(r_k), None), (row(ln_g), None), (row(ln_b), None),
        (ones_blk, None), (sel, None),
        _layer_operand(w_out_all, j),
    ]
    args = [a for a, _ in operands]
    specs = [s if s is not None else _resident_spec(a.shape) for a, s in operands]
    return pl.pallas_call(
        functools.partial(_cd_body, tm=tm),
        grid=(t_len // (2 * tm),),
        in_specs=specs,
        out_specs=pl.BlockSpec((2 * tm, d), lambda i: (i, 0)),
        out_shape=jax.ShapeDtypeStruct((t_len, d), F32),
        scratch_shapes=[
            pltpu.VMEM((SUBLANES, MIX_W), F32),
            pltpu.VMEM((1, MIX_W), F32),
            pltpu.VMEM((1, CD_PR_COLS), F32),
            pltpu.VMEM((MIX_W // LANES, LANES, LANES), F32),
            pltpu.VMEM((CD_LORA, d), BF16),
            pltpu.VMEM((CD_PR + CD_PR_COLS - CD_LORA, d), BF16),
            pltpu.VMEM((2 * MIX_W, d), BF16),
            pltpu.VMEM((tm, CD_PR + CD_PR_COLS), F32),
            pltpu.VMEM((tm, CD_PR + CD_PR_COLS), F32),
        ],
        compiler_params=pltpu.CompilerParams(
            dimension_semantics=("arbitrary",), vmem_limit_bytes=VMEM_LIMIT),
        name="mix_lru_rwkv",
    )(*args)


def kernel(x, ffn1_norm, ffn1_wg, ffn1_wu, ffn1_wd, mix_norm, ffn2_norm, ffn2_wg, ffn2_wu, ffn2_wd,
           ab_w_in, ab_w_out, ssm_conv_w, ssm_conv_b, ssm_dt_bias, ssm_a_log, ssm_d, ssm_norm,
           cd_w_in, cd_w_out, lru_conv_w, lru_conv_b, lru_wa, lru_ba, lru_wx, lru_bx, lru_lambda,
           rwkv_mu, rwkv_w0, rwkv_w2, rwkv_a0, rwkv_a2, rwkv_g2, rwkv_kk, rwkv_ka, rwkv_rk,
           rwkv_ln_g, rwkv_ln_b, final_norm):
    bsz, t_len, d = x.shape
    depth = ffn1_norm.shape[0]
    outs = []
    for b in range(bsz):
        xb = x.reshape(t_len, d) if bsz == 1 else x[b]
        for layer in range(depth):
            j = layer // 2
            xb = _ffn(xb, ffn1_norm[layer], ffn1_wg, ffn1_wu, ffn1_wd, layer, final_norm, final_norm=False)
            if layer % 2 == 0:
                xb = _mix_ab(xb, mix_norm[layer], ab_w_in, ab_w_out, j, ssm_conv_w[j], ssm_conv_b[j],
                             ssm_dt_bias[j], ssm_a_log[j], ssm_d[j], ssm_norm[j])
            else:
                xb = _mix_cd(xb, mix_norm[layer], cd_w_in, cd_w_out, j, lru_conv_w[j], lru_conv_b[j], lru_wa[j],
                             lru_ba[j], lru_wx[j], lru_bx[j], lru_lambda[j], rwkv_mu[j], rwkv_w0[j],
                             rwkv_w2[j], rwkv_a0[j], rwkv_a2[j], rwkv_g2[j], rwkv_kk[j], rwkv_ka[j],
                             rwkv_rk[j].reshape(-1), rwkv_ln_g[j], rwkv_ln_b[j])
            xb = _ffn(xb, ffn2_norm[layer], ffn2_wg, ffn2_wu, ffn2_wd, layer, final_norm,
                      final_norm=(layer == depth - 1))
        outs.append(xb)
    return outs[0].reshape(1, t_len, d) if bsz == 1 else jnp.stack(outs, axis=0)
```

```python
import functools
import math

import jax
import jax.numpy as jnp
import numpy as np
from jax import lax
from jax.experimental import pallas as pl
from jax.experimental.pallas import tpu as pltpu

F32 = jnp.float32
BF16 = jnp.bfloat16

D_FF = 2816
NORM_EPS = 1e-6
MIX_W = 512

RET_HEADS = 4
RET_DK = 128
RET_CHUNK = 128
ROPE_BASE = 10000.0
HEAD_NORM_EPS = 1e-6

SSM_HEADS = 8
SSM_HEAD_DIM = 64
SSM_GROUPS = 2
SSM_STATE = 64
SSM_CONV = 4
SSM_CHUNK = 128
SSM_XBC = 768
SSM_NORM_EPS = 1e-5

LRU_BLOCKS = 4
LRU_BLOCK = 128
LRU_CONV = 4
LRU_C = 8.0

RWKV_HEAD_DIM = 64
DECAY_LORA = 64
ICL_LORA = 64
GATE_LORA = 160
RWKV_LN_EPS = 64e-5
RWKV_CHUNK = 64

LANES = 128
SUBLANES = 8
VMEM_LIMIT = 60 * 1024 * 1024

FFN_TM = 1024
FFN_DMA_DEPTH = 3
FFN_TF = 256
MXU_DIM = 256
FFN_TK = (0, 6 * MXU_DIM, D_FF)
MIX_TM = 256
PROJ_TN = 256

AB_Q, AB_K, AB_V, AB_G, AB_Z, AB_XBC, AB_DT = 0, 512, 1024, 1536, 2048, 2560, 3328
CD_XB, CD_GB, CD_PR, CD_LORA = 0, 512, 1024, 2560
CD_PR_COLS = 2048


def _bdot(a, b):
    return jnp.dot(a.astype(BF16), b.astype(BF16), preferred_element_type=F32)


def _bdot_nt(a, b):
    return lax.dot_general(a.astype(BF16), b.astype(BF16), (((1,), (1,)), ((), ())),
                           preferred_element_type=F32)


def _bdot_tn(a, b):
    return lax.dot_general(a.astype(BF16), b.astype(BF16), (((0,), (0,)), ((), ())),
                           preferred_element_type=F32)


def _split(x, pieces):
    out = []
    for _ in range(pieces - 1):
        p = x.astype(BF16)
        out.append(p)
        x = x - p.astype(F32)
    out.append(x.astype(BF16))
    return out


def _sel_dot_lhs(x, sel_stack, pieces):
    return jnp.dot(jnp.concatenate(_split(x, pieces), axis=1), sel_stack, preferred_element_type=F32)


def _sel_dot_rhs(sel, x, pieces):
    n = x.shape[1]
    res = jnp.dot(sel, jnp.concatenate(_split(x, pieces), axis=1), preferred_element_type=F32)
    out = res[:, 0:n]
    for p in range(1, pieces):
        out = out + res[:, p * n:(p + 1) * n]
    return out


def _rms_norm(x, g, eps):
    return x * lax.rsqrt(jnp.mean(x * x, axis=-1, keepdims=True) + eps) * g


def _silu(x):
    return x * jax.nn.sigmoid(x)


def _softplus(x):
    return jnp.maximum(x, 0.0) + jnp.log1p(jnp.exp(-jnp.abs(x)))


def _sqrt_nonneg(y):
    return y * lax.rsqrt(jnp.maximum(y, jnp.finfo(jnp.float32).tiny))


def _gelu_tanh(x):
    c = math.sqrt(2.0 / math.pi)
    return 0.5 * x * (1.0 + jnp.tanh(c * (x + 0.044715 * (x * x * x))))


def _causal_conv(tail_ref, x, w_ref, b_ref, tm, width):
    tail = tail_ref[...]
    row = lax.broadcasted_iota(jnp.int32, (SUBLANES, 1), 0)
    acc = b_ref[...] + w_ref[width - 1:width, :] * x
    for s in range(1, width):
        delayed = pltpu.roll(x, s, axis=0)
        head = jnp.where(row < s, pltpu.roll(tail, s, axis=0), delayed[0:SUBLANES])
        delayed = jnp.concatenate([head, delayed[SUBLANES:]], axis=0)
        acc = acc + w_ref[width - 1 - s:width - s, :] * delayed
    tail_ref[...] = x[tm - SUBLANES:tm]
    return acc


def _ffn_body(x_ref, g_ref, wg_hbm, wu_hbm, wd_hbm, fin_ref, o_ref, act_sc, wg_sc, wu_sc, wd_sc,
              cstage_sc, rstage_sc, csem, rsem, *, final_norm, layer):
    n_sec = D_FF // FFN_TF
    depth = cstage_sc.shape[0]
    col_items = [(hbm, sc, j) for j in range(n_sec) for hbm, sc in ((wg_hbm, wg_sc), (wu_hbm, wu_sc))]

    def column_copy(t):
        hbm, _, j = col_items[t]
        return pltpu.make_async_copy(hbm.at[layer, :, j * FFN_TF:(j + 1) * FFN_TF],
                                     cstage_sc.at[t % depth], csem.at[t % depth])

    def row_copy(t):
        return pltpu.make_async_copy(wd_hbm.at[layer, t * FFN_TF:(t + 1) * FFN_TF, :],
                                     rstage_sc.at[t % depth], rsem.at[t % depth])

    def take_column(t):
        _, sc, j = col_items[t]
        column_copy(t).wait()
        sc[:, j * FFN_TF:(j + 1) * FFN_TF] = cstage_sc[t % depth].astype(BF16)
        if t + depth < len(col_items):
            column_copy(t + depth).start()

    def take_row(t):
        row_copy(t).wait()
        wd_sc[t * FFN_TF:(t + 1) * FFN_TF, :] = rstage_sc[t % depth].astype(BF16)
        if t + depth < n_sec:
            row_copy(t + depth).start()

    def step(first):
        if first:
            for t in range(depth):
                column_copy(t).start()
            for t in range(depth):
                row_copy(t).start()
        x = x_ref[...]
        h = _rms_norm(x, g_ref[...], NORM_EPS).astype(BF16)
        for j in range(n_sec):
            cols = slice(j * FFN_TF, (j + 1) * FFN_TF)
            if first:
                take_column(2 * j)
                take_column(2 * j + 1)
            gate = jnp.dot(h, wg_sc[:, cols], preferred_element_type=F32)
            up = jnp.dot(h, wu_sc[:, cols], preferred_element_type=F32)
            act_sc[:, cols] = (_silu(gate) * up).astype(BF16)
        down = None
        for lo, hi in zip(FFN_TK[:-1], FFN_TK[1:]):
            if first:
                for t in range(lo // FFN_TF, hi // FFN_TF):
                    take_row(t)
            part = jnp.dot(act_sc[:, lo:hi], wd_sc[lo:hi, :], preferred_element_type=F32)
            down = part if down is None else down + part
        y = x + 0.5 * down
        if final_norm:
            y = _rms_norm(y, fin_ref[...], NORM_EPS)
        o_ref[...] = y

    pl.when(pl.program_id(0) == 0)(functools.partial(step, True))
    pl.when(pl.program_id(0) > 0)(functools.partial(step, False))


def _resident_spec(shape):
    nd = len(shape)
    return pl.BlockSpec(shape, lambda i, _nd=nd: (0,) * _nd, pipeline_mode=pl.Buffered(1))


def _next_tile_spec(tm, d, n_tiles):
    return pl.BlockSpec((tm, d), lambda i: (jnp.minimum(2 * i + 2, n_tiles - 1), 0))


def _layer_spec(shape, layer):
    nd = len(shape)
    return pl.BlockSpec((None,) + tuple(shape), lambda i, _nd=nd: (layer,) + (0,) * _nd,
                        pipeline_mode=pl.Buffered(1))


def _layer_operand(stack, layer):
    if stack.shape[0] == 1:
        return stack.reshape(stack.shape[1:]), _resident_spec(stack.shape[1:])
    return stack, _layer_spec(stack.shape[1:], layer)


def _ffn(x, norm_g, wg, wu, wd, layer, fin_g, *, final_norm):
    t_len, d = x.shape
    tm = min(FFN_TM, t_len)
    hbm = pl.BlockSpec(memory_space=pl.ANY)
    return pl.pallas_call(
        functools.partial(_ffn_body, final_norm=final_norm, layer=layer),
        grid=(t_len // tm,),
        in_specs=[
            pl.BlockSpec((tm, d), lambda i: (i, 0)),
            _resident_spec((1, d)),
            hbm, hbm, hbm,
            _resident_spec((1, d)),
        ],
        out_specs=pl.BlockSpec((tm, d), lambda i: (i, 0)),
        out_shape=jax.ShapeDtypeStruct((t_len, d), F32),
        scratch_shapes=[
            pltpu.VMEM((tm, D_FF), BF16),
            pltpu.VMEM((d, D_FF), BF16), pltpu.VMEM((d, D_FF), BF16), pltpu.VMEM((D_FF, d), BF16),
            pltpu.VMEM((FFN_DMA_DEPTH, d, FFN_TF), F32),
            pltpu.VMEM((FFN_DMA_DEPTH, FFN_TF, d), F32),
            pltpu.SemaphoreType.DMA((FFN_DMA_DEPTH,)),
            pltpu.SemaphoreType.DMA((FFN_DMA_DEPTH,)),
        ],
        compiler_params=pltpu.CompilerParams(
            dimension_semantics=("arbitrary",), vmem_limit_bytes=VMEM_LIMIT),
        name="ffn_final" if final_norm else "ffn",
    )(x, norm_g.reshape(1, d), wg, wu, wd, fin_g.reshape(1, d))


def _cast_weight(dst_sc, src_ref):
    for lo in range(0, dst_sc.shape[1], MXU_DIM):
        hi = min(lo + MXU_DIM, dst_sc.shape[1])
        dst_sc[:, lo:hi] = src_ref[:, lo:hi].astype(BF16)


def _cast_weight_rows(dst_sc, src_ref):
    for lo in range(0, dst_sc.shape[0], MXU_DIM):
        hi = min(lo + MXU_DIM, dst_sc.shape[0])
        dst_sc[lo:hi, :] = src_ref[lo:hi, :].astype(BF16)


def _projection_steps(x, p_ref, g_ref, wt_sc, tail_sc):
    h = _rms_norm(x, g_ref[...], NORM_EPS).astype(BF16)
    main = wt_sc.shape[0]
    nt_dot = lambda w: lax.dot_general(h, w, (((1,), (1,)), ((), ())), preferred_element_type=F32)

    def section(lo, hi):
        p_ref[:, lo:hi] = nt_dot(wt_sc[lo:hi, :])

    def tail():
        p_ref[:, main:main + tail_sc.shape[0]] = nt_dot(tail_sc[...])

    steps = [functools.partial(section, lo, min(lo + PROJ_TN, main)) for lo in range(0, main, PROJ_TN)]
    return steps + [tail]


def _run_next(steps):
    if steps:
        steps.pop(0)()


def _run_all(steps):
    while steps:
        _run_next(steps)


def _ab_mix(p_ref, x, rota, rotb_ref, rdec_ref, rqfs_ref, rkte_ref, rcd_ref,
            cw_ref, cb_ref, dtb_ref, alog_ref, dexp_ref, sng_ref, exp_ref, tri_ref,
            sret_sc, sssm_sc, ext_sc, woutb_sc, tm, side):
    def proj(lo, width):
        return p_ref[:, lo:lo + width]

    cos_a, sin_a = rota[0:1, :], rota[1:2, :]
    sin_a_sgn, cos_a_sgn = rota[2:3, :], rota[3:4, :]
    cos_b, sin_b = rotb_ref[:, 0:RET_DK], rotb_ref[:, RET_DK:2 * RET_DK]
    cos2 = cos_a * cos_b - sin_a * sin_b
    sin2 = sin_a_sgn * cos_b + cos_a_sgn * sin_b

    q_all = proj(AB_Q, MIX_W)
    k_all = proj(AB_K, MIX_W)
    v_all = proj(AB_V, MIX_W)
    gate_ret = _silu(proj(AB_G, MIX_W))
    scale = RET_DK ** -0.5
    c_len = RET_CHUNK
    n_chunks = tm // c_len
    cells = [(c, hh) for c in range(n_chunks) for hh in range(RET_HEADS)]
    rows_of = lambda c: slice(c * c_len, (c + 1) * c_len)
    lanes_of = lambda hh: slice(hh * RET_DK, (hh + 1) * RET_DK)

    def rotate(t, c, hh):
        t = t[rows_of(c), lanes_of(hh)]
        return t * cos2[rows_of(c)] + pltpu.roll(t, RET_DK // 2, axis=1) * sin2[rows_of(c)]

    qh = {u: rotate(q_all, *u) for u in cells}
    kh = {u: rotate(k_all, *u) * scale for u in cells}
    vh = {(c, hh): v_all[rows_of(c), lanes_of(hh)].astype(BF16) for c, hh in cells}
    scores = {(c, hh): _bdot_nt(qh[(c, hh)], kh[(c, hh)]) * rdec_ref[hh] for c, hh in cells}
    kv = {(c, hh): _bdot_tn(kh[(c, hh)] * rkte_ref[:, lanes_of(hh)], vh[(c, hh)]) for c, hh in cells}
    state = {}
    for hh in range(RET_HEADS):
        s_cur = sret_sc[hh]
        for c in range(n_chunks):
            state[(c, hh)] = s_cur
            s_cur = s_cur * rcd_ref[:, lanes_of(hh)] + kv[(c, hh)]
        sret_sc[hh] = s_cur
    outs = {(c, hh): _bdot(jnp.concatenate([scores[(c, hh)], qh[(c, hh)] * rqfs_ref[:, lanes_of(hh)]], axis=1),
                           jnp.concatenate([vh[(c, hh)], state[(c, hh)].astype(BF16)], axis=0))
            for c, hh in cells}
    outs = {u: o * lax.rsqrt(jnp.mean(o * o, axis=-1, keepdims=True) + HEAD_NORM_EPS) for u, o in outs.items()}
    y_ret = gate_ret * jnp.concatenate(
        [jnp.concatenate([outs[(c, hh)] for hh in range(RET_HEADS)], axis=1) for c in range(n_chunks)], axis=0)

    z = proj(AB_Z, MIX_W)
    xbc = _silu(_causal_conv(ext_sc, proj(AB_XBC, SSM_XBC), cw_ref, cb_ref, tm, SSM_CONV))
    xs = xbc[:, :MIX_W]
    bm = xbc[:, MIX_W:MIX_W + LANES]
    cm = xbc[:, MIX_W + LANES:MIX_W + 2 * LANES]
    lane = lax.broadcasted_iota(jnp.int32, (1, LANES), 1)
    dt = _softplus(proj(AB_DT, LANES) + dtb_ref[...])
    a_neg = jnp.where(lane < SSM_HEADS, -jnp.exp(alog_ref[...]), 0.0)
    adt = dt * a_neg
    xdt = xs * _sel_dot_lhs(dt, exp_ref[0:2 * LANES, :], 2)
    acs_all = _sel_dot_rhs(tri_ref[...], adt, 3)
    acs_e_all = _sel_dot_lhs(acs_all, exp_ref[...], 3)
    l_len = SSM_CHUNK
    row = lax.broadcasted_iota(jnp.int32, (l_len, l_len), 0)
    col = lax.broadcasted_iota(jnp.int32, (l_len, l_len), 1)
    causal = row >= col
    first_half = lax.broadcasted_iota(jnp.int32, (1, LANES), 1) < SSM_HEAD_DIM
    grp_mask = [first_half, jnp.logical_not(first_half)]
    srow = lax.broadcasted_iota(jnp.int32, (LANES, MIX_W), 0)
    scol = lax.broadcasted_iota(jnp.int32, (LANES, MIX_W), 1)
    state_mask = (srow // SSM_STATE) == (scol // (MIX_W // SSM_GROUPS))
    s_cur = sssm_sc[...]
    s_enter = []
    for c in range(tm // l_len):
        rs = slice(c * l_len, (c + 1) * l_len)
        last_e = acs_e_all[(c + 1) * l_len - 1:(c + 1) * l_len, :]
        upd = _bdot_tn(bm[rs], xdt[rs] * jnp.exp(last_e - acs_e_all[rs]))
        s_enter.append(s_cur)
        s_cur = s_cur * jnp.exp(last_e) + jnp.where(state_mask, upd, 0.0)
    sssm_sc[...] = s_cur

    ssm_rows = []
    for c in range(tm // l_len):
        rs = slice(c * l_len, (c + 1) * l_len)
        acs = acs_all[rs]
        acs_e = acs_e_all[rs]
        acs_t = acs.T
        bm_c = bm[rs]
        cm_c = cm[rs]
        y_off = _bdot(cm_c, s_enter[c]) * jnp.exp(acs_e)
        cb = [jnp.where(causal, _bdot_nt(jnp.where(grp_mask[g], cm_c, 0.0), bm_c), 0.0)
              for g in range(SSM_GROUPS)]
        tiles = []
        for j in range(MIX_W // LANES):
            xt = xdt[rs, j * LANES:(j + 1) * LANES]
            weights = []
            for hh in (2 * j, 2 * j + 1):
                seg = acs[:, hh:hh + 1] - acs_t[hh:hh + 1, :]
                weights.append((cb[hh // (SSM_HEADS // SSM_GROUPS)] * jnp.exp(jnp.minimum(seg, 0.0))).astype(BF16))
                _run_next(side)
            tiles.append(jnp.dot(
                jnp.concatenate(weights, axis=1),
                jnp.concatenate([jnp.where(first_half, xt, 0.0), jnp.where(first_half, 0.0, xt)], axis=0).astype(BF16),
                preferred_element_type=F32))
        y_diag = jnp.concatenate(tiles, axis=1)
        y = y_diag + y_off + dexp_ref[...] * xs[rs]
        y = y * _silu(z[rs])
        halves = []
        for g in range(SSM_GROUPS):
            yg = y[:, g * (MIX_W // SSM_GROUPS):(g + 1) * (MIX_W // SSM_GROUPS)]
            halves.append(yg * lax.rsqrt(jnp.mean(yg * yg, axis=-1, keepdims=True) + SSM_NORM_EPS))
        ssm_rows.append(jnp.concatenate(halves, axis=1) * sng_ref[...])
    y_ssm = jnp.concatenate(ssm_rows, axis=0)

    _run_all(side)
    return x + _bdot(jnp.concatenate([y_ret, y_ssm], axis=1), woutb_sc[...])


def _ab_body(x_ref, xn_ref, rota_ref, rotb_ref, g_ref, win_ref, rdec_ref, rqfs_ref, rkte_ref, rcd_ref,
             cw_ref, cb_ref, dtb_ref, alog_ref, dexp_ref, sng_ref, exp_ref, tri_ref, wout_ref,
             o_ref, sret_sc, sssm_sc, ext_sc, winb_sc, wdt_sc, woutb_sc, pa_sc, pb_sc, *, tm):
    steps = functools.partial(_projection_steps, g_ref=g_ref, wt_sc=winb_sc, tail_sc=wdt_sc)

    @pl.when(pl.program_id(0) == 0)
    def _():
        sret_sc[...] = jnp.zeros_like(sret_sc)
        sssm_sc[...] = jnp.zeros_like(sssm_sc)
        ext_sc[...] = jnp.zeros_like(ext_sc)
        _cast_weight_rows(winb_sc, win_ref)
        _cast_weight(woutb_sc, wout_ref)
        wdt_sc[...] = jnp.zeros_like(wdt_sc)
        wdt_sc[0:SSM_HEADS, :] = win_ref[AB_DT:AB_DT + SSM_HEADS, :].astype(BF16)
        _run_all(steps(x_ref[0:tm, :], pa_sc))

    mix = functools.partial(
        _ab_mix, rotb_ref=rotb_ref, rdec_ref=rdec_ref, rqfs_ref=rqfs_ref, rkte_ref=rkte_ref, rcd_ref=rcd_ref,
        cw_ref=cw_ref, cb_ref=cb_ref, dtb_ref=dtb_ref, alog_ref=alog_ref, dexp_ref=dexp_ref, sng_ref=sng_ref,
        exp_ref=exp_ref, tri_ref=tri_ref, sret_sc=sret_sc, sssm_sc=sssm_sc, ext_sc=ext_sc, woutb_sc=woutb_sc,
        tm=tm)
    o_ref[0:tm, :] = mix(pa_sc, x_ref[0:tm, :], rota_ref[0], side=steps(x_ref[tm:2 * tm, :], pb_sc))
    o_ref[tm:2 * tm, :] = mix(pb_sc, x_ref[tm:2 * tm, :], rota_ref[1], side=steps(xn_ref[...], pa_sc))


def _rotary_tables(n_tiles, tm):
    inv_freq = ROPE_BASE ** (-np.arange(0, RET_DK, 2, dtype=np.float64) / RET_DK)
    two = lambda t: np.concatenate([t, t], axis=-1)
    sign = np.concatenate([-np.ones(RET_DK // 2), np.ones(RET_DK // 2)])
    ang_a = (np.arange(n_tiles, dtype=np.float64) * tm)[:, None] * inv_freq[None, :]
    ang_b = np.arange(tm, dtype=np.float64)[:, None] * inv_freq[None, :]
    cos_a, sin_a = two(np.cos(ang_a)), two(np.sin(ang_a))
    rot_a = np.zeros((n_tiles, SUBLANES, RET_DK))
    rot_a[:, 0], rot_a[:, 1], rot_a[:, 2], rot_a[:, 3] = cos_a, sin_a, sign * sin_a, sign * cos_a
    rot_b = np.concatenate([two(np.cos(ang_b)), two(np.sin(ang_b))], axis=1)
    return jnp.asarray(rot_a, F32), jnp.asarray(rot_b, F32)


def _retention_tables():
    c = RET_CHUNK
    log_g = np.log1p(-(2.0 ** (-5.0 - np.arange(RET_HEADS, dtype=np.float64))))
    pos = np.arange(c, dtype=np.float64)
    rel = pos[:, None] - pos[None, :]
    rdec = np.where(rel >= 0, np.exp(np.maximum(rel, 0.0)[None] * log_g[:, None, None]), 0.0)
    per_head = lambda t: np.repeat(t, RET_DK, axis=1)
    rkte = per_head(np.exp((c - 1 - pos)[:, None] * log_g[None, :]))
    rqfs = per_head(np.exp((pos + 1.0)[:, None] * log_g[None, :]))
    rcd = per_head(np.exp(c * log_g)[None, :])
    return tuple(jnp.asarray(t, F32) for t in (rdec, rqfs, rkte, rcd))


def _mix_ab(x, norm_g, w_in_all, w_out_all, j, conv_w, conv_b, dt_bias, a_log, d_skip, norm_ssm):
    t_len, d = x.shape
    tm = min(MIX_TM, t_len)
    rot_a, rot_b = _rotary_tables(t_len // tm, tm)
    rdec, rqfs, rkte, rcd = _retention_tables()

    pad_row = lambda v: jnp.pad(v.astype(F32), (0, LANES - v.shape[0])).reshape(1, LANES)
    expand = (jnp.arange(LANES)[:, None] == (jnp.arange(MIX_W)[None, :] // SSM_HEAD_DIM)).astype(BF16)
    expand = jnp.concatenate([expand] * 3, axis=0)
    t_idx = jnp.arange(tm)
    tri = (((t_idx[:, None] // SSM_CHUNK) == (t_idx[None, :] // SSM_CHUNK)) &
           (t_idx[:, None] >= t_idx[None, :])).astype(BF16)

    operands = [
        (x, pl.BlockSpec((2 * tm, d), lambda i: (i, 0))),
        (x, _next_tile_spec(tm, d, t_len // tm)),
        (rot_a, pl.BlockSpec((2, SUBLANES, RET_DK), lambda i: (i, 0, 0))),
        (rot_b, None),
        (norm_g.reshape(1, d), None),
        _layer_operand(jnp.swapaxes(w_in_all, 1, 2), j),
        (rdec, None), (rqfs, None), (rkte, None), (rcd, None),
        (conv_w, None), (conv_b.reshape(1, SSM_XBC), None),
        (pad_row(dt_bias), None), (pad_row(a_log), None),
        (jnp.repeat(d_skip.astype(F32), SSM_HEAD_DIM).reshape(1, MIX_W), None),
        (norm_ssm.reshape(1, MIX_W), None),
        (expand, None), (tri, None),
        _layer_operand(w_out_all, j),
    ]
    args = [a for a, _ in operands]
    specs = [s if s is not None else _resident_spec(a.shape) for a, s in operands]
    return pl.pallas_call(
        functools.partial(_ab_body, tm=tm),
        grid=(t_len // (2 * tm),),
        in_specs=specs,
        out_specs=pl.BlockSpec((2 * tm, d), lambda i: (i, 0)),
        out_shape=jax.ShapeDtypeStruct((t_len, d), F32),
        scratch_shapes=[
            pltpu.VMEM((RET_HEADS, RET_DK, RET_DK), F32),
            pltpu.VMEM((LANES, MIX_W), F32),
            pltpu.VMEM((SUBLANES, SSM_XBC), F32),
            pltpu.VMEM((AB_DT, d), BF16),
            pltpu.VMEM((LANES, d), BF16),
            pltpu.VMEM((2 * MIX_W, d), BF16),
            pltpu.VMEM((tm, AB_DT + LANES), F32),
            pltpu.VMEM((tm, AB_DT + LANES), F32),
        ],
        compiler_params=pltpu.CompilerParams(
            dimension_semantics=("arbitrary",), vmem_limit_bytes=VMEM_LIMIT),
        name="mix_ret_ssd",
    )(*args)


def _head_sum(x, ones_blk):
    tiles = [_sel_dot_lhs(x[:, j * LANES:(j + 1) * LANES], ones_blk, 2) for j in range(x.shape[1] // LANES)]
    return jnp.concatenate(tiles, axis=1)


def _cd_mix(p_ref, x, lcw_ref, lcb_ref, wa_ref, ba_ref, wx_ref, bx_ref, lam_ref,
            mu_ref, w0_ref, w2_ref, a0_ref, a2_ref, g2_ref, kk_ref, ka_ref, rk_ref, lng_ref, lnb_ref,
            ones_ref, sel_ref, ext_sc, hcar_sc, pcar_sc, s_sc, woutb_sc, tm, side):
    def proj(lo, width):
        return p_ref[:, lo:lo + width]

    rows = lax.broadcasted_iota(jnp.int32, (tm, 1), 0)

    xc = _causal_conv(ext_sc, proj(CD_XB, MIX_W), lcw_ref, lcb_ref, tm, LRU_CONV)
    _run_next(side)
    r_parts, i_parts = [], []
    for n in range(LRU_BLOCKS):
        ls = slice(n * LRU_BLOCK, (n + 1) * LRU_BLOCK)
        r_parts.append(jax.nn.sigmoid(_bdot(xc[:, ls], wa_ref[n]) + ba_ref[:, ls]))
        i_parts.append(jax.nn.sigmoid(_bdot(xc[:, ls], wx_ref[n]) + bx_ref[:, ls]))
    r_gate = jnp.concatenate(r_parts, axis=1)
    i_gate = jnp.concatenate(i_parts, axis=1)
    log_a = -LRU_C * r_gate * _softplus(-lam_ref[...])
    a_cum = jnp.exp(log_a)
    hs = _sqrt_nonneg(-jnp.tanh(log_a) * (1.0 + a_cum * a_cum)) * (i_gate * xc)
    _run_next(side)
    in_group = rows % SUBLANES
    shift = 1
    while shift < SUBLANES:
        valid = in_group >= shift
        a_prev = jnp.where(valid, pltpu.roll(a_cum, shift, axis=0), 1.0)
        h_prev = jnp.where(valid, pltpu.roll(hs, shift, axis=0), 0.0)
        hs = hs + a_cum * h_prev
        a_cum = a_cum * a_prev
        shift *= 2
    carry = hcar_sc[...]
    groups = []
    for gi in range(tm // SUBLANES):
        grp = slice(gi * SUBLANES, (gi + 1) * SUBLANES)
        hg = hs[grp] + a_cum[grp] * carry
        groups.append(hg)
        carry = hg[SUBLANES - 1:SUBLANES, :]
    hs = jnp.concatenate(groups, axis=0)
    hcar_sc[...] = carry
    _run_next(side)
    y_lru = hs * _gelu_tanh(proj(CD_GB, MIX_W))

    pr = proj(CD_PR, CD_PR_COLS)
    prev = pltpu.roll(pr, 1, axis=0)
    row8 = lax.broadcasted_iota(jnp.int32, (SUBLANES, 1), 0)
    prev = jnp.concatenate([jnp.where(row8 == 0, pcar_sc[...], prev[0:SUBLANES]), prev[SUBLANES:]], axis=0)
    pcar_sc[...] = pr[tm - 1:tm, :]
    ps = pr + (prev - pr) * mu_ref[...]
    _run_next(side)
    r = ps[:, 0:MIX_W]
    k = ps[:, MIX_W:2 * MIX_W]
    v = ps[:, 2 * MIX_W:3 * MIX_W]
    wl = ps[:, 3 * MIX_W:3 * MIX_W + LANES]
    al = ps[:, 3 * MIX_W + LANES:3 * MIX_W + 2 * LANES]
    gl = ps[:, 3 * MIX_W + 2 * LANES:3 * MIX_W + 4 * LANES]
    log_w = -math.exp(-0.5) * jax.nn.sigmoid(w0_ref[...] + _bdot(jnp.tanh(wl), w2_ref[...]))
    a = jax.nn.sigmoid(a0_ref[...] + _bdot(al, a2_ref[...]))
    g = _bdot(jax.nn.sigmoid(gl), g2_ref[...])
    _run_next(side)
    ones_blk = ones_ref[...]
    kk = k * kk_ref[...]
    kk = kk * lax.rsqrt(jnp.maximum(_head_sum(kk * kk, ones_blk), 1e-24))
    k = k * (1.0 + (a - 1.0) * ka_ref[...])
    c_len = RWKV_CHUNK
    pair = 2 * c_len
    n_chunks = tm // c_len
    n_pairs = MIX_W // LANES

    cs = _sel_dot_rhs(sel_ref[...], log_w, 3)
    last = [cs[(c + 1) * c_len - 1:(c + 1) * c_len, :] for c in range(n_chunks)]
    tot = jnp.concatenate([jnp.broadcast_to(t, (c_len, MIX_W)) for t in last], axis=0)
    e_neg = jnp.exp(-cs)
    e_end = jnp.exp(tot - cs)
    nb = -(kk * a)
    bt = kk * jnp.exp(cs - log_w)
    rt = r * jnp.exp(cs)
    ab = nb * e_neg
    kb = k * e_neg
    ae = nb * e_end
    ke = k * e_end

    lane = lax.broadcasted_iota(jnp.int32, (1, LANES), 1)
    m0 = lane < RWKV_HEAD_DIM
    trow = lax.broadcasted_iota(jnp.int32, (c_len, LANES), 0)
    scol = lax.broadcasted_iota(jnp.int32, (c_len, LANES), 1) % c_len
    strict = trow > scol
    incl = trow >= scol
    eye = (trow == scol).astype(F32)

    def merge_mask(s):
        return ((trow // s) == (scol // s) + 1) & ((trow // s) % 2 == 1)

    def blockdiag(t):
        return jnp.concatenate([jnp.where(m0, t, 0.0), jnp.where(m0, 0.0, t)], axis=0)

    def tile(t, c, j):
        return t[c * c_len:(c + 1) * c_len, j * LANES:(j + 1) * LANES]

    units = [(c, j) for c in range(n_chunks) for j in range(n_pairs)]
    vbd = {u: blockdiag(tile(v, *u)).astype(BF16) for u in units}
    n_mat, lhs_x, lhs_y, lhs_z, w_col = {}, {}, {}, {}, {}
    for u in units:
        bt_u = tile(bt, *u).astype(BF16)
        rt_u = tile(rt, *u).astype(BF16)
        gram = _bdot_nt(jnp.concatenate([bt_u, rt_u], axis=0),
                        jnp.concatenate([blockdiag(tile(ab, *u)), blockdiag(tile(kb, *u))], axis=0))
        n_mat[u] = jnp.where(strict, gram[0:c_len, 0:LANES], 0.0)
        lhs_x[u] = jnp.concatenate(
            [bt_u, jnp.where(strict, gram[0:c_len, LANES:2 * LANES], 0.0).astype(BF16)], axis=1)
        lhs_y[u] = jnp.concatenate(
            [rt_u, jnp.where(incl, gram[c_len:pair, 0:LANES], 0.0).astype(BF16),
             jnp.where(incl, gram[c_len:pair, LANES:2 * LANES], 0.0).astype(BF16)], axis=1)
        lhs_z[u] = jnp.concatenate(
            [blockdiag(tile(ae, *u)).T.astype(BF16), blockdiag(tile(ke, *u)).T.astype(BF16)], axis=1)
        c, j = u
        w_row = jnp.exp(last[c][:, j * LANES:(j + 1) * LANES])
        w_col[u] = jnp.broadcast_to(w_row, (pair, LANES)).T

    n_bd = {u: blockdiag(n_mat[u]).astype(BF16) for u in units}
    t_inv = {u: eye + jnp.where(merge_mask(1), n_mat[u], 0.0) for u in units}
    s = 2
    while s < c_len:
        mask = merge_mask(s)
        half = {u: jnp.dot(t_inv[u].astype(BF16), n_bd[u], preferred_element_type=F32) for u in units}
        t_inv = {u: t_inv[u] + jnp.where(mask, _bdot(half[u], blockdiag(t_inv[u])), 0.0) for u in units}
        s *= 2
    t_inv = {u: t_inv[u].astype(BF16) for u in units}

    y_rows = []
    for c in range(n_chunks):
        z_prev = [s_sc[j] for j in range(n_pairs)]
        zb = [z.astype(BF16) for z in z_prev]
        xs = [jnp.dot(lhs_x[(c, j)], jnp.concatenate([zb[j], vbd[(c, j)]], axis=0),
                      preferred_element_type=F32) for j in range(n_pairs)]
        _run_next(side)
        us = [jnp.dot(t_inv[(c, j)], blockdiag(xs[j]).astype(BF16), preferred_element_type=F32)
              for j in range(n_pairs)]
        _run_next(side)
        us = [blockdiag(u).astype(BF16) for u in us]
        tiles = []
        for j in range(n_pairs):
            tiles.append(jnp.dot(lhs_y[(c, j)], jnp.concatenate([zb[j], us[j], vbd[(c, j)]], axis=0),
                                 preferred_element_type=F32))
            s_sc[j] = z_prev[j] * w_col[(c, j)] + jnp.dot(
                lhs_z[(c, j)], jnp.concatenate([us[j], vbd[(c, j)]], axis=0), preferred_element_type=F32)
        y_rows.append(jnp.concatenate(tiles, axis=1))
    y = jnp.concatenate(y_rows, axis=0)
    inv_n = 1.0 / RWKV_HEAD_DIM
    mean = _head_sum(y, ones_blk) * inv_n
    yc = y - mean
    var = _head_sum(yc * yc, ones_blk) * inv_n
    y = yc * lax.rsqrt(var + RWKV_LN_EPS) * lng_ref[...] + lnb_ref[...]
    y = y + _head_sum(r * k * rk_ref[...], ones_blk) * v
    y_rwkv = y * g

    _run_all(side)
    return x + _bdot(jnp.concatenate([y_lru, y_rwkv], axis=1), woutb_sc[...])


def _cd_body(x_ref, xn_ref, g_ref, win_ref, lcw_ref, lcb_ref, wa_ref, ba_ref, wx_ref, bx_ref, lam_ref,
             mu_ref, w0_ref, w2_ref, a0_ref, a2_ref, g2_ref, kk_ref, ka_ref, rk_ref, lng_ref, lnb_ref,
             ones_ref, sel_ref, wout_ref, o_ref,
             ext_sc, hcar_sc, pcar_sc, s_sc, winb_sc, wlora_sc, woutb_sc, pa_sc, pb_sc, *, tm):
    steps = functools.partial(_projection_steps, g_ref=g_ref, wt_sc=winb_sc, tail_sc=wlora_sc)

    @pl.when(pl.program_id(0) == 0)
    def _():
        ext_sc[...] = jnp.zeros_like(ext_sc)
        hcar_sc[...] = jnp.zeros_like(hcar_sc)
        pcar_sc[...] = jnp.zeros_like(pcar_sc)
        s_sc[...] = jnp.zeros_like(s_sc)
        _cast_weight_rows(winb_sc, win_ref)
        _cast_weight(woutb_sc, wout_ref)
        wlora_sc[...] = jnp.zeros_like(wlora_sc)
        src = CD_LORA
        for dst, width in ((0, DECAY_LORA), (LANES, ICL_LORA), (2 * LANES, GATE_LORA)):
            wlora_sc[dst:dst + width, :] = win_ref[src:src + width, :].astype(BF16)
            src += width
        _run_all(steps(x_ref[0:tm, :], pa_sc))

    mix = functools.partial(
        _cd_mix, lcw_ref=lcw_ref, lcb_ref=lcb_ref, wa_ref=wa_ref, ba_ref=ba_ref, wx_ref=wx_ref, bx_ref=bx_ref,
        lam_ref=lam_ref, mu_ref=mu_ref, w0_ref=w0_ref, w2_ref=w2_ref, a0_ref=a0_ref, a2_ref=a2_ref, g2_ref=g2_ref,
        kk_ref=kk_ref, ka_ref=ka_ref, rk_ref=rk_ref, lng_ref=lng_ref, lnb_ref=lnb_ref, ones_ref=ones_ref,
        sel_ref=sel_ref, ext_sc=ext_sc, hcar_sc=hcar_sc, pcar_sc=pcar_sc, s_sc=s_sc, woutb_sc=woutb_sc, tm=tm)
    o_ref[0:tm, :] = mix(pa_sc, x_ref[0:tm, :], side=steps(x_ref[tm:2 * tm, :], pb_sc))
    o_ref[tm:2 * tm, :] = mix(pb_sc, x_ref[tm:2 * tm, :], side=steps(xn_ref[...], pa_sc))


def _mix_cd(x, norm_g, w_in_all, w_out_all, j, lru_conv_w, lru_conv_b, lru_wa, lru_ba, lru_wx, lru_bx,
            lru_lambda, mu, w0, w2, a0, a2, g2, k_k, k_a, r_k, ln_g, ln_b):
    t_len, d = x.shape
    tm = min(MIX_TM, t_len)

    def pad_lora(t, lo, width, padded):
        return jnp.pad(t[..., lo:lo + width], [(0, 0)] * (t.ndim - 1) + [(0, padded - width)])

    def lora_cols(t, base):
        return jnp.concatenate([
            pad_lora(t, base, DECAY_LORA, LANES),
            pad_lora(t, base + DECAY_LORA, ICL_LORA, LANES),
            pad_lora(t, base + DECAY_LORA + ICL_LORA, GATE_LORA, 2 * LANES)], axis=-1)

    mu_row = mu.reshape(1, -1)
    mu_pad = jnp.concatenate([mu_row[:, :3 * MIX_W], lora_cols(mu_row, 3 * MIX_W)], axis=-1)
    pad_rows = lambda t, n: jnp.pad(t, ((0, n - t.shape[0]), (0, 0))).astype(BF16)
    row = lambda t: t.reshape(1, -1).astype(F32)
    ones_blk = ((jnp.arange(LANES)[:, None] // RWKV_HEAD_DIM) ==
                (jnp.arange(LANES)[None, :] // RWKV_HEAD_DIM)).astype(BF16)
    ones_blk = jnp.concatenate([ones_blk] * 2, axis=0)
    t_idx = jnp.arange(tm)
    same_chunk = (t_idx[:, None] // RWKV_CHUNK) == (t_idx[None, :] // RWKV_CHUNK)
    sel = (same_chunk & (t_idx[:, None] >= t_idx[None, :])).astype(BF16)

    operands = [
        (x, pl.BlockSpec((2 * tm, d), lambda i: (i, 0))),
        (x, _next_tile_spec(tm, d, t_len // tm)),
        (row(norm_g), None),
        _layer_operand(jnp.swapaxes(w_in_all, 1, 2), j),
        (lru_conv_w, None), (row(lru_conv_b), None),
        (lru_wa.astype(BF16), None), (row(lru_ba), None),
        (lru_wx.astype(BF16), None), (row(lru_bx), None),
        (row(lru_lambda), None),
        (mu_pad, None),
        (row(w0), None), (pad_rows(w2, LANES), None),
        (row(a0), None), (pad_rows(a2, LANES), None),
        (pad_rows(g2, 2 * LANES), None),
        (row(k_k), None), (row(k_a), None), (row(r_k), None), (row(ln_g), None), (row(ln_b), None),
        (ones_blk, None), (sel, None),
        _layer_operand(w_out_all, j),
    ]
    args = [a for a, _ in operands]
    specs = [s if s is not None else _resident_spec(a.shape) for a, s in operands]
    return pl.pallas_call(
        functools.partial(_cd_body, tm=tm),
        grid=(t_len // (2 * tm),),
        in_specs=specs,
        out_specs=pl.BlockSpec((2 * tm, d), lambda i: (i, 0)),
        out_shape=jax.ShapeDtypeStruct((t_len, d), F32),
        scratch_shapes=[
            pltpu.VMEM((SUBLANES, MIX_W), F32),
            pltpu.VMEM((1, MIX_W), F32),
            pltpu.VMEM((1, CD_PR_COLS), F32),
            pltpu.VMEM((MIX_W // LANES, LANES, LANES), F32),
            pltpu.VMEM((CD_LORA, d), BF16),
            pltpu.VMEM((CD_PR + CD_PR_COLS - CD_LORA, d), BF16),
            pltpu.VMEM((2 * MIX_W, d), BF16),
            pltpu.VMEM((tm, CD_PR + CD_PR_COLS), F32),
            pltpu.VMEM((tm, CD_PR + CD_PR_COLS), F32),
        ],
        compiler_params=pltpu.CompilerParams(
            dimension_semantics=("arbitrary",), vmem_limit_bytes=VMEM_LIMIT),
        name="mix_lru_rwkv",
    )(*args)


def kernel(x, ffn1_norm, ffn1_wg, ffn1_wu, ffn1_wd, mix_norm, ffn2_norm, ffn2_wg, ffn2_wu, ffn2_wd,
           ab_w_in, ab_w_out, ssm_conv_w, ssm_conv_b, ssm_dt_bias, ssm_a_log, ssm_d, ssm_norm,
           cd_w_in, cd_w_out, lru_conv_w, lru_conv_b, lru_wa, lru_ba, lru_wx, lru_bx, lru_lambda,
           rwkv_mu, rwkv_w0, rwkv_w2, rwkv_a0, rwkv_a2, rwkv_g2, rwkv_kk, rwkv_ka, rwkv_rk,
           rwkv_ln_g, rwkv_ln_b, final_norm):
    bsz, t_len, d = x.shape
    depth = ffn1_norm.shape[0]
    outs = []
    for b in range(bsz):
        xb = x.reshape(t_len, d) if bsz == 1 else x[b]
        for layer in range(depth):
            j = layer // 2
            xb = _ffn(xb, ffn1_norm[layer], ffn1_wg, ffn1_wu, ffn1_wd, layer, final_norm, final_norm=False)
            if layer % 2 == 0:
                xb = _mix_ab(xb, mix_norm[layer], ab_w_in, ab_w_out, j, ssm_conv_w[j], ssm_conv_b[j],
                             ssm_dt_bias[j], ssm_a_log[j], ssm_d[j], ssm_norm[j])
            else:
                xb = _mix_cd(xb, mix_norm[layer], cd_w_in, cd_w_out, j, lru_conv_w[j], lru_conv_b[j], lru_wa[j],
                             lru_ba[j], lru_wx[j], lru_bx[j], lru_lambda[j], rwkv_mu[j], rwkv_w0[j],
                             rwkv_w2[j], rwkv_a0[j], rwkv_a2[j], rwkv_g2[j], rwkv_kk[j], rwkv_ka[j],
                             rwkv_rk[j].reshape(-1), rwkv_ln_g[j], rwkv_ln_b[j])
            xb = _ffn(xb, ffn2_norm[layer], ffn2_wg, ffn2_wu, ffn2_wd, layer, final_norm,
                      final_norm=(layer == depth - 1))
        outs.append(xb)
    return outs[0].reshape(1, t_len, d) if bsz == 1 else jnp.stack(outs, axis=0)
```

```python
import functools
import math

import jax
import jax.numpy as jnp
import numpy as np
from jax import lax
from jax.experimental import pallas as pl
from jax.experimental.pallas import tpu as pltpu

F32 = jnp.float32
BF16 = jnp.bfloat16

D_FF = 2816
NORM_EPS = 1e-6
MIX_W = 512

RET_HEADS = 4
RET_DK = 128
RET_CHUNK = 128
ROPE_BASE = 10000.0
HEAD_NORM_EPS = 1e-6

SSM_HEADS = 8
SSM_HEAD_DIM = 64
SSM_GROUPS = 2
SSM_STATE = 64
SSM_CONV = 4
SSM_CHUNK = 128
SSM_XBC = 768
SSM_NORM_EPS = 1e-5

LRU_BLOCKS = 4
LRU_BLOCK = 128
LRU_CONV = 4
LRU_C = 8.0

RWKV_HEAD_DIM = 64
DECAY_LORA = 64
ICL_LORA = 64
GATE_LORA = 160
RWKV_LN_EPS = 64e-5
RWKV_CHUNK = 64

LANES = 128
SUBLANES = 8
VMEM_LIMIT = 56 * 1024 * 1024

FFN_TM = 512
FFN_TF = 256
MXU_DIM = 256
FFN_TK = (0, 6 * MXU_DIM, D_FF)
MIX_TM = 256
PROJ_TN = 256

AB_Q, AB_K, AB_V, AB_G, AB_Z, AB_XBC, AB_DT = 0, 512, 1024, 1536, 2048, 2560, 3328
CD_XB, CD_GB, CD_PR, CD_LORA = 0, 512, 1024, 2560
CD_PR_COLS = 2048


def _bdot(a, b):
    return jnp.dot(a.astype(BF16), b.astype(BF16), preferred_element_type=F32)


def _bdot_nt(a, b):
    return lax.dot_general(a.astype(BF16), b.astype(BF16), (((1,), (1,)), ((), ())),
                           preferred_element_type=F32)


def _bdot_tn(a, b):
    return lax.dot_general(a.astype(BF16), b.astype(BF16), (((0,), (0,)), ((), ())),
                           preferred_element_type=F32)


def _split(x, pieces):
    out = []
    for _ in range(pieces - 1):
        p = x.astype(BF16)
        out.append(p)
        x = x - p.astype(F32)
    out.append(x.astype(BF16))
    return out


def _sel_dot_lhs(x, sel_stack, pieces):
    return jnp.dot(jnp.concatenate(_split(x, pieces), axis=1), sel_stack, preferred_element_type=F32)


def _sel_dot_rhs(sel, x, pieces):
    n = x.shape[1]
    res = jnp.dot(sel, jnp.concatenate(_split(x, pieces), axis=1), preferred_element_type=F32)
    out = res[:, 0:n]
    for p in range(1, pieces):
        out = out + res[:, p * n:(p + 1) * n]
    return out


def _rms_norm(x, g, eps):
    return x * lax.rsqrt(jnp.mean(x * x, axis=-1, keepdims=True) + eps) * g


def _silu(x):
    return x * jax.nn.sigmoid(x)


def _softplus(x):
    return jnp.maximum(x, 0.0) + jnp.log1p(jnp.exp(-jnp.abs(x)))


def _sqrt_nonneg(y):
    return y * lax.rsqrt(jnp.maximum(y, jnp.finfo(jnp.float32).tiny))


def _gelu_tanh(x):
    c = math.sqrt(2.0 / math.pi)
    return 0.5 * x * (1.0 + jnp.tanh(c * (x + 0.044715 * (x * x * x))))


def _causal_conv(tail_ref, x, w_ref, b_ref, tm, width):
    tail = tail_ref[...]
    row = lax.broadcasted_iota(jnp.int32, (SUBLANES, 1), 0)
    acc = b_ref[...] + w_ref[width - 1:width, :] * x
    for s in range(1, width):
        delayed = pltpu.roll(x, s, axis=0)
        head = jnp.where(row < s, pltpu.roll(tail, s, axis=0), delayed[0:SUBLANES])
        delayed = jnp.concatenate([head, delayed[SUBLANES:]], axis=0)
        acc = acc + w_ref[width - 1 - s:width - s, :] * delayed
    tail_ref[...] = x[tm - SUBLANES:tm]
    return acc


def _ffn_body(x_ref, g_ref, wg_hbm, wu_hbm, wd_hbm, fin_ref, o_ref, act_sc, wg_sc, wu_sc, wd_sc, sem,
              *, final_norm, layer):
    n_sec = D_FF // FFN_TF

    def column_copy(k, hbm, sc, j):
        cols = slice(j * FFN_TF, (j + 1) * FFN_TF)
        return pltpu.make_async_copy(hbm.at[layer, :, cols], sc.at[:, cols], sem.at[k, j])

    def row_copy(j):
        rows = slice(j * FFN_TF, (j + 1) * FFN_TF)
        return pltpu.make_async_copy(wd_hbm.at[layer, rows, :], wd_sc.at[rows, :], sem.at[2, j])

    def step(first):
        if first:
            for j in range(n_sec):
                column_copy(0, wg_hbm, wg_sc, j).start()
                column_copy(1, wu_hbm, wu_sc, j).start()
            for j in range(n_sec):
                row_copy(j).start()
        x = x_ref[...]
        h = _rms_norm(x, g_ref[...], NORM_EPS).astype(BF16)
        for j in range(n_sec):
            cols = slice(j * FFN_TF, (j + 1) * FFN_TF)
            if first:
                column_copy(0, wg_hbm, wg_sc, j).wait()
                column_copy(1, wu_hbm, wu_sc, j).wait()
            gate = jnp.dot(h, wg_sc[:, cols].astype(BF16), preferred_element_type=F32)
            up = jnp.dot(h, wu_sc[:, cols].astype(BF16), preferred_element_type=F32)
            act_sc[:, cols] = (_silu(gate) * up).astype(BF16)
        down = None
        for lo, hi in zip(FFN_TK[:-1], FFN_TK[1:]):
            if first:
                for j in range(lo // FFN_TF, hi // FFN_TF):
                    row_copy(j).wait()
            part = jnp.dot(act_sc[:, lo:hi], wd_sc[lo:hi, :].astype(BF16), preferred_element_type=F32)
            down = part if down is None else down + part
        y = x + 0.5 * down
        if final_norm:
            y = _rms_norm(y, fin_ref[...], NORM_EPS)
        o_ref[...] = y

    pl.when(pl.program_id(0) == 0)(functools.partial(step, True))
    pl.when(pl.program_id(0) > 0)(functools.partial(step, False))


def _resident_spec(shape):
    nd = len(shape)
    return pl.BlockSpec(shape, lambda i, _nd=nd: (0,) * _nd, pipeline_mode=pl.Buffered(1))


def _next_tile_spec(tm, d, n_tiles):
    return pl.BlockSpec((tm, d), lambda i: (jnp.minimum(2 * i + 2, n_tiles - 1), 0))


def _layer_spec(shape, layer):
    nd = len(shape)
    return pl.BlockSpec((None,) + tuple(shape), lambda i, _nd=nd: (layer,) + (0,) * _nd,
                        pipeline_mode=pl.Buffered(1))


def _layer_operand(stack, layer):
    if stack.shape[0] == 1:
        return stack.reshape(stack.shape[1:]), _resident_spec(stack.shape[1:])
    return stack, _layer_spec(stack.shape[1:], layer)


def _ffn(x, norm_g, wg, wu, wd, layer, fin_g, *, final_norm):
    t_len, d = x.shape
    tm = min(FFN_TM, t_len)
    hbm = pl.BlockSpec(memory_space=pl.ANY)
    return pl.pallas_call(
        functools.partial(_ffn_body, final_norm=final_norm, layer=layer),
        grid=(t_len // tm,),
        in_specs=[
            pl.BlockSpec((tm, d), lambda i: (i, 0)),
            _resident_spec((1, d)),
            hbm, hbm, hbm,
            _resident_spec((1, d)),
        ],
        out_specs=pl.BlockSpec((tm, d), lambda i: (i, 0)),
        out_shape=jax.ShapeDtypeStruct((t_len, d), F32),
        scratch_shapes=[
            pltpu.VMEM((tm, D_FF), BF16),
            pltpu.VMEM((d, D_FF), F32), pltpu.VMEM((d, D_FF), F32), pltpu.VMEM((D_FF, d), F32),
            pltpu.SemaphoreType.DMA((3, D_FF // FFN_TF)),
        ],
        compiler_params=pltpu.CompilerParams(
            dimension_semantics=("arbitrary",), vmem_limit_bytes=VMEM_LIMIT),
        name="ffn_final" if final_norm else "ffn",
    )(x, norm_g.reshape(1, d), wg, wu, wd, fin_g.reshape(1, d))


def _cast_weight(dst_sc, src_ref):
    for lo in range(0, dst_sc.shape[1], MXU_DIM):
        hi = min(lo + MXU_DIM, dst_sc.shape[1])
        dst_sc[:, lo:hi] = src_ref[:, lo:hi].astype(BF16)


def _cast_weight_rows(dst_sc, src_ref):
    for lo in range(0, dst_sc.shape[0], MXU_DIM):
        hi = min(lo + MXU_DIM, dst_sc.shape[0])
        dst_sc[lo:hi, :] = src_ref[lo:hi, :].astype(BF16)


def _projection_steps(x, p_ref, g_ref, wt_sc, tail_sc):
    h = _rms_norm(x, g_ref[...], NORM_EPS).astype(BF16)
    main = wt_sc.shape[0]
    nt_dot = lambda w: lax.dot_general(h, w, (((1,), (1,)), ((), ())), preferred_element_type=F32)

    def section(lo, hi):
        p_ref[:, lo:hi] = nt_dot(wt_sc[lo:hi, :])

    def tail():
        p_ref[:, main:main + tail_sc.shape[0]] = nt_dot(tail_sc[...])

    steps = [functools.partial(section, lo, min(lo + PROJ_TN, main)) for lo in range(0, main, PROJ_TN)]
    return steps + [tail]


def _run_next(steps):
    if steps:
        steps.pop(0)()


def _run_all(steps):
    while steps:
        _run_next(steps)


def _ab_mix(p_ref, x, rota, rotb_ref, rdec_ref, rqfs_ref, rkte_ref, rcd_ref,
            cw_ref, cb_ref, dtb_ref, alog_ref, dexp_ref, sng_ref, exp_ref, tri_ref,
            sret_sc, sssm_sc, ext_sc, woutb_sc, tm, side):
    def proj(lo, width):
        return p_ref[:, lo:lo + width]

    cos_a, sin_a = rota[0:1, :], rota[1:2, :]
    sin_a_sgn, cos_a_sgn = rota[2:3, :], rota[3:4, :]
    cos_b, sin_b = rotb_ref[:, 0:RET_DK], rotb_ref[:, RET_DK:2 * RET_DK]
    cos2 = cos_a * cos_b - sin_a * sin_b
    sin2 = sin_a_sgn * cos_b + cos_a_sgn * sin_b

    q_all = proj(AB_Q, MIX_W)
    k_all = proj(AB_K, MIX_W)
    v_all = proj(AB_V, MIX_W)
    gate_ret = _silu(proj(AB_G, MIX_W))
    scale = RET_DK ** -0.5
    c_len = RET_CHUNK
    n_chunks = tm // c_len
    cells = [(c, hh) for c in range(n_chunks) for hh in range(RET_HEADS)]
    rows_of = lambda c: slice(c * c_len, (c + 1) * c_len)
    lanes_of = lambda hh: slice(hh * RET_DK, (hh + 1) * RET_DK)

    def rotate(t, c, hh):
        t = t[rows_of(c), lanes_of(hh)]
        return t * cos2[rows_of(c)] + pltpu.roll(t, RET_DK // 2, axis=1) * sin2[rows_of(c)]

    qh = {u: rotate(q_all, *u) for u in cells}
    kh = {u: rotate(k_all, *u) * scale for u in cells}
    vh = {(c, hh): v_all[rows_of(c), lanes_of(hh)].astype(BF16) for c, hh in cells}
    scores = {(c, hh): _bdot_nt(qh[(c, hh)], kh[(c, hh)]) * rdec_ref[hh] for c, hh in cells}
    kv = {(c, hh): _bdot_tn(kh[(c, hh)] * rkte_ref[:, lanes_of(hh)], vh[(c, hh)]) for c, hh in cells}
    state = {}
    for hh in range(RET_HEADS):
        s_cur = sret_sc[hh]
        for c in range(n_chunks):
            state[(c, hh)] = s_cur
            s_cur = s_cur * rcd_ref[:, lanes_of(hh)] + kv[(c, hh)]
        sret_sc[hh] = s_cur
    outs = {(c, hh): _bdot(jnp.concatenate([scores[(c, hh)], qh[(c, hh)] * rqfs_ref[:, lanes_of(hh)]], axis=1),
                           jnp.concatenate([vh[(c, hh)], state[(c, hh)].astype(BF16)], axis=0))
            for c, hh in cells}
    outs = {u: o * lax.rsqrt(jnp.mean(o * o, axis=-1, keepdims=True) + HEAD_NORM_EPS) for u, o in outs.items()}
    y_ret = gate_ret * jnp.concatenate(
        [jnp.concatenate([outs[(c, hh)] for hh in range(RET_HEADS)], axis=1) for c in range(n_chunks)], axis=0)

    z = proj(AB_Z, MIX_W)
    xbc = _silu(_causal_conv(ext_sc, proj(AB_XBC, SSM_XBC), cw_ref, cb_ref, tm, SSM_CONV))
    xs = xbc[:, :MIX_W]
    bm = xbc[:, MIX_W:MIX_W + LANES]
    cm = xbc[:, MIX_W + LANES:MIX_W + 2 * LANES]
    lane = lax.broadcasted_iota(jnp.int32, (1, LANES), 1)
    dt = _softplus(proj(AB_DT, LANES) + dtb_ref[...])
    a_neg = jnp.where(lane < SSM_HEADS, -jnp.exp(alog_ref[...]), 0.0)
    adt = dt * a_neg
    xdt = xs * _sel_dot_lhs(dt, exp_ref[0:2 * LANES, :], 2)
    acs_all = _sel_dot_rhs(tri_ref[...], adt, 3)
    acs_e_all = _sel_dot_lhs(acs_all, exp_ref[...], 3)
    l_len = SSM_CHUNK
    row = lax.broadcasted_iota(jnp.int32, (l_len, l_len), 0)
    col = lax.broadcasted_iota(jnp.int32, (l_len, l_len), 1)
    causal = row >= col
    first_half = lax.broadcasted_iota(jnp.int32, (1, LANES), 1) < SSM_HEAD_DIM
    grp_mask = [first_half, jnp.logical_not(first_half)]
    srow = lax.broadcasted_iota(jnp.int32, (LANES, MIX_W), 0)
    scol = lax.broadcasted_iota(jnp.int32, (LANES, MIX_W), 1)
    state_mask = (srow // SSM_STATE) == (scol // (MIX_W // SSM_GROUPS))
    s_cur = sssm_sc[...]
    s_enter = []
    for c in range(tm // l_len):
        rs = slice(c * l_len, (c + 1) * l_len)
        last_e = acs_e_all[(c + 1) * l_len - 1:(c + 1) * l_len, :]
        upd = _bdot_tn(bm[rs], xdt[rs] * jnp.exp(last_e - acs_e_all[rs]))
        s_enter.append(s_cur)
        s_cur = s_cur * jnp.exp(last_e) + jnp.where(state_mask, upd, 0.0)
    sssm_sc[...] = s_cur

    ssm_rows = []
    for c in range(tm // l_len):
        rs = slice(c * l_len, (c + 1) * l_len)
        acs = acs_all[rs]
        acs_e = acs_e_all[rs]
        acs_t = acs.T
        bm_c = bm[rs]
        cm_c = cm[rs]
        y_off = _bdot(cm_c, s_enter[c]) * jnp.exp(acs_e)
        cb = [jnp.where(causal, _bdot_nt(jnp.where(grp_mask[g], cm_c, 0.0), bm_c), 0.0)
              for g in range(SSM_GROUPS)]
        tiles = []
        for j in range(MIX_W // LANES):
            xt = xdt[rs, j * LANES:(j + 1) * LANES]
            weights = []
            for hh in (2 * j, 2 * j + 1):
                seg = acs[:, hh:hh + 1] - acs_t[hh:hh + 1, :]
                weights.append((cb[hh // (SSM_HEADS // SSM_GROUPS)] * jnp.exp(jnp.minimum(seg, 0.0))).astype(BF16))
                _run_next(side)
            tiles.append(jnp.dot(
                jnp.concatenate(weights, axis=1),
                jnp.concatenate([jnp.where(first_half, xt, 0.0), jnp.where(first_half, 0.0, xt)], axis=0).astype(BF16),
                preferred_element_type=F32))
        y_diag = jnp.concatenate(tiles, axis=1)
        y = y_diag + y_off + dexp_ref[...] * xs[rs]
        y = y * _silu(z[rs])
        halves = []
        for g in range(SSM_GROUPS):
            yg = y[:, g * (MIX_W // SSM_GROUPS):(g + 1) * (MIX_W // SSM_GROUPS)]
            halves.append(yg * lax.rsqrt(jnp.mean(yg * yg, axis=-1, keepdims=True) + SSM_NORM_EPS))
        ssm_rows.append(jnp.concatenate(halves, axis=1) * sng_ref[...])
    y_ssm = jnp.concatenate(ssm_rows, axis=0)

    _run_all(side)
    return x + _bdot(jnp.concatenate([y_ret, y_ssm], axis=1), woutb_sc[...])


def _ab_body(x_ref, xn_ref, rota_ref, rotb_ref, g_ref, win_ref, rdec_ref, rqfs_ref, rkte_ref, rcd_ref,
             cw_ref, cb_ref, dtb_ref, alog_ref, dexp_ref, sng_ref, exp_ref, tri_ref, wout_ref,
             o_ref, sret_sc, sssm_sc, ext_sc, winb_sc, wdt_sc, woutb_sc, pa_sc, pb_sc, *, tm):
    steps = functools.partial(_projection_steps, g_ref=g_ref, wt_sc=winb_sc, tail_sc=wdt_sc)

    @pl.when(pl.program_id(0) == 0)
    def _():
        sret_sc[...] = jnp.zeros_like(sret_sc)
        sssm_sc[...] = jnp.zeros_like(sssm_sc)
        ext_sc[...] = jnp.zeros_like(ext_sc)
        _cast_weight_rows(winb_sc, win_ref)
        _cast_weight(woutb_sc, wout_ref)
        wdt_sc[...] = jnp.zeros_like(wdt_sc)
        wdt_sc[0:SSM_HEADS, :] = win_ref[AB_DT:AB_DT + SSM_HEADS, :].astype(BF16)
        _run_all(steps(x_ref[0:tm, :], pa_sc))

    mix = functools.partial(
        _ab_mix, rotb_ref=rotb_ref, rdec_ref=rdec_ref, rqfs_ref=rqfs_ref, rkte_ref=rkte_ref, rcd_ref=rcd_ref,
        cw_ref=cw_ref, cb_ref=cb_ref, dtb_ref=dtb_ref, alog_ref=alog_ref, dexp_ref=dexp_ref, sng_ref=sng_ref,
        exp_ref=exp_ref, tri_ref=tri_ref, sret_sc=sret_sc, sssm_sc=sssm_sc, ext_sc=ext_sc, woutb_sc=woutb_sc,
        tm=tm)
    o_ref[0:tm, :] = mix(pa_sc, x_ref[0:tm, :], rota_ref[0], side=steps(x_ref[tm:2 * tm, :], pb_sc))
    o_ref[tm:2 * tm, :] = mix(pb_sc, x_ref[tm:2 * tm, :], rota_ref[1], side=steps(xn_ref[...], pa_sc))


def _rotary_tables(n_tiles, tm):
    inv_freq = ROPE_BASE ** (-np.arange(0, RET_DK, 2, dtype=np.float64) / RET_DK)
    two = lambda t: np.concatenate([t, t], axis=-1)
    sign = np.concatenate([-np.ones(RET_DK // 2), np.ones(RET_DK // 2)])
    ang_a = (np.arange(n_tiles, dtype=np.float64) * tm)[:, None] * inv_freq[None, :]
    ang_b = np.arange(tm, dtype=np.float64)[:, None] * inv_freq[None, :]
    cos_a, sin_a = two(np.cos(ang_a)), two(np.sin(ang_a))
    rot_a = np.zeros((n_tiles, SUBLANES, RET_DK))
    rot_a[:, 0], rot_a[:, 1], rot_a[:, 2], rot_a[:, 3] = cos_a, sin_a, sign * sin_a, sign * cos_a
    rot_b = np.concatenate([two(np.cos(ang_b)), two(np.sin(ang_b))], axis=1)
    return jnp.asarray(rot_a, F32), jnp.asarray(rot_b, F32)


def _retention_tables():
    c = RET_CHUNK
    log_g = np.log1p(-(2.0 ** (-5.0 - np.arange(RET_HEADS, dtype=np.float64))))
    pos = np.arange(c, dtype=np.float64)
    rel = pos[:, None] - pos[None, :]
    rdec = np.where(rel >= 0, np.exp(np.maximum(rel, 0.0)[None] * log_g[:, None, None]), 0.0)
    per_head = lambda t: np.repeat(t, RET_DK, axis=1)
    rkte = per_head(np.exp((c - 1 - pos)[:, None] * log_g[None, :]))
    rqfs = per_head(np.exp((pos + 1.0)[:, None] * log_g[None, :]))
    rcd = per_head(np.exp(c * log_g)[None, :])
    return tuple(jnp.asarray(t, F32) for t in (rdec, rqfs, rkte, rcd))


def _mix_ab(x, norm_g, w_in_all, w_out_all, j, conv_w, conv_b, dt_bias, a_log, d_skip, norm_ssm):
    t_len, d = x.shape
    tm = min(MIX_TM, t_len)
    rot_a, rot_b = _rotary_tables(t_len // tm, tm)
    rdec, rqfs, rkte, rcd = _retention_tables()

    pad_row = lambda v: jnp.pad(v.astype(F32), (0, LANES - v.shape[0])).reshape(1, LANES)
    expand = (jnp.arange(LANES)[:, None] == (jnp.arange(MIX_W)[None, :] // SSM_HEAD_DIM)).astype(BF16)
    expand = jnp.concatenate([expand] * 3, axis=0)
    t_idx = jnp.arange(tm)
    tri = (((t_idx[:, None] // SSM_CHUNK) == (t_idx[None, :] // SSM_CHUNK)) &
           (t_idx[:, None] >= t_idx[None, :])).astype(BF16)

    operands = [
        (x, pl.BlockSpec((2 * tm, d), lambda i: (i, 0))),
        (x, _next_tile_spec(tm, d, t_len // tm)),
        (rot_a, pl.BlockSpec((2, SUBLANES, RET_DK), lambda i: (i, 0, 0))),
        (rot_b, None),
        (norm_g.reshape(1, d), None),
        _layer_operand(jnp.swapaxes(w_in_all, 1, 2), j),
        (rdec, None), (rqfs, None), (rkte, None), (rcd, None),
        (conv_w, None), (conv_b.reshape(1, SSM_XBC), None),
        (pad_row(dt_bias), None), (pad_row(a_log), None),
        (jnp.repeat(d_skip.astype(F32), SSM_HEAD_DIM).reshape(1, MIX_W), None),
        (norm_ssm.reshape(1, MIX_W), None),
        (expand, None), (tri, None),
        _layer_operand(w_out_all, j),
    ]
    args = [a for a, _ in operands]
    specs = [s if s is not None else _resident_spec(a.shape) for a, s in operands]
    return pl.pallas_call(
        functools.partial(_ab_body, tm=tm),
        grid=(t_len // (2 * tm),),
        in_specs=specs,
        out_specs=pl.BlockSpec((2 * tm, d), lambda i: (i, 0)),
        out_shape=jax.ShapeDtypeStruct((t_len, d), F32),
        scratch_shapes=[
            pltpu.VMEM((RET_HEADS, RET_DK, RET_DK), F32),
            pltpu.VMEM((LANES, MIX_W), F32),
            pltpu.VMEM((SUBLANES, SSM_XBC), F32),
            pltpu.VMEM((AB_DT, d), BF16),
            pltpu.VMEM((LANES, d), BF16),
            pltpu.VMEM((2 * MIX_W, d), BF16),
            pltpu.VMEM((tm, AB_DT + LANES), F32),
            pltpu.VMEM((tm, AB_DT + LANES), F32),
        ],
        compiler_params=pltpu.CompilerParams(
            dimension_semantics=("arbitrary",), vmem_limit_bytes=VMEM_LIMIT),
        name="mix_ret_ssd",
    )(*args)


def _head_sum(x, ones_blk):
    tiles = [_sel_dot_lhs(x[:, j * LANES:(j + 1) * LANES], ones_blk, 2) for j in range(x.shape[1] // LANES)]
    return jnp.concatenate(tiles, axis=1)


def _cd_mix(p_ref, x, lcw_ref, lcb_ref, wa_ref, ba_ref, wx_ref, bx_ref, lam_ref,
            mu_ref, w0_ref, w2_ref, a0_ref, a2_ref, g2_ref, kk_ref, ka_ref, rk_ref, lng_ref, lnb_ref,
            ones_ref, sel_ref, ext_sc, hcar_sc, pcar_sc, s_sc, woutb_sc, tm, side):
    def proj(lo, width):
        return p_ref[:, lo:lo + width]

    rows = lax.broadcasted_iota(jnp.int32, (tm, 1), 0)

    xc = _causal_conv(ext_sc, proj(CD_XB, MIX_W), lcw_ref, lcb_ref, tm, LRU_CONV)
    _run_next(side)
    r_parts, i_parts = [], []
    for n in range(LRU_BLOCKS):
        ls = slice(n * LRU_BLOCK, (n + 1) * LRU_BLOCK)
        r_parts.append(jax.nn.sigmoid(_bdot(xc[:, ls], wa_ref[n]) + ba_ref[:, ls]))
        i_parts.append(jax.nn.sigmoid(_bdot(xc[:, ls], wx_ref[n]) + bx_ref[:, ls]))
    r_gate = jnp.concatenate(r_parts, axis=1)
    i_gate = jnp.concatenate(i_parts, axis=1)
    log_a = -LRU_C * r_gate * _softplus(-lam_ref[...])
    a_cum = jnp.exp(log_a)
    hs = _sqrt_nonneg(-jnp.tanh(log_a) * (1.0 + a_cum * a_cum)) * (i_gate * xc)
    _run_next(side)
    in_group = rows % SUBLANES
    shift = 1
    while shift < SUBLANES:
        valid = in_group >= shift
        a_prev = jnp.where(valid, pltpu.roll(a_cum, shift, axis=0), 1.0)
        h_prev = jnp.where(valid, pltpu.roll(hs, shift, axis=0), 0.0)
        hs = hs + a_cum * h_prev
        a_cum = a_cum * a_prev
        shift *= 2
    carry = hcar_sc[...]
    groups = []
    for gi in range(tm // SUBLANES):
        grp = slice(gi * SUBLANES, (gi + 1) * SUBLANES)
        hg = hs[grp] + a_cum[grp] * carry
        groups.append(hg)
        carry = hg[SUBLANES - 1:SUBLANES, :]
    hs = jnp.concatenate(groups, axis=0)
    hcar_sc[...] = carry
    _run_next(side)
    y_lru = hs * _gelu_tanh(proj(CD_GB, MIX_W))

    row8 = lax.broadcasted_iota(jnp.int32, (SUBLANES, 1), 0)

    def shifted(lo, width):
        p = proj(CD_PR + lo, width)
        prev = pltpu.roll(p, 1, axis=0)
        prev = jnp.concatenate(
            [jnp.where(row8 == 0, pcar_sc[:, lo:lo + width], prev[0:SUBLANES]), prev[SUBLANES:]], axis=0)
        pcar_sc[:, lo:lo + width] = p[tm - 1:tm, :]
        return p + (prev - p) * mu_ref[:, lo:lo + width]

    wl = shifted(3 * MIX_W, LANES)
    al = shifted(3 * MIX_W + LANES, LANES)
    gl = shifted(3 * MIX_W + 2 * LANES, 2 * LANES)
    k = shifted(MIX_W, MIX_W)
    _run_next(side)
    r = shifted(0, MIX_W)
    v = shifted(2 * MIX_W, MIX_W)
    log_w = -math.exp(-0.5) * jax.nn.sigmoid(w0_ref[...] + _bdot(jnp.tanh(wl), w2_ref[...]))
    a = jax.nn.sigmoid(a0_ref[...] + _bdot(al, a2_ref[...]))
    g = _bdot(jax.nn.sigmoid(gl), g2_ref[...])
    _run_next(side)
    ones_blk = ones_ref[...]
    kk = k * kk_ref[...]
    kk = kk * lax.rsqrt(jnp.maximum(_head_sum(kk * kk, ones_blk), 1e-24))
    k = k * (1.0 + (a - 1.0) * ka_ref[...])
    c_len = RWKV_CHUNK
    pair = 2 * c_len
    n_chunks = tm // c_len
    n_pairs = MIX_W // LANES

    cs = _sel_dot_rhs(sel_ref[...], log_w, 3)
    last = [cs[(c + 1) * c_len - 1:(c + 1) * c_len, :] for c in range(n_chunks)]
    tot = jnp.concatenate([jnp.broadcast_to(t, (c_len, MIX_W)) for t in last], axis=0)
    e_neg = jnp.exp(-cs)
    e_end = jnp.exp(tot - cs)
    nb = -(kk * a)
    bt = kk * jnp.exp(cs - log_w)
    rt = r * jnp.exp(cs)
    ab = nb * e_neg
    kb = k * e_neg
    ae = nb * e_end
    ke = k * e_end

    lane = lax.broadcasted_iota(jnp.int32, (1, LANES), 1)
    m0 = lane < RWKV_HEAD_DIM
    trow = lax.broadcasted_iota(jnp.int32, (c_len, LANES), 0)
    scol = lax.broadcasted_iota(jnp.int32, (c_len, LANES), 1) % c_len
    strict = trow > scol
    incl = trow >= scol
    eye = (trow == scol).astype(F32)

    def merge_mask(s):
        return ((trow // s) == (scol // s) + 1) & ((trow // s) % 2 == 1)

    def blockdiag(t):
        return jnp.concatenate([jnp.where(m0, t, 0.0), jnp.where(m0, 0.0, t)], axis=0)

    def tile(t, c, j):
        return t[c * c_len:(c + 1) * c_len, j * LANES:(j + 1) * LANES]

    units = [(c, j) for c in range(n_chunks) for j in range(n_pairs)]
    vbd = {u: blockdiag(tile(v, *u)).astype(BF16) for u in units}
    n_mat, lhs_x, lhs_y, lhs_z, w_col = {}, {}, {}, {}, {}
    for u in units:
        bt_u = tile(bt, *u).astype(BF16)
        rt_u = tile(rt, *u).astype(BF16)
        gram = _bdot_nt(jnp.concatenate([bt_u, rt_u], axis=0),
                        jnp.concatenate([blockdiag(tile(ab, *u)), blockdiag(tile(kb, *u))], axis=0))
        n_mat[u] = jnp.where(strict, gram[0:c_len, 0:LANES], 0.0)
        lhs_x[u] = jnp.concatenate(
            [bt_u, jnp.where(strict, gram[0:c_len, LANES:2 * LANES], 0.0).astype(BF16)], axis=1)
        lhs_y[u] = jnp.concatenate(
            [rt_u, jnp.where(incl, gram[c_len:pair, 0:LANES], 0.0).astype(BF16),
             jnp.where(incl, gram[c_len:pair, LANES:2 * LANES], 0.0).astype(BF16)], axis=1)
        lhs_z[u] = jnp.concatenate(
            [blockdiag(tile(ae, *u)).T.astype(BF16), blockdiag(tile(ke, *u)).T.astype(BF16)], axis=1)
        c, j = u
        w_row = jnp.exp(last[c][:, j * LANES:(j + 1) * LANES])
        w_col[u] = jnp.broadcast_to(w_row, (pair, LANES)).T

    n_bd = {u: blockdiag(n_mat[u]).astype(BF16) for u in units}
    t_inv = {u: eye + jnp.where(merge_mask(1), n_mat[u], 0.0) for u in units}
    s = 2
    while s < c_len:
        mask = merge_mask(s)
        half = {u: jnp.dot(t_inv[u].astype(BF16), n_bd[u], preferred_element_type=F32) for u in units}
        t_inv = {u: t_inv[u] + jnp.where(mask, _bdot(half[u], blockdiag(t_inv[u])), 0.0) for u in units}
        s *= 2
    t_inv = {u: t_inv[u].astype(BF16) for u in units}

    y_rows = []
    for c in range(n_chunks):
        z_prev = [s_sc[j] for j in range(n_pairs)]
        zb = [z.astype(BF16) for z in z_prev]
        xs = [jnp.dot(lhs_x[(c, j)], jnp.concatenate([zb[j], vbd[(c, j)]], axis=0),
                      preferred_element_type=F32) for j in range(n_pairs)]
        _run_next(side)
        us = [jnp.dot(t_inv[(c, j)], blockdiag(xs[j]).astype(BF16), preferred_element_type=F32)
              for j in range(n_pairs)]
        _run_next(side)
        us = [blockdiag(u).astype(BF16) for u in us]
        tiles = []
        for j in range(n_pairs):
            tiles.append(jnp.dot(lhs_y[(c, j)], jnp.concatenate([zb[j], us[j], vbd[(c, j)]], axis=0),
                                 preferred_element_type=F32))
            s_sc[j] = z_prev[j] * w_col[(c, j)] + jnp.dot(
                lhs_z[(c, j)], jnp.concatenate([us[j], vbd[(c, j)]], axis=0), preferred_element_type=F32)
        y_rows.append(jnp.concatenate(tiles, axis=1))
    y = jnp.concatenate(y_rows, axis=0)
    inv_n = 1.0 / RWKV_HEAD_DIM
    mean = _head_sum(y, ones_blk) * inv_n
    yc = y - mean
    var = _head_sum(yc * yc, ones_blk) * inv_n
    y = yc * lax.rsqrt(var + RWKV_LN_EPS) * lng_ref[...] + lnb_ref[...]
    y = y + _head_sum(r * k * rk_ref[...], ones_blk) * v
    y_rwkv = y * g

    _run_all(side)
    return x + _bdot(jnp.concatenate([y_lru, y_rwkv], axis=1), woutb_sc[...])


def _cd_body(x_ref, xn_ref, g_ref, win_ref, lcw_ref, lcb_ref, wa_ref, ba_ref, wx_ref, bx_ref, lam_ref,
             mu_ref, w0_ref, w2_ref, a0_ref, a2_ref, g2_ref, kk_ref, ka_ref, rk_ref, lng_ref, lnb_ref,
             ones_ref, sel_ref, wout_ref, o_ref,
             ext_sc, hcar_sc, pcar_sc, s_sc, winb_sc, wlora_sc, woutb_sc, pa_sc, pb_sc, *, tm):
    steps = functools.partial(_projection_steps, g_ref=g_ref, wt_sc=winb_sc, tail_sc=wlora_sc)

    @pl.when(pl.program_id(0) == 0)
    def _():
        ext_sc[...] = jnp.zeros_like(ext_sc)
        hcar_sc[...] = jnp.zeros_like(hcar_sc)
        pcar_sc[...] = jnp.zeros_like(pcar_sc)
        s_sc[...] = jnp.zeros_like(s_sc)
        _cast_weight_rows(winb_sc, win_ref)
        _cast_weight(woutb_sc, wout_ref)
        wlora_sc[...] = jnp.zeros_like(wlora_sc)
        src = CD_LORA
        for dst, width in ((0, DECAY_LORA), (LANES, ICL_LORA), (2 * LANES, GATE_LORA)):
            wlora_sc[dst:dst + width, :] = win_ref[src:src + width, :].astype(BF16)
            src += width
        _run_all(steps(x_ref[0:tm, :], pa_sc))

    mix = functools.partial(
        _cd_mix, lcw_ref=lcw_ref, lcb_ref=lcb_ref, wa_ref=wa_ref, ba_ref=ba_ref, wx_ref=wx_ref, bx_ref=bx_ref,
        lam_ref=lam_ref, mu_ref=mu_ref, w0_ref=w0_ref, w2_ref=w2_ref, a0_ref=a0_ref, a2_ref=a2_ref, g2_ref=g2_ref,
        kk_ref=kk_ref, ka_ref=ka_ref, rk_ref=rk_ref, lng_ref=lng_ref, lnb_ref=lnb_ref, ones_ref=ones_ref,
        sel_ref=sel_ref, ext_sc=ext_sc, hcar_sc=hcar_sc, pcar_sc=pcar_sc, s_sc=s_sc, woutb_sc=woutb_sc, tm=tm)
    o_ref[0:tm, :] = mix(pa_sc, x_ref[0:tm, :], side=steps(x_ref[tm:2 * tm, :], pb_sc))
    o_ref[tm:2 * tm, :] = mix(pb_sc, x_ref[tm:2 * tm, :], side=steps(xn_ref[...], pa_sc))


def _mix_cd(x, norm_g, w_in_all, w_out_all, j, lru_conv_w, lru_conv_b, lru_wa, lru_ba, lru_wx, lru_bx,
            lru_lambda, mu, w0, w2, a0, a2, g2, k_k, k_a, r_k, ln_g, ln_b):
    t_len, d = x.shape
    tm = min(MIX_TM, t_len)

    def pad_lora(t, lo, width, padded):
        return jnp.pad(t[..., lo:lo + width], [(0, 0)] * (t.ndim - 1) + [(0, padded - width)])

    def lora_cols(t, base):
        return jnp.concatenate([
            pad_lora(t, base, DECAY_LORA, LANES),
            pad_lora(t, base + DECAY_LORA, ICL_LORA, LANES),
            pad_lora(t, base + DECAY_LORA + ICL_LORA, GATE_LORA, 2 * LANES)], axis=-1)

    mu_row = mu.reshape(1, -1)
    mu_pad = jnp.concatenate([mu_row[:, :3 * MIX_W], lora_cols(mu_row, 3 * MIX_W)], axis=-1)
    pad_rows = lambda t, n: jnp.pad(t, ((0, n - t.shape[0]), (0, 0))).astype(BF16)
    row = lambda t: t.reshape(1, -1).astype(F32)
    ones_blk = ((jnp.arange(LANES)[:, None] // RWKV_HEAD_DIM) ==
                (jnp.arange(LANES)[None, :] // RWKV_HEAD_DIM)).astype(BF16)
    ones_blk = jnp.concatenate([ones_blk] * 2, axis=0)
    t_idx = jnp.arange(tm)
    same_chunk = (t_idx[:, None] // RWKV_CHUNK) == (t_idx[None, :] // RWKV_CHUNK)
    sel = (same_chunk & (t_idx[:, None] >= t_idx[None, :])).astype(BF16)

    operands = [
        (x, pl.BlockSpec((2 * tm, d), lambda i: (i, 0))),
        (x, _next_tile_spec(tm, d, t_len // tm)),
        (row(norm_g), None),
        _layer_operand(jnp.swapaxes(w_in_all, 1, 2), j),
        (lru_conv_w, None), (row(lru_conv_b), None),
        (lru_wa.astype(BF16), None), (row(lru_ba), None),
        (lru_wx.astype(BF16), None), (row(lru_bx), None),
        (row(lru_lambda), None),
        (mu_pad, None),
        (row(w0), None), (pad_rows(w2, LANES), None),
        (row(a0), None), (pad_rows(a2, LANES), None),
        (pad_rows(g2, 2 * LANES), None),
        (row(k_k), None), (row(k_a), None), (row(r_k), None), (row(ln_g), None), (row(ln_b), None),
        (ones_blk, None), (sel, None),
        _layer_operand(w_out_all, j),
    ]
    args = [a for a, _ in operands]
    specs = [s if s is not None else _resident_spec(a.shape) for a, s in operands]
    return pl.pallas_call(
        functools.partial(_cd_body, tm=tm),
        grid=(t_len // (2 * tm),),
        in_specs=specs,
        out_specs=pl.BlockSpec((2 * tm, d), lambda i: (i, 0)),
        out_shape=jax.ShapeDtypeStruct((t_len, d), F32),
        scratch_shapes=[
            pltpu.VMEM((SUBLANES, MIX_W), F32),
            pltpu.VMEM((1, MIX_W), F32),
            pltpu.VMEM((1, CD_PR_COLS), F32),
            pltpu.VMEM((MIX_W // LANES, LANES, LANES), F32),
            pltpu.VMEM((CD_LORA, d), BF16),
            pltpu.VMEM((CD_PR + CD_PR_COLS - CD_LORA, d), BF16),
            pltpu.VMEM((2 * MIX_W, d), BF16),
            pltpu.VMEM((tm, CD_PR + CD_PR_COLS), F32),
            pltpu.VMEM((tm, CD_PR + CD_PR_COLS), F32),
        ],
        compiler_params=pltpu.CompilerParams(
            dimension_semantics=("arbitrary",), vmem_limit_bytes=VMEM_LIMIT),
        name="mix_lru_rwkv",
    )(*args)


def kernel(x, ffn1_norm, ffn1_wg, ffn1_wu, ffn1_wd, mix_norm, ffn2_norm, ffn2_wg, ffn2_wu, ffn2_wd,
           ab_w_in, ab_w_out, ssm_conv_w, ssm_conv_b, ssm_dt_bias, ssm_a_log, ssm_d, ssm_norm,
           cd_w_in, cd_w_out, lru_conv_w, lru_conv_b, lru_wa, lru_ba, lru_wx, lru_bx, lru_lambda,
           rwkv_mu, rwkv_w0, rwkv_w2, rwkv_a0, rwkv_a2, rwkv_g2, rwkv_kk, rwkv_ka, rwkv_rk,
           rwkv_ln_g, rwkv_ln_b, final_norm):
    bsz, t_len, d = x.shape
    depth = ffn1_norm.shape[0]
    outs = []
    for b in range(bsz):
        xb = x.reshape(t_len, d) if bsz == 1 else x[b]
        for layer in range(depth):
            j = layer // 2
            xb = _ffn(xb, ffn1_norm[layer], ffn1_wg, ffn1_wu, ffn1_wd, layer, final_norm, final_norm=False)
            if layer % 2 == 0:
                xb = _mix_ab(xb, mix_norm[layer], ab_w_in, ab_w_out, j, ssm_conv_w[j], ssm_conv_b[j],
                             ssm_dt_bias[j], ssm_a_log[j], ssm_d[j], ssm_norm[j])
            else:
                xb = _mix_cd(xb, mix_norm[layer], cd_w_in, cd_w_out, j, lru_conv_w[j], lru_conv_b[j], lru_wa[j],
                             lru_ba[j], lru_wx[j], lru_bx[j], lru_lambda[j], rwkv_mu[j], rwkv_w0[j],
                             rwkv_w2[j], rwkv_a0[j], rwkv_a2[j], rwkv_g2[j], rwkv_kk[j], rwkv_ka[j],
                             rwkv_rk[j].reshape(-1), rwkv_ln_g[j], rwkv_ln_b[j])
            xb = _ffn(xb, ffn2_norm[layer], ffn2_wg, ffn2_wu, ffn2_wd, layer, final_norm,
                      final_norm=(layer == depth - 1))
        outs.append(xb)
    return outs[0].reshape(1, t_len, d) if bsz == 1 else jnp.stack(outs, axis=0)
```
